```python
import math
import jax, jax.numpy as jnp
from jax import lax
import numpy as np

D_MODEL = 2048
BATCH = 4
SEQ = 4096
DEPTH = 2

D_BR = D_MODEL // 2
N_BRANCHES = 4
GMLP_CHUNK = 128
GMLP_GROUPS = 8
GMLP_GROUP_DIM = D_BR // GMLP_GROUPS
LRU_BLOCKS = 8
LRU_BLOCK_DIM = D_BR // LRU_BLOCKS
CONV_WIDTH = 4
LRU_C = 8.0
NSA_HEAD_DIM = 64
NSA_HEADS = D_BR // NSA_HEAD_DIM
NSA_KV_HEADS = NSA_HEADS // 4
NSA_GROUP = NSA_HEADS // NSA_KV_HEADS
NSA_KV_W = NSA_KV_HEADS * NSA_HEAD_DIM
CMP_BLOCK = 32
CMP_STRIDE = 16
CMP_HIDDEN = 256
SLC_BLOCK = 64
SLC_TOPK = 8
WINDOW = 256
Q_BLOCK = 128
MEM_LEN = 256
MEM_HEADS = 4
MEM_HEAD_DIM = D_BR // MEM_HEADS
REL_BUCKETS = 32
REL_MAX_DIST = 1024
DN_ALPHA = (2 * DEPTH) ** 0.25
DN_BETA = (8 * DEPTH) ** -0.25
LN_EPS = 1e-5
IN_WIDTH = 9 * D_BR + 6 * NSA_KV_W + 3 * NSA_HEADS + N_BRANCHES * D_MODEL

kernel_name = "hybrid_gated_gmlp_rglru_nsa_mem_deepnorm"


def in_split_points():
    sizes = (D_BR, D_BR, D_BR, D_BR, D_BR, D_BR,
             NSA_KV_W, NSA_KV_W, NSA_KV_W, NSA_KV_W, NSA_KV_W, NSA_KV_W,
             3 * NSA_HEADS, D_BR, D_BR, D_BR, N_BRANCHES * D_MODEL)
    return tuple(int(v) for v in np.cumsum(sizes)[:-1])


def layer_norm(x, g, b):
    xf = x.astype(jnp.float32)
    mu = jnp.mean(xf, -1, keepdims=True)
    var = jnp.mean(jnp.square(xf - mu), -1, keepdims=True)
    return ((xf - mu) * lax.rsqrt(var + LN_EPS) * g + b).astype(x.dtype)


def masked_softmax(s, mask):
    s = jnp.where(mask, s.astype(jnp.float32), -1e30)
    m = jnp.max(s, axis=-1, keepdims=True)
    p = jnp.where(mask, jnp.exp(s - m), 0.0)
    return p / jnp.maximum(jnp.sum(p, -1, keepdims=True), 1e-30)


def rel_bucket(dist):
    n = jnp.maximum(dist, 0)
    exact = REL_BUCKETS // 2
    nf = jnp.maximum(n, 1).astype(jnp.float32)
    large = exact + (jnp.log(nf / exact) / math.log(REL_MAX_DIST / exact)
                     * (REL_BUCKETS - exact)).astype(jnp.int32)
    return jnp.where(n < exact, n, jnp.minimum(large, REL_BUCKETS - 1))


def gmlp_spatial_gating(u, v, ln_g, ln_b, w_s, b_s):
    bsz, seq, _ = u.shape
    u = jax.nn.gelu(u)
    v = layer_norm(jax.nn.gelu(v), ln_g, ln_b)
    vc = v.reshape(bsz, seq // GMLP_CHUNK, GMLP_CHUNK, GMLP_GROUPS, GMLP_GROUP_DIM)
    causal = jnp.tril(jnp.ones((GMLP_CHUNK, GMLP_CHUNK), dtype=bool))
    w = jnp.where(causal, w_s, 0).astype(v.dtype)
    mixed = jnp.einsum('gts,bcsgd->bctgd', w, vc) + jnp.swapaxes(b_s, 0, 1)[:, :, None]
    return u * mixed.reshape(bsz, seq, D_BR)


def causal_depthwise_conv(x, w, b):
    out = lax.conv_general_dilated(
        x, w[:, None, :].astype(x.dtype), window_strides=(1,),
        padding=[(CONV_WIDTH - 1, 0)], dimension_numbers=('NWC', 'WIO', 'NWC'),
        feature_group_count=x.shape[-1])
    return out + b


def block_diag_linear(x, w, b):
    bsz, seq, _ = x.shape
    xb = x.reshape(bsz, seq, LRU_BLOCKS, LRU_BLOCK_DIM)
    return jnp.einsum('bsnd,nde->bsne', xb, w).reshape(bsz, seq, D_BR) + b


def rg_lru(x, wa, ba, wx, bx, lam):
    f32 = jnp.float32
    r = jax.nn.sigmoid(block_diag_linear(x, wa, ba).astype(f32))
    i = jax.nn.sigmoid(block_diag_linear(x, wx, bx).astype(f32))
    log_a = -LRU_C * r * jax.nn.softplus(-lam.astype(f32))
    a = jnp.exp(log_a)
    gated = jnp.sqrt(-jnp.expm1(2.0 * log_a)) * i * x.astype(f32)

    def combine(lhs, rhs):
        a1, b1 = lhs
        a2, b2 = rhs
        return a1 * a2, a2 * b1 + b2

    _, h = lax.associative_scan(combine, (a, gated), axis=1)
    return h.astype(x.dtype)


def nsa_attention(q, k_c, v_c, k_s, v_s, k_w, v_w, gate_logits, gate_b, rel_table,
                  pe_k, pe_v, w1_k, w1_v, w2_k, w2_v):
    f32 = jnp.float32
    bsz, seq, _ = q.shape
    dk = NSA_HEAD_DIM
    pos = jnp.arange(seq)
    tab = rel_table.reshape(REL_BUCKETS, NSA_KV_HEADS, NSA_GROUP)

    def heads_kv(t):
        return t.reshape(bsz, seq, NSA_KV_HEADS, dk).transpose(0, 2, 1, 3)

    qh = q.reshape(bsz, seq, NSA_KV_HEADS, NSA_GROUP, dk).transpose(0, 2, 3, 1, 4) * (dk ** -0.5)
    k_c, v_c, k_s, v_s, k_w, v_w = (heads_kv(t) for t in (k_c, v_c, k_s, v_s, k_w, v_w))

    def compress(t, pe, w1, w2):
        t16 = t.reshape(bsz, NSA_KV_HEADS, seq // CMP_STRIDE, CMP_STRIDE, dk)
        blocks = jnp.concatenate([t16[:, :, :-1], t16[:, :, 1:]], axis=3) + pe
        flat = blocks.reshape(blocks.shape[:3] + (CMP_BLOCK * dk,))
        return jax.nn.silu(flat @ w1) @ w2

    kc = compress(k_c, pe_k, w1_k, w2_k)
    vc = compress(v_c, pe_v, w1_v, w2_v)
    n_cmp = kc.shape[2]
    c_start = jnp.arange(n_cmp) * CMP_STRIDE
    dist_c = pos[:, None] - (c_start + CMP_BLOCK - 1)[None, :]
    bias_c = jnp.moveaxis(tab[rel_bucket(dist_c)], (-2, -1), (0, 1))
    s_c = jnp.einsum('bhgsd,bhnd->bhgsn', qh, kc).astype(f32) + bias_c
    p_c = masked_softmax(s_c, dist_c >= 0)
    o_c = jnp.einsum('bhgsn,bhnd->bhgsd', p_c.astype(vc.dtype), vc)

    n_slc = seq // SLC_BLOCK
    s_start = jnp.arange(n_slc) * SLC_BLOCK
    overlap = jnp.clip(
        jnp.minimum(c_start[:, None] + CMP_BLOCK, s_start[None, :] + SLC_BLOCK)
        - jnp.maximum(c_start[:, None], s_start[None, :]), 0, None).astype(f32) / CMP_BLOCK
    imp = jnp.einsum('bhgsn,nj->bhsj', p_c, overlap)
    q_blk = pos // SLC_BLOCK
    j = jnp.arange(n_slc)
    forced = (j[None, :] == 0) | (j[None, :] == q_blk[:, None]) | (j[None, :] == q_blk[:, None] - 1)
    future = j[None, :] > q_blk[:, None]
    imp = jnp.where(future, -jnp.inf, jnp.where(forced, jnp.inf, imp))
    top_k = min(SLC_TOPK, n_slc)
    _, sel = lax.top_k(imp, top_k)

    k_sb = k_s.reshape(bsz, NSA_KV_HEADS, n_slc, SLC_BLOCK, dk)
    v_sb = v_s.reshape(bsz, NSA_KV_HEADS, n_slc, SLC_BLOCK, dk)
    k_wp = jnp.pad(k_w, ((0, 0), (0, 0), (WINDOW, 0), (0, 0)))
    v_wp = jnp.pad(v_w, ((0, 0), (0, 0), (WINDOW, 0), (0, 0)))
    n_qb = seq // Q_BLOCK
    b_ix = jnp.arange(bsz)[:, None, None, None]
    h_ix = jnp.arange(NSA_KV_HEADS)[None, :, None, None]
    tab_kv = jnp.transpose(tab, (1, 0, 2))
    offs = jnp.arange(SLC_BLOCK)
    win_offs = jnp.arange(WINDOW + Q_BLOCK)

    def block_step(args):
        qb, q_b, sel_b = args
        tq = qb * Q_BLOCK + jnp.arange(Q_BLOCK)
        ks = k_sb[b_ix, h_ix, sel_b]
        vs = v_sb[b_ix, h_ix, sel_b]
        kpos = sel_b[..., None] * SLC_BLOCK + offs
        dist = tq[:, None, None] - kpos
        bias = tab_kv[h_ix[..., None], rel_bucket(dist)]
        bias = jnp.moveaxis(bias, -1, 2).reshape(bsz, NSA_KV_HEADS, NSA_GROUP, Q_BLOCK, top_k * SLC_BLOCK)
        s = jnp.einsum('bhgqd,bhqnkd->bhgqnk', q_b, ks).astype(f32).reshape(
            bsz, NSA_KV_HEADS, NSA_GROUP, Q_BLOCK, top_k * SLC_BLOCK) + bias
        mask = (dist >= 0).reshape(bsz, NSA_KV_HEADS, 1, Q_BLOCK, top_k * SLC_BLOCK)
        p = masked_softmax(s, mask).reshape(bsz, NSA_KV_HEADS, NSA_GROUP, Q_BLOCK, top_k, SLC_BLOCK)
        o_s = jnp.einsum('bhgqnk,bhqnkd->bhgqd', p.astype(vs.dtype), vs)

        kw = lax.dynamic_slice_in_dim(k_wp, qb * Q_BLOCK, WINDOW + Q_BLOCK, axis=2)
        vw = lax.dynamic_slice_in_dim(v_wp, qb * Q_BLOCK, WINDOW + Q_BLOCK, axis=2)
        kpos_w = qb * Q_BLOCK - WINDOW + win_offs
        dist_w = tq[:, None] - kpos_w[None, :]
        mask_w = (dist_w >= 0) & (dist_w < WINDOW) & (kpos_w[None, :] >= 0)
        bias_w = jnp.moveaxis(tab[rel_bucket(dist_w)], (-2, -1), (0, 1))
        s_w = jnp.einsum('bhgqd,bhkd->bhgqk', q_b, kw).astype(f32) + bias_w
        p_w = masked_softmax(s_w, mask_w)
        o_w = jnp.einsum('bhgqk,bhkd->bhgqd', p_w.astype(vw.dtype), vw)
        return o_s, o_w

    q_blocks = jnp.moveaxis(qh.reshape(bsz, NSA_KV_HEADS, NSA_GROUP, n_qb, Q_BLOCK, dk), 3, 0)
    sel_blocks = jnp.moveaxis(sel.reshape(bsz, NSA_KV_HEADS, n_qb, Q_BLOCK, top_k), 2, 0)
    o_s, o_w = lax.map(block_step, (jnp.arange(n_qb), q_blocks, sel_blocks))
    o_s = jnp.moveaxis(o_s, 0, 3).reshape(bsz, NSA_KV_HEADS, NSA_GROUP, seq, dk)
    o_w = jnp.moveaxis(o_w, 0, 3).reshape(bsz, NSA_KV_HEADS, NSA_GROUP, seq, dk)

    g = jax.nn.sigmoid(gate_logits.astype(f32) + gate_b)
    g = g.reshape(bsz, seq, 3, NSA_KV_HEADS, NSA_GROUP).transpose(2, 0, 3, 4, 1)[..., None]
    o = g[0] * o_c + g[1] * o_s + g[2] * o_w
    return o.transpose(0, 3, 1, 2, 4).reshape(bsz, seq, D_BR).astype(q.dtype)


def memory_attention(q, mem, w_kv):
    bsz, seq, _ = q.shape
    k, v = jnp.split(mem @ w_kv, 2, axis=-1)
    k = k.reshape(bsz, -1, MEM_HEADS, MEM_HEAD_DIM)
    v = v.reshape(bsz, -1, MEM_HEADS, MEM_HEAD_DIM)
    qh = q.reshape(bsz, seq, MEM_HEADS, MEM_HEAD_DIM) * (MEM_HEAD_DIM ** -0.5)
    p = jax.nn.softmax(jnp.einsum('bshd,bmhd->bhsm', qh, k).astype(jnp.float32), axis=-1)
    return jnp.einsum('bhsm,bmhd->bshd', p.astype(v.dtype), v).reshape(bsz, seq, D_BR)


def setup_inputs(seed: int = 0) -> dict:
    key = jax.random.key(seed)
    ks = jax.random.split(key, 32)
    f32 = jnp.float32
    dk = NSA_HEAD_DIM

    def nrm(k, shape, scale):
        return jax.random.normal(k, shape, f32) * scale

    a_pow = jax.random.uniform(ks[14], (DEPTH, D_BR), f32, 0.9, 0.999)
    a_base = a_pow ** (1.0 / LRU_C)
    lru_lambda = jnp.log(a_base) - jnp.log1p(-a_base)
    return {
        "x": nrm(ks[0], (BATCH, SEQ, D_MODEL), 1.0),
        "mem": nrm(ks[1], (BATCH, MEM_LEN, D_MODEL), 1.0),
        "rel_bias": nrm(ks[2], (REL_BUCKETS, NSA_HEADS), 0.5),
        "w_in": nrm(ks[3], (DEPTH, D_MODEL, IN_WIDTH), D_MODEL ** -0.5),
        "sgu_ln_g": 1.0 + nrm(ks[4], (DEPTH, D_BR), 0.01),
        "sgu_ln_b": nrm(ks[5], (DEPTH, D_BR), 0.01),
        "sgu_w": nrm(ks[6], (DEPTH, GMLP_GROUPS, GMLP_CHUNK, GMLP_CHUNK), GMLP_CHUNK ** -0.5),
        "sgu_b": 1.0 + nrm(ks[7], (DEPTH, GMLP_GROUPS, GMLP_CHUNK), 0.01),
        "conv_w": nrm(ks[8], (DEPTH, CONV_WIDTH, D_BR), CONV_WIDTH ** -0.5),
        "conv_b": nrm(ks[9], (DEPTH, D_BR), 0.01),
        "lru_wa": nrm(ks[10], (DEPTH, LRU_BLOCKS, LRU_BLOCK_DIM, LRU_BLOCK_DIM), LRU_BLOCK_DIM ** -0.5),
        "lru_ba": nrm(ks[11], (DEPTH, D_BR), 0.01),
        "lru_wx": nrm(ks[12], (DEPTH, LRU_BLOCKS, LRU_BLOCK_DIM, LRU_BLOCK_DIM), LRU_BLOCK_DIM ** -0.5),
        "lru_bx": nrm(ks[13], (DEPTH, D_BR), 0.01),
        "lru_lambda": lru_lambda,
        "cmp_pe_k": nrm(ks[15], (DEPTH, CMP_BLOCK, dk), 0.1),
        "cmp_pe_v": nrm(ks[16], (DEPTH, CMP_BLOCK, dk), 0.1),
        "cmp_w1_k": nrm(ks[17], (DEPTH, CMP_BLOCK * dk, CMP_HIDDEN), (CMP_BLOCK * dk) ** -0.5),
        "cmp_w1_v": nrm(ks[18], (DEPTH, CMP_BLOCK * dk, CMP_HIDDEN), (CMP_BLOCK * dk) ** -0.5),
        "cmp_w2_k": nrm(ks[19], (DEPTH, CMP_HIDDEN, dk), CMP_HIDDEN ** -0.5),
        "cmp_w2_v": nrm(ks[20], (DEPTH, CMP_HIDDEN, dk), CMP_HIDDEN ** -0.5),
        "nsa_gate_b": nrm(ks[21], (DEPTH, 3 * NSA_HEADS), 0.01),
        "w_mem_kv": nrm(ks[22], (DEPTH, D_MODEL, 2 * D_BR), D_MODEL ** -0.5),
        "w_branch": nrm(ks[23], (DEPTH, N_BRANCHES, D_BR, D_MODEL), DN_BETA * D_BR ** -0.5),
        "w_out": nrm(ks[24], (DEPTH, D_MODEL, D_MODEL), DN_BETA * D_MODEL ** -0.5),
        "ln_g": 1.0 + nrm(ks[25], (DEPTH, D_MODEL), 0.01),
        "ln_b": nrm(ks[26], (DEPTH, D_MODEL), 0.01),
    }


def reference(x, mem, rel_bias, w_in, sgu_ln_g, sgu_ln_b, sgu_w, sgu_b, conv_w, conv_b,
              lru_wa, lru_ba, lru_wx, lru_bx, lru_lambda, cmp_pe_k, cmp_pe_v, cmp_w1_k, cmp_w1_v,
              cmp_w2_k, cmp_w2_v, nsa_gate_b, w_mem_kv, w_branch, w_out, ln_g, ln_b):
    bsz, seq, _ = x.shape
    split_points = in_split_points()
    for l in range(DEPTH):
        h = x @ w_in[l]
        (u_a, v_a, g_a, x_b, g_b, q_c, kc_c, vc_c, ks_c, vs_c, kw_c, vw_c, gl_c, g_c,
         q_m, g_m, g_merge) = jnp.split(h, split_points, axis=-1)

        o_a = gmlp_spatial_gating(u_a, v_a, sgu_ln_g[l], sgu_ln_b[l], sgu_w[l], sgu_b[l]) * jax.nn.silu(g_a)
        x_conv = causal_depthwise_conv(x_b, conv_w[l], conv_b[l])
        o_b = rg_lru(x_conv, lru_wa[l], lru_ba[l], lru_wx[l], lru_bx[l], lru_lambda[l]) * jax.nn.silu(g_b)
        o_c = nsa_attention(q_c, kc_c, vc_c, ks_c, vs_c, kw_c, vw_c, gl_c, nsa_gate_b[l], rel_bias,
                            cmp_pe_k[l], cmp_pe_v[l], cmp_w1_k[l], cmp_w1_v[l],
                            cmp_w2_k[l], cmp_w2_v[l]) * jax.nn.silu(g_c)
        o_m = memory_attention(q_m, mem, w_mem_kv[l]) * jax.nn.silu(g_m)

        gates = jax.nn.sigmoid(g_merge.reshape(bsz, seq, N_BRANCHES, D_MODEL))
        merged = gates[:, :, 0] * (o_a @ w_branch[l, 0])
        merged = merged + gates[:, :, 1] * (o_b @ w_branch[l, 1])
        merged = merged + gates[:, :, 2] * (o_c @ w_branch[l, 2])
        merged = merged + gates[:, :, 3] * (o_m @ w_branch[l, 3])
        y = merged @ w_out[l]
        x = layer_norm(DN_ALPHA * x + y, ln_g[l], ln_b[l])
    return x
```

```python
import functools
import math

import numpy as np
import jax
import jax.numpy as jnp
from jax import lax
from jax.experimental import pallas as pl
from jax.experimental.pallas import tpu as pltpu

F32 = jnp.float32
BF16 = jnp.bfloat16

D_MODEL = 2048
DEPTH = 2
D_BR = D_MODEL // 2
N_BRANCHES = 4
GMLP_CHUNK = 128
GMLP_GROUPS = 8
LRU_BLOCKS = 8
LRU_BLOCK_DIM = D_BR // LRU_BLOCKS
CONV_WIDTH = 4
LRU_C = 8.0
DK = 64
NSA_HEADS = D_BR // DK
HKV = NSA_HEADS // 4
GQA = NSA_HEADS // HKV
KV_W = HKV * DK
CMP_BLOCK = 32
CMP_STRIDE = 16
CMP_HIDDEN = 256
SLC_BLOCK = 64
SLC_TOPK = 8
WINDOW = 256
Q_BLOCK = 128
MEM_HEADS = 4
MEM_HEAD_DIM = D_BR // MEM_HEADS
REL_BUCKETS = 32
REL_MAX_DIST = 1024
DN_ALPHA = (2 * DEPTH) ** 0.25
LN_EPS = 1e-5

NEG = -1e30
SEL_TK = 256
SEL_ND = 10
MASK_ROWS = 64
VMEM_LIMIT = 56 * 1024 * 1024

C_MERGE = 0
C_U, C_V, C_GA, C_XB, C_GB, C_QC, C_GC, C_QM, C_GM = (8192 + 1024 * k for k in range(9))
C_KV = 8192 + 9 * 1024
N_MAIN = C_KV + 6 * KV_W


def _sigmoid(x):
    return 1.0 / (1.0 + jnp.exp(-x))


def _silu(x):
    return x * _sigmoid(x)


def _gelu_tanh(x):
    return 0.5 * x * (1.0 + jnp.tanh(math.sqrt(2.0 / math.pi) * (x + 0.044715 * (x * x * x))))


def _dot(a, b):
    return jnp.dot(a, b, preferred_element_type=F32)


def _dot_nt(a, b):
    return lax.dot_general(a, b, (((1,), (1,)), ((), ())), preferred_element_type=F32)


def _params(**kw):
    return pltpu.CompilerParams(vmem_limit_bytes=VMEM_LIMIT, **kw)


def _mm_kernel(x_ref, w_ref, o_ref):
    o_ref[...] = _dot(x_ref[...], w_ref[...]).astype(o_ref.dtype)


def matmul(x, w, out_dtype, tm, tn):
    m, k = x.shape
    n = w.shape[1]
    assert m % tm == 0 and n % tn == 0, (m, n, tm, tn)
    return pl.pallas_call(
        _mm_kernel,
        grid=(n // tn, m // tm),
        in_specs=[pl.BlockSpec((tm, k), lambda j, i: (i, 0)),
                  pl.BlockSpec((k, tn), lambda j, i: (0, j))],
        out_specs=pl.BlockSpec((tm, tn), lambda j, i: (i, j)),
        out_shape=jax.ShapeDtypeStruct((m, n), out_dtype),
        compiler_params=_params(),
        name="matmul",
    )(x, w)


def _gmlp_kernel(u_ref, v_ref, ga_ref, lng_ref, lnb_ref, w_ref, bs_ref, o_ref, *, rows):
    gd = D_BR // GMLP_GROUPS
    for c in range(rows // GMLP_CHUNK):
        r = slice(c * GMLP_CHUNK, (c + 1) * GMLP_CHUNK)
        v = _gelu_tanh(v_ref[r, :])
        mu = jnp.mean(v, axis=-1, keepdims=True)
        vc = v - mu
        var = jnp.mean(vc * vc, axis=-1, keepdims=True)
        vb = (vc * lax.rsqrt(var + LN_EPS) * lng_ref[...] + lnb_ref[...]).astype(BF16)
        u = _gelu_tanh(u_ref[r, :]) * _silu(ga_ref[r, :])
        for g in range(GMLP_GROUPS):
            cs = slice(g * gd, (g + 1) * gd)
            mixed = _dot(w_ref[g], vb[:, cs]) + bs_ref[:, g:g + 1]
            o_ref[r, cs] = (u[:, cs] * mixed).astype(o_ref.dtype)


def gmlp_branch(h, ln_g, ln_b, w_s, b_s, rows=512):
    m = h.shape[0]
    rows = min(rows, m)
    causal = jnp.tril(jnp.ones((GMLP_CHUNK, GMLP_CHUNK), dtype=bool))
    w = jnp.where(causal, w_s, 0).astype(BF16)
    col = lambda c: pl.BlockSpec((rows, D_BR), lambda i, c=c: (i, c // D_BR))
    full = lambda a: pl.BlockSpec(a.shape, lambda i: (0,) * a.ndim)
    args = (ln_g.reshape(1, D_BR), ln_b.reshape(1, D_BR), w, b_s.T)
    return pl.pallas_call(
        functools.partial(_gmlp_kernel, rows=rows),
        grid=(m // rows,),
        in_specs=[col(C_U), col(C_V), col(C_GA)] + [full(a) for a in args],
        out_specs=pl.BlockSpec((rows, D_BR), lambda i: (i, 0)),
        out_shape=jax.ShapeDtypeStruct((m, D_BR), BF16),
        compiler_params=_params(),
        name="gmlp",
    )(h, h, h, *args)


def _lru_kernel(xb_ref, gb_ref, cw_ref, cb_ref, wa_ref, ba_ref, wx_ref, bx_ref, lam_ref, o_ref,
                xbuf, hcarry, a_s, g_s, *, ts):
    @pl.when(pl.program_id(1) == 0)
    def _():
        xbuf[0:8, :] = jnp.zeros((8, D_BR), F32)
        hcarry[...] = jnp.zeros((8, D_BR), F32)

    xbuf[8:8 + ts, :] = xb_ref[...]
    xc = cb_ref[...] + cw_ref[0:1, :] * xbuf[pl.ds(8 - (CONV_WIDTH - 1), ts), :]
    for k in range(1, CONV_WIDTH):
        xc = xc + cw_ref[k:k + 1, :] * xbuf[pl.ds(8 - (CONV_WIDTH - 1) + k, ts), :]
    xbuf[0:8, :] = xbuf[ts:ts + 8, :]

    xcb = xc.astype(BF16)
    bd = LRU_BLOCK_DIM
    r = jnp.concatenate([_dot(xcb[:, n * bd:(n + 1) * bd], wa_ref[n]) for n in range(LRU_BLOCKS)], axis=1)
    i = jnp.concatenate([_dot(xcb[:, n * bd:(n + 1) * bd], wx_ref[n]) for n in range(LRU_BLOCKS)], axis=1)
    r = _sigmoid(r + ba_ref[...])
    i = _sigmoid(i + bx_ref[...])
    nl = -lam_ref[...]
    softplus = jnp.maximum(nl, 0.0) + jnp.log1p(jnp.exp(-jnp.abs(nl)))
    log_a = (-LRU_C * softplus) * r
    a = jnp.exp(log_a)
    a_s[...] = a
    g_s[...] = jnp.sqrt(1.0 - a * a) * i * xc

    row = lax.broadcasted_iota(jnp.int32, (8, D_BR), 0)

    def body(j, carry):
        r0 = pl.multiple_of(j * 8, 8)
        av = a_s[pl.ds(r0, 8), :]
        bv = g_s[pl.ds(r0, 8), :]
        for d in (1, 2, 4):
            keep = row >= d
            a_sh = pltpu.roll(av, d, axis=0)
            b_sh = pltpu.roll(bv, d, axis=0)
            bv = jnp.where(keep, av * b_sh + bv, bv)
            av = jnp.where(keep, av * a_sh, av)
        hv = av * carry + bv
        g_s[pl.ds(r0, 8), :] = hv
        return jnp.broadcast_to(hv[7:8, :], (8, D_BR))

    hcarry[...] = lax.fori_loop(0, ts // 8, body, hcarry[...])
    o_ref[...] = (g_s[...] * _silu(gb_ref[...])).astype(o_ref.dtype)


def lru_branch(h, bsz, seq, conv_w, conv_b, wa, ba, wx, bx, lam, ts=512):
    ts = min(ts, seq)
    ns = seq // ts
    col = lambda c: pl.BlockSpec((ts, D_BR), lambda b, s, c=c: (b * ns + s, c // D_BR))
    full = lambda a: pl.BlockSpec(a.shape, lambda b, s: (0,) * a.ndim)
    row = lambda a: a.reshape(1, D_BR)
    args = (conv_w, row(conv_b), wa.astype(BF16), row(ba), wx.astype(BF16), row(bx), row(lam))
    return pl.pallas_call(
        functools.partial(_lru_kernel, ts=ts),
        grid=(bsz, ns),
        in_specs=[col(C_XB), col(C_GB)] + [full(a) for a in args],
        out_specs=pl.BlockSpec((ts, D_BR), lambda b, s: (b * ns + s, 0)),
        out_shape=jax.ShapeDtypeStruct((bsz * seq, D_BR), BF16),
        scratch_shapes=[pltpu.VMEM((ts + 8, D_BR), F32), pltpu.VMEM((8, D_BR), F32),
                        pltpu.VMEM((ts, D_BR), F32), pltpu.VMEM((ts, D_BR), F32)],
        compiler_params=_params(dimension_semantics=("arbitrary", "arbitrary")),
        name="lru",
    )(h, h, *args)


def _mem_kernel(q_ref, g_ref, kv_ref, o_ref):
    hd = MEM_HEAD_DIM
    for hh in range(MEM_HEADS):
        cs = slice(hh * hd, (hh + 1) * hd)
        q = (q_ref[:, cs] * (hd ** -0.5)).astype(BF16)
        s = _dot_nt(q, kv_ref[:, cs])
        p = jnp.exp(s - jnp.max(s, axis=-1, keepdims=True))
        l = jnp.sum(p, axis=-1, keepdims=True)
        o = _dot(p.astype(BF16), kv_ref[:, D_BR + hh * hd:D_BR + (hh + 1) * hd]) / l
        o_ref[:, cs] = (o * _silu(g_ref[:, cs])).astype(o_ref.dtype)


def mem_branch(h, kv, bsz, seq, tq=512):
    tq = min(tq, seq)
    nq = seq // tq
    mlen = kv.shape[0] // bsz
    col = lambda c: pl.BlockSpec((tq, D_BR), lambda b, i, c=c: (b * nq + i, c // D_BR))
    return pl.pallas_call(
        _mem_kernel,
        grid=(bsz, nq),
        in_specs=[col(C_QM), col(C_GM), pl.BlockSpec((mlen, 2 * D_BR), lambda b, i: (b, 0))],
        out_specs=pl.BlockSpec((tq, D_BR), lambda b, i: (b * nq + i, 0)),
        out_shape=jax.ShapeDtypeStruct((bsz * seq, D_BR), BF16),
        compiler_params=_params(),
        name="mem_attn",
    )(h, h, kv)


def _cmp_kernel(tk_ref, tv_ref, pek_ref, pev_ref, w1k_ref, w1v_ref, w2kt_ref, w2v_ref, kct_ref, vc_ref):
    n = tk_ref.shape[1]

    def hidden(t_ref, pe_ref, w1_ref):
        t = t_ref[0]
        lo = _dot((t + pe_ref[0:1, :]).astype(BF16), w1_ref[0])
        hi = _dot((t + pe_ref[1:2, :]).astype(BF16), w1_ref[1])
        pre = lo + pltpu.roll(hi, n - 1, axis=0)
        return _silu(pre).astype(BF16)

    kct_ref[0] = _dot_nt(w2kt_ref[...], hidden(tk_ref, pek_ref, w1k_ref)).astype(kct_ref.dtype)
    vc_ref[0] = _dot(hidden(tv_ref, pev_ref, w1v_ref), w2v_ref[...]).astype(vc_ref.dtype)


def nsa_compress(tk, tv, pe_k, pe_v, w1_k, w1_v, w2_k, w2_v):
    nb, n, width = tk.shape
    half = CMP_STRIDE * DK
    args = (pe_k.reshape(2, half), pe_v.reshape(2, half),
            w1_k.reshape(2, half, CMP_HIDDEN).astype(BF16), w1_v.reshape(2, half, CMP_HIDDEN).astype(BF16),
            w2_k.T.astype(BF16), w2_v.astype(BF16))
    full = lambda a: pl.BlockSpec(a.shape, lambda b: (0,) * a.ndim)
    blk = pl.BlockSpec((1, n, width), lambda b: (b, 0, 0))
    return pl.pallas_call(
        _cmp_kernel,
        grid=(nb,),
        in_specs=[blk, blk] + [full(a) for a in args],
        out_specs=[pl.BlockSpec((1, DK, n), lambda b: (b, 0, 0)), pl.BlockSpec((1, n, DK), lambda b: (b, 0, 0))],
        out_shape=[jax.ShapeDtypeStruct((nb, DK, n), BF16), jax.ShapeDtypeStruct((nb, n, DK), BF16)],
        compiler_params=_params(),
        name="nsa_compress",
    )(tk, tv, *args)


def _nsa_kernel(q_ref, gc_ref, gl_ref, gb_ref, kct_ref, vc_ref, bc_ref, ov_ref,
                ks_ref, vs_ref, bt_ref, kw_ref, vw_ref, bw_ref, o_ref, *, n_cmp_pad):
    i = pl.program_id(2)
    qb = Q_BLOCK
    rows = GQA * qb
    q = (q_ref[...] * (DK ** -0.5)).astype(BF16)
    qst = jnp.concatenate([q[:, g * DK:(g + 1) * DK] for g in range(GQA)], axis=0)

    def softmax_rows(s):
        m = jnp.max(s, axis=-1, keepdims=True)
        p = jnp.exp(s - m)
        return p, jnp.sum(p, axis=-1, keepdims=True)

    s = _dot(qst, kct_ref[0]) + bc_ref[...].reshape(rows, n_cmp_pad)
    m = jnp.max(s, axis=-1, keepdims=True)
    p = jnp.where(s > 0.1 * NEG, jnp.exp(s - m), 0.0)
    p = p / jnp.maximum(jnp.sum(p, axis=-1, keepdims=True), 1e-30)
    o_c = _dot(p.astype(BF16), vc_ref[0])

    psum = p[0:qb]
    for g in range(1, GQA):
        psum = psum + p[g * qb:(g + 1) * qb]
    p_hi = psum.astype(BF16)
    rem = psum - p_hi.astype(F32)
    p_mid = rem.astype(BF16)
    p_lo = (rem - p_mid.astype(F32)).astype(BF16)
    ov = ov_ref[...]
    imp = _dot(p_hi, ov) + _dot(p_mid, ov) + _dot(p_lo, ov)

    jf = lax.broadcasted_iota(jnp.int32, (qb, MASK_ROWS), 1).astype(F32)
    rowi = lax.broadcasted_iota(jnp.int32, (qb, MASK_ROWS), 0)
    qblk = ((i * qb + rowi) >> int(math.log2(SLC_BLOCK))).astype(F32)
    val = jnp.where(jf == 0.0, 3e38, jnp.where(jf == qblk, 3e38, jnp.where(jf == qblk - 1.0, 3e38, imp)))
    val = jnp.where(jf > qblk, -1.0, val)
    selneg = jnp.full((qb, MASK_ROWS), NEG, F32)
    for _ in range(SLC_TOPK):
        mx = jnp.max(val, axis=-1, keepdims=True)
        idx = jnp.min(jnp.where(val == mx, jf, float(MASK_ROWS)), axis=-1, keepdims=True)
        hit = jf == idx
        selneg = jnp.where(hit, 0.0, selneg)
        val = jnp.where(hit, -2.0, val)

    selneg_b = selneg.astype(BF16)
    q_sel = jnp.concatenate([qst, jnp.concatenate([selneg_b] * GQA, axis=0)], axis=1)
    step = SEL_TK // qb

    def body(j, carry):
        m_old, l_old, acc = carry
        c0 = pl.multiple_of(j * SEL_TK, SEL_TK)
        dd = jnp.minimum(i - j * step, SEL_ND - 1)
        sc = _dot(q_sel, ks_ref[0, 0, :, pl.ds(c0, SEL_TK)]) + bt_ref[0, dd]
        m_new = jnp.maximum(m_old, jnp.max(sc, axis=-1, keepdims=True))
        alpha = jnp.exp(m_old - m_new)
        pt = jnp.exp(sc - m_new)
        l_new = alpha * l_old + jnp.sum(pt, axis=-1, keepdims=True)
        acc = alpha * acc + _dot(pt.astype(BF16), vs_ref[0, 0, pl.ds(c0, SEL_TK), :])
        return m_new, l_new, acc

    init = (jnp.full((rows, 1), NEG, F32), jnp.zeros((rows, 1), F32), jnp.zeros((rows, DK), F32))
    _, l_s, acc_s = lax.fori_loop(0, (i * qb) // SEL_TK + 1, body, init)
    o_s = acc_s / l_s

    wk = WINDOW + qb
    w0 = pl.multiple_of(i * qb, qb)
    colj = lax.broadcasted_iota(jnp.int32, (rows, MASK_ROWS), 1)
    q_win = jnp.concatenate([qst, jnp.where(colj == 0, NEG, 0.0).astype(BF16)], axis=1)
    sw = _dot(q_win, kw_ref[0, 0, :, pl.ds(w0, wk)]) + bw_ref[0]
    pw, l_w = softmax_rows(sw)
    o_w = _dot(pw.astype(BF16), vw_ref[0, 0, pl.ds(w0, wk), :]) / l_w

    gt = _sigmoid(gl_ref[...] + gb_ref[...])
    outs = []
    for g in range(GQA):
        r = slice(g * qb, (g + 1) * qb)
        outs.append(gt[:, g:g + 1] * o_c[r] + gt[:, GQA + g:GQA + g + 1] * o_s[r]
                    + gt[:, 2 * GQA + g:2 * GQA + g + 1] * o_w[r])
    o = jnp.concatenate(outs, axis=1)
    o_ref[...] = (o * _silu(gc_ref[...])).astype(o_ref.dtype)


def _rel_bucket(dist):
    n = jnp.maximum(dist, 0)
    exact = REL_BUCKETS // 2
    nf = jnp.maximum(n, 1).astype(jnp.float32)
    large = exact + (jnp.log(nf / exact) / math.log(REL_MAX_DIST / exact)
                     * (REL_BUCKETS - exact)).astype(jnp.int32)
    return jnp.where(n < exact, n, jnp.minimum(large, REL_BUCKETS - 1))


def nsa_bias_tables(rel_bias, seq):
    qb = Q_BLOCK
    nmax = max(seq, SEL_ND * qb + qb)
    bvec = rel_bias[_rel_bucket(jnp.arange(nmax))].T.astype(F32)

    def lookup(dist, valid):
        return jnp.where(valid, bvec[:, jnp.clip(dist, 0, nmax - 1)], NEG)

    n_pad = seq // CMP_STRIDE
    t = jnp.arange(seq)[:, None]
    nn = jnp.arange(n_pad)[None, :]
    dc = t - (nn * CMP_STRIDE + CMP_BLOCK - 1)
    bias_c = lookup(dc, (dc >= 0) & (nn < n_pad - 1))

    r = jnp.arange(qb)[:, None]
    c = jnp.arange(SEL_TK)[None, :]
    ds = jnp.arange(SEL_ND)[:, None, None] * qb + (r - c)[None]
    bias_t = lookup(ds, ds >= 0)
    bias_t = bias_t.reshape(HKV, GQA, SEL_ND, qb, SEL_TK).transpose(0, 2, 1, 3, 4).reshape(
        HKV, SEL_ND, GQA * qb, SEL_TK)

    cw = jnp.arange(WINDOW + qb)[None, :]
    dw = WINDOW + r - cw
    bias_w = lookup(dw, (dw >= 0) & (dw < WINDOW)).reshape(HKV, GQA * qb, WINDOW + qb)
    return bias_c, bias_t, bias_w


def _overlap_matrix(n_pad):
    c_start = np.arange(n_pad)[:, None] * CMP_STRIDE
    s_start = np.arange(MASK_ROWS)[None, :] * SLC_BLOCK
    ov = np.clip(np.minimum(c_start + CMP_BLOCK, s_start + SLC_BLOCK) - np.maximum(c_start, s_start), 0, None)
    return (ov.astype(np.float32) / CMP_BLOCK)


def nsa_branch(h, hgl, gate_b4, kct, vc, tables, bsz, seq):
    qb = Q_BLOCK
    nqb = seq // qb
    n_pad = seq // CMP_STRIDE
    n_slc = seq // SLC_BLOCK
    assert n_slc <= MASK_ROWS and seq % SEL_TK == 0
    bias_c, bias_t, bias_w = tables
    m = bsz * seq

    def heads_t(c0):
        return h[:, c0:c0 + KV_W].reshape(bsz, seq, HKV, DK).transpose(0, 2, 3, 1).astype(BF16)

    def heads(c0):
        return h[:, c0:c0 + KV_W].reshape(bsz, seq, HKV, DK).transpose(0, 2, 1, 3).astype(BF16)

    e_rows = (np.arange(seq)[None, :] // SLC_BLOCK == np.arange(MASK_ROWS)[:, None])
    e_rows = jnp.broadcast_to(jnp.asarray(e_rows, BF16), (bsz, HKV, MASK_ROWS, seq))
    ks_ext = jnp.concatenate([heads_t(C_KV + 2 * KV_W), e_rows], axis=2)
    vs = heads(C_KV + 3 * KV_W)
    invalid = np.zeros((MASK_ROWS, seq + WINDOW), np.float32)
    invalid[0, :WINDOW] = 1.0
    kw_pad = jnp.pad(heads_t(C_KV + 4 * KV_W), ((0, 0), (0, 0), (0, 0), (WINDOW, 0)))
    kw_ext = jnp.concatenate(
        [kw_pad, jnp.broadcast_to(jnp.asarray(invalid, BF16), (bsz, HKV, MASK_ROWS, seq + WINDOW))], axis=2)
    vw_pad = jnp.pad(heads(C_KV + 5 * KV_W), ((0, 0), (0, 0), (WINDOW, 0), (0, 0)))
    ov = jnp.asarray(_overlap_matrix(n_pad), BF16)

    gw = GQA * DK
    rowblk = lambda c: pl.BlockSpec((qb, gw), lambda hh, b, i, c=c: (b * nqb + i, c // gw + hh))
    per_bh = lambda a: pl.BlockSpec((1, 1) + a.shape[2:], lambda hh, b, i: (b, hh, 0, 0))
    in_specs = [
        rowblk(C_QC), rowblk(C_GC),
        pl.BlockSpec((qb, 128), lambda hh, b, i: (b * nqb + i, hh)),
        pl.BlockSpec((1, 128), lambda hh, b, i: (0, hh)),
        pl.BlockSpec((1, DK, n_pad), lambda hh, b, i: (b * HKV + hh, 0, 0)),
        pl.BlockSpec((1, n_pad, DK), lambda hh, b, i: (b * HKV + hh, 0, 0)),
        pl.BlockSpec((GQA, qb, n_pad), lambda hh, b, i: (hh, i, 0)),
        pl.BlockSpec(ov.shape, lambda hh, b, i: (0, 0)),
        per_bh(ks_ext), per_bh(vs),
        pl.BlockSpec((1,) + bias_t.shape[1:], lambda hh, b, i: (hh, 0, 0, 0)),
        per_bh(kw_ext), per_bh(vw_pad),
        pl.BlockSpec((1,) + bias_w.shape[1:], lambda hh, b, i: (hh, 0, 0)),
    ]
    return pl.pallas_call(
        functools.partial(_nsa_kernel, n_cmp_pad=n_pad),
        grid=(HKV, bsz, nqb),
        in_specs=in_specs,
        out_specs=pl.BlockSpec((qb, gw), lambda hh, b, i: (b * nqb + i, hh)),
        out_shape=jax.ShapeDtypeStruct((m, D_BR), BF16),
        compiler_params=_params(),
        name="nsa_attn",
    )(h, h, hgl, gate_b4, kct, vc, bias_c, ov, ks_ext, vs, bias_t, kw_ext, vw_pad, bias_w)


def _final_kernel(oa_ref, ob_ref, oc_ref, om_ref, g0_ref, g1_ref, g2_ref, g3_ref, x_ref,
                  wb_ref, wo_ref, lng_ref, lnb_ref, y_ref, yb_ref):
    merged = _sigmoid(g0_ref[...]) * _dot(oa_ref[...], wb_ref[0])
    merged = merged + _sigmoid(g1_ref[...]) * _dot(ob_ref[...], wb_ref[1])
    merged = merged + _sigmoid(g2_ref[...]) * _dot(oc_ref[...], wb_ref[2])
    merged = merged + _sigmoid(g3_ref[...]) * _dot(om_ref[...], wb_ref[3])
    z = DN_ALPHA * x_ref[...] + _dot(merged.astype(BF16), wo_ref[...])
    mu = jnp.mean(z, axis=-1, keepdims=True)
    zc = z - mu
    var = jnp.mean(zc * zc, axis=-1, keepdims=True)
    y = zc * lax.rsqrt(var + LN_EPS) * lng_ref[...] + lnb_ref[...]
    y_ref[...] = y
    yb_ref[...] = y.astype(BF16)


def final_merge(o_a, o_b, o_c, o_m, h, x, w_branch, w_out, ln_g, ln_b, tm=128):
    m = x.shape[0]
    tm = min(tm, m)
    br = pl.BlockSpec((tm, D_BR), lambda i: (i, 0))
    gate = lambda k: pl.BlockSpec((tm, D_MODEL), lambda i, k=k: (i, C_MERGE // D_MODEL + k))
    xs = pl.BlockSpec((tm, D_MODEL), lambda i: (i, 0))
    resident = lambda a: pl.BlockSpec(a.shape, lambda i: (0,) * a.ndim, pipeline_mode=pl.Buffered(1))
    wb = w_branch.astype(BF16)
    wo = w_out.astype(BF16)
    lg, lb = ln_g.reshape(1, D_MODEL), ln_b.reshape(1, D_MODEL)
    return pl.pallas_call(
        _final_kernel,
        grid=(m // tm,),
        in_specs=[br, br, br, br, gate(0), gate(1), gate(2), gate(3), xs,
                  resident(wb), resident(wo), resident(lg), resident(lb)],
        out_specs=[xs, xs],
        out_shape=[jax.ShapeDtypeStruct((m, D_MODEL), F32), jax.ShapeDtypeStruct((m, D_MODEL), BF16)],
        compiler_params=_params(),
        name="merge_out_ln",
    )(o_a, o_b, o_c, o_m, h, h, h, h, x, wb, wo, lg, lb)


def _pack_w_in(w):
    sizes = (D_BR,) * 6 + (KV_W,) * 6 + (3 * NSA_HEADS,) + (D_BR,) * 3 + (N_BRANCHES * D_MODEL,)
    offs = np.concatenate([[0], np.cumsum(sizes)])
    seg = lambda k: w[:, int(offs[k]):int(offs[k + 1])]
    main = jnp.concatenate([seg(16), seg(0), seg(1), seg(2), seg(3), seg(4), seg(5), seg(13), seg(14), seg(15),
                            seg(6), seg(7), seg(8), seg(9), seg(10), seg(11)], axis=1).astype(BF16)
    src = np.full((HKV * 128,), -1, np.int64)
    for hh in range(HKV):
        for brn in range(3):
            for g in range(GQA):
                src[hh * 128 + brn * GQA + g] = brn * NSA_HEADS + hh * GQA + g
    return main, src, seg(12)


def _spread_gate_cols(a, src):
    return jnp.where(jnp.asarray(src >= 0), a[..., np.maximum(src, 0)], 0)


def layer(x, xb, mem_b, tables, bsz, seq, w_in, sgu_ln_g, sgu_ln_b, sgu_w, sgu_b, conv_w, conv_b,
          lru_wa, lru_ba, lru_wx, lru_bx, lru_lambda, cmp_pe_k, cmp_pe_v, cmp_w1_k, cmp_w1_v,
          cmp_w2_k, cmp_w2_v, nsa_gate_b, w_mem_kv, w_branch, w_out, ln_g, ln_b):
    m = bsz * seq
    w_main, src, w_gl = _pack_w_in(w_in)
    tm = min(1024, m)
    h = matmul(xb, w_main, F32, tm, 512)
    hgl = matmul(xb, _spread_gate_cols(w_gl, src).astype(BF16), F32, tm, HKV * 128)
    gate_b4 = _spread_gate_cols(nsa_gate_b, src).reshape(1, HKV * 128)

    o_a = gmlp_branch(h, sgu_ln_g, sgu_ln_b, sgu_w, sgu_b)
    o_b = lru_branch(h, bsz, seq, conv_w, conv_b, lru_wa, lru_ba, lru_wx, lru_bx, lru_lambda)

    def stride_blocks(c0):
        t = h[:, c0:c0 + KV_W].reshape(bsz, seq // CMP_STRIDE, CMP_STRIDE, HKV, DK)
        return t.transpose(0, 3, 1, 2, 4).reshape(bsz * HKV, seq // CMP_STRIDE, CMP_STRIDE * DK)

    kct, vc = nsa_compress(stride_blocks(C_KV), stride_blocks(C_KV + KV_W), cmp_pe_k, cmp_pe_v,
                           cmp_w1_k, cmp_w1_v, cmp_w2_k, cmp_w2_v)
    o_c = nsa_branch(h, hgl, gate_b4, kct, vc, tables, bsz, seq)

    mrows = mem_b.shape[0]
    kv = matmul(mem_b, w_mem_kv.astype(BF16), BF16, min(512, mrows), 512)
    o_m = mem_branch(h, kv, bsz, seq)
    return final_merge(o_a, o_b, o_c, o_m, h, x, w_branch, w_out, ln_g, ln_b)


def kernel(x, mem, rel_bias, w_in, sgu_ln_g, sgu_ln_b, sgu_w, sgu_b, conv_w, conv_b, lru_wa, lru_ba, lru_wx,
           lru_bx, lru_lambda, cmp_pe_k, cmp_pe_v, cmp_w1_k, cmp_w1_v, cmp_w2_k, cmp_w2_v, nsa_gate_b,
           w_mem_kv, w_branch, w_out, ln_g, ln_b):
    bsz, seq, _ = x.shape
    tables = nsa_bias_tables(rel_bias, seq)
    xf = x.reshape(bsz * seq, D_MODEL)
    xb = xf.astype(BF16)
    mem_b = mem.reshape(-1, D_MODEL).astype(BF16)
    per_layer = (w_in, sgu_ln_g, sgu_ln_b, sgu_w, sgu_b, conv_w, conv_b, lru_wa, lru_ba, lru_wx, lru_bx,
                 lru_lambda, cmp_pe_k, cmp_pe_v, cmp_w1_k, cmp_w1_v, cmp_w2_k, cmp_w2_v, nsa_gate_b,
                 w_mem_kv, w_branch, w_out, ln_g, ln_b)
    for l in range(w_in.shape[0]):
        xf, xb = layer(xf, xb, mem_b, tables, bsz, seq, *(p[l] for p in per_layer))
    return xf.reshape(bsz, seq, D_MODEL)
```

```python
import functools
import math

import numpy as np
import jax
import jax.numpy as jnp
from jax import lax
from jax.experimental import pallas as pl
from jax.experimental.pallas import tpu as pltpu

F32 = jnp.float32
BF16 = jnp.bfloat16

D_MODEL = 2048
DEPTH = 2
D_BR = D_MODEL // 2
N_BRANCHES = 4
GMLP_CHUNK = 128
GMLP_GROUPS = 8
LRU_BLOCKS = 8
LRU_BLOCK_DIM = D_BR // LRU_BLOCKS
CONV_WIDTH = 4
LRU_C = 8.0
DK = 64
NSA_HEADS = D_BR // DK
HKV = NSA_HEADS // 4
GQA = NSA_HEADS // HKV
KV_W = HKV * DK
CMP_BLOCK = 32
CMP_STRIDE = 16
CMP_HIDDEN = 256
SLC_BLOCK = 64
SLC_TOPK = 8
WINDOW = 256
Q_BLOCK = 128
MEM_HEADS = 4
MEM_HEAD_DIM = D_BR // MEM_HEADS
REL_BUCKETS = 32
REL_MAX_DIST = 1024
DN_ALPHA = (2 * DEPTH) ** 0.25
LN_EPS = 1e-5

LANES = 128
SUBLANES = 8
NEG = -1e30
SEL_TK = 256
SEL_FAR = 9
MASK_ROWS = LANES - DK
NA = Q_BLOCK // CMP_STRIDE
VMEM_LIMIT = 56 * 1024 * 1024

C1_U, C1_V, C1_GA, C1_XB, C1_GB, C1_QC = (D_BR * k for k in range(6))
C1_KV = 6 * D_BR
N1 = C1_KV + 6 * KV_W
C2_GC, C2_QM, C2_GM, C2_MERGE = (D_BR * k for k in range(4))
N2 = C2_MERGE + N_BRANCHES * D_MODEL
GL_OFF = N1


def _sigmoid(x):
    return 1.0 / (1.0 + jnp.exp(-x))


def _silu(x):
    return x * _sigmoid(x)


def _gelu_tanh(x):
    return 0.5 * x * (1.0 + jnp.tanh(math.sqrt(2.0 / math.pi) * (x + 0.044715 * (x * x * x))))


def _dot(a, b):
    return jnp.dot(a, b, preferred_element_type=F32)


def _dot_nt(a, b):
    return lax.dot_general(a, b, (((1,), (1,)), ((), ())), preferred_element_type=F32)


def _params(**kw):
    return pltpu.CompilerParams(vmem_limit_bytes=VMEM_LIMIT, **kw)


def _mm_kernel(x_ref, w_ref, o_ref):
    o_ref[...] = _dot(x_ref[...], w_ref[...]).astype(o_ref.dtype)


def matmul(x, w, out_dtype, tm, tn):
    m, k = x.shape
    n = w.shape[1]
    assert m % tm == 0 and n % tn == 0, (m, n, tm, tn)
    return pl.pallas_call(
        _mm_kernel,
        grid=(n // tn, m // tm),
        in_specs=[pl.BlockSpec((tm, k), lambda j, i: (i, 0)),
                  pl.BlockSpec((k, tn), lambda j, i: (0, j))],
        out_specs=pl.BlockSpec((tm, tn), lambda j, i: (i, j)),
        out_shape=jax.ShapeDtypeStruct((m, n), out_dtype),
        compiler_params=_params(),
        name="matmul",
    )(x, w)


def _mm_castw_kernel(x_ref, w_ref, o_ref, wb_ref):
    @pl.when(pl.program_id(1) == 0)
    def _():
        wb_ref[...] = w_ref[0].astype(BF16)

    o_ref[...] = _dot(x_ref[...], wb_ref[...]).astype(o_ref.dtype)


def matmul_f32w(x, w3, layer_idx, n, out_dtype, tm, tn):
    m, k = x.shape
    assert m % tm == 0 and n % tn == 0 and n <= w3.shape[2], (m, n, tm, tn)
    return pl.pallas_call(
        _mm_castw_kernel,
        grid=(n // tn, m // tm),
        in_specs=[pl.BlockSpec((tm, k), lambda j, i: (i, 0)),
                  pl.BlockSpec((1, k, tn), lambda j, i: (layer_idx, 0, j))],
        out_specs=pl.BlockSpec((tm, tn), lambda j, i: (i, j)),
        out_shape=jax.ShapeDtypeStruct((m, n), out_dtype),
        scratch_shapes=[pltpu.VMEM((k, tn), BF16)],
        compiler_params=_params(dimension_semantics=("arbitrary", "arbitrary")),
        name="matmul_f32w",
    )(x, w3)


def _gmlp_kernel(u_ref, v_ref, ga_ref, lng_ref, lnb_ref, w_ref, bs_ref, o_ref, *, rows):
    gd = D_BR // GMLP_GROUPS
    for c in range(rows // GMLP_CHUNK):
        r = slice(c * GMLP_CHUNK, (c + 1) * GMLP_CHUNK)
        v = _gelu_tanh(v_ref[r, :])
        mu = jnp.mean(v, axis=-1, keepdims=True)
        vc = v - mu
        var = jnp.mean(vc * vc, axis=-1, keepdims=True)
        vb = (vc * lax.rsqrt(var + LN_EPS) * lng_ref[...] + lnb_ref[...]).astype(BF16)
        u = _gelu_tanh(u_ref[r, :]) * _silu(ga_ref[r, :])
        for g in range(GMLP_GROUPS):
            cs = slice(g * gd, (g + 1) * gd)
            mixed = _dot(w_ref[g], vb[:, cs]) + bs_ref[:, g:g + 1]
            o_ref[r, cs] = (u[:, cs] * mixed).astype(o_ref.dtype)


def gmlp_branch(h1, ln_g, ln_b, w_s, b_s, rows=512):
    m = h1.shape[0]
    rows = min(rows, m)
    causal = jnp.tril(jnp.ones((GMLP_CHUNK, GMLP_CHUNK), dtype=bool))
    w = jnp.where(causal, w_s, 0).astype(BF16)
    col = lambda c: pl.BlockSpec((rows, D_BR), lambda i, c=c: (i, c // D_BR))
    full = lambda a: pl.BlockSpec(a.shape, lambda i: (0,) * a.ndim)
    args = (ln_g.reshape(1, D_BR), ln_b.reshape(1, D_BR), w, b_s.T)
    return pl.pallas_call(
        functools.partial(_gmlp_kernel, rows=rows),
        grid=(m // rows,),
        in_specs=[col(C1_U), col(C1_V), col(C1_GA)] + [full(a) for a in args],
        out_specs=pl.BlockSpec((rows, D_BR), lambda i: (i, 0)),
        out_shape=jax.ShapeDtypeStruct((m, D_BR), BF16),
        compiler_params=_params(),
        name="gmlp",
    )(h1, h1, h1, *args)


def _lru_kernel(xb_ref, gb_ref, cw_ref, cb_ref, wa_ref, ba_ref, wx_ref, bx_ref, lam_ref, o_ref,
                xbuf, hcarry, a_s, g_s, *, ts):
    @pl.when(pl.program_id(1) == 0)
    def _():
        xbuf[0:8, :] = jnp.zeros((8, D_BR), F32)
        hcarry[...] = jnp.zeros((8, D_BR), F32)

    xbuf[8:8 + ts, :] = xb_ref[...]
    xc = cb_ref[...] + cw_ref[0:1, :] * xbuf[pl.ds(8 - (CONV_WIDTH - 1), ts), :]
    for k in range(1, CONV_WIDTH):
        xc = xc + cw_ref[k:k + 1, :] * xbuf[pl.ds(8 - (CONV_WIDTH - 1) + k, ts), :]
    xbuf[0:8, :] = xbuf[ts:ts + 8, :]

    xcb = xc.astype(BF16)
    bd = LRU_BLOCK_DIM
    r = jnp.concatenate([_dot(xcb[:, n * bd:(n + 1) * bd], wa_ref[n]) for n in range(LRU_BLOCKS)], axis=1)
    i = jnp.concatenate([_dot(xcb[:, n * bd:(n + 1) * bd], wx_ref[n]) for n in range(LRU_BLOCKS)], axis=1)
    r = _sigmoid(r + ba_ref[...])
    i = _sigmoid(i + bx_ref[...])
    nl = -lam_ref[...]
    softplus = jnp.maximum(nl, 0.0) + jnp.log1p(jnp.exp(-jnp.abs(nl)))
    log_a = (-LRU_C * softplus) * r
    a = jnp.exp(log_a)
    a_s[...] = a
    g_s[...] = jnp.sqrt(1.0 - a * a) * i * xc

    row = lax.broadcasted_iota(jnp.int32, (8, D_BR), 0)

    def body(j, carry):
        r0 = pl.multiple_of(j * 8, 8)
        av = a_s[pl.ds(r0, 8), :]
        bv = g_s[pl.ds(r0, 8), :]
        for d in (1, 2, 4):
            keep = row >= d
            a_sh = pltpu.roll(av, d, axis=0)
            b_sh = pltpu.roll(bv, d, axis=0)
            bv = jnp.where(keep, av * b_sh + bv, bv)
            av = jnp.where(keep, av * a_sh, av)
        hv = av * carry + bv
        g_s[pl.ds(r0, 8), :] = hv
        return jnp.broadcast_to(hv[7:8, :], (8, D_BR))

    hcarry[...] = lax.fori_loop(0, ts // 8, body, hcarry[...])
    o_ref[...] = (g_s[...] * _silu(gb_ref[...])).astype(o_ref.dtype)


def lru_branch(h1, bsz, seq, conv_w, conv_b, wa, ba, wx, bx, lam, ts=512):
    ts = min(ts, seq)
    ns = seq // ts
    col = lambda c: pl.BlockSpec((ts, D_BR), lambda b, s, c=c: (b * ns + s, c // D_BR))
    full = lambda a: pl.BlockSpec(a.shape, lambda b, s: (0,) * a.ndim)
    row = lambda a: a.reshape(1, D_BR)
    args = (conv_w, row(conv_b), wa.astype(BF16), row(ba), wx.astype(BF16), row(bx), row(lam))
    return pl.pallas_call(
        functools.partial(_lru_kernel, ts=ts),
        grid=(bsz, ns),
        in_specs=[col(C1_XB), col(C1_GB)] + [full(a) for a in args],
        out_specs=pl.BlockSpec((ts, D_BR), lambda b, s: (b * ns + s, 0)),
        out_shape=jax.ShapeDtypeStruct((bsz * seq, D_BR), BF16),
        scratch_shapes=[pltpu.VMEM((ts + 8, D_BR), F32), pltpu.VMEM((8, D_BR), F32),
                        pltpu.VMEM((ts, D_BR), F32), pltpu.VMEM((ts, D_BR), F32)],
        compiler_params=_params(dimension_semantics=("arbitrary", "arbitrary")),
        name="lru",
    )(h1, h1, *args)


def _mem_kernel(q_ref, g_ref, kv_ref, o_ref):
    hd = MEM_HEAD_DIM
    for hh in range(MEM_HEADS):
        cs = slice(hh * hd, (hh + 1) * hd)
        q = (q_ref[:, cs] * (hd ** -0.5)).astype(BF16)
        s = _dot_nt(q, kv_ref[:, cs])
        p = jnp.exp(s - jnp.max(s, axis=-1, keepdims=True))
        l = jnp.sum(p, axis=-1, keepdims=True)
        o = _dot(p.astype(BF16), kv_ref[:, D_BR + hh * hd:D_BR + (hh + 1) * hd]) / l
        o_ref[:, cs] = (o * _silu(g_ref[:, cs])).astype(o_ref.dtype)


def mem_branch(h2, kv, bsz, seq, tq=512):
    tq = min(tq, seq)
    nq = seq // tq
    mlen = kv.shape[0] // bsz
    col = lambda c: pl.BlockSpec((tq, D_BR), lambda b, i, c=c: (b * nq + i, c // D_BR))
    return pl.pallas_call(
        _mem_kernel,
        grid=(bsz, nq),
        in_specs=[col(C2_QM), col(C2_GM), pl.BlockSpec((mlen, 2 * D_BR), lambda b, i: (b, 0))],
        out_specs=pl.BlockSpec((tq, D_BR), lambda b, i: (b * nq + i, 0)),
        out_shape=jax.ShapeDtypeStruct((bsz * seq, D_BR), BF16),
        compiler_params=_params(),
        name="mem_attn",
    )(h2, h2, kv)


def _cmp_kernel(tk_ref, tv_ref, pek_ref, pev_ref, w1k_ref, w1v_ref, w2kt_ref, w2v_ref, kct_ref, vc_ref):
    n = tk_ref.shape[1]

    def hidden(t_ref, pe_ref, w1_ref):
        t = t_ref[0]
        lo = _dot((t + pe_ref[0:1, :]).astype(BF16), w1_ref[0])
        hi = _dot((t + pe_ref[1:2, :]).astype(BF16), w1_ref[1])
        pre = lo + pltpu.roll(hi, n - 1, axis=0)
        return _silu(pre).astype(BF16)

    kct_ref[0] = _dot_nt(w2kt_ref[...], hidden(tk_ref, pek_ref, w1k_ref)).astype(kct_ref.dtype)
    vc_ref[0] = _dot(hidden(tv_ref, pev_ref, w1v_ref), w2v_ref[...]).astype(vc_ref.dtype)


def nsa_compress(tk, tv, pe_k, pe_v, w1_k, w1_v, w2_k, w2_v):
    nb, n, width = tk.shape
    half = CMP_STRIDE * DK
    args = (pe_k.reshape(2, half), pe_v.reshape(2, half),
            w1_k.reshape(2, half, CMP_HIDDEN).astype(BF16), w1_v.reshape(2, half, CMP_HIDDEN).astype(BF16),
            w2_k.T.astype(BF16), w2_v.astype(BF16))
    full = lambda a: pl.BlockSpec(a.shape, lambda b: (0,) * a.ndim)
    blk = pl.BlockSpec((1, n, width), lambda b: (b, 0, 0))
    return pl.pallas_call(
        _cmp_kernel,
        grid=(nb,),
        in_specs=[blk, blk] + [full(a) for a in args],
        out_specs=[pl.BlockSpec((1, DK, n), lambda b: (b, 0, 0)), pl.BlockSpec((1, n, DK), lambda b: (b, 0, 0))],
        out_shape=[jax.ShapeDtypeStruct((nb, DK, n), BF16), jax.ShapeDtypeStruct((nb, n, DK), BF16)],
        compiler_params=_params(),
        name="nsa_compress",
    )(tk, tv, *args)


def _nsa_kernel(q_ref, gc_ref, gl_ref, gb_ref, kct_ref, vc_ref, gcb_ref, ovt_ref,
                ks_ref, vs_ref, gs_ref, kw_ref, vw_ref, bw_ref, o_ref, vt_s, *, n_pad, c0):
    i = pl.program_id(2)
    qb = Q_BLOCK
    wc = 2 * n_pad
    lane = lax.broadcasted_iota(jnp.int32, (qb, LANES), 1)
    q = q_ref[...] * (DK ** -0.5)
    qpad = []
    for g in range(GQA):
        t = q[:, (g // 2) * LANES:(g // 2 + 1) * LANES]
        if g % 2:
            t = pltpu.roll(t, DK, axis=1)
        qpad.append(jnp.where(lane < DK, t, 0.0))

    shift = lax.rem(NA * i + wc - c0, wc)
    o_c = []
    psum = None
    for g in range(GQA):
        bias = pltpu.roll(gcb_ref[0, g * qb:(g + 1) * qb, :], shift, axis=1)[:, :n_pad]
        s = _dot(qpad[g][:, :DK].astype(BF16), kct_ref[0]) + bias
        m = jnp.max(s, axis=-1, keepdims=True)
        p = jnp.where(s > 0.1 * NEG, jnp.exp(s - m), 0.0)
        p = p / jnp.maximum(jnp.sum(p, axis=-1, keepdims=True), 1e-30)
        o_c.append(_dot(p.astype(BF16), vc_ref[0]))
        psum = p if psum is None else psum + p

    p_hi = psum.astype(BF16)
    rem = psum - p_hi.astype(F32)
    p_mid = rem.astype(BF16)
    p_lo = (rem - p_mid.astype(F32)).astype(BF16)
    ovt = ovt_ref[...]
    imp_t = _dot_nt(ovt, p_hi) + _dot_nt(ovt, p_mid) + _dot_nt(ovt, p_lo)

    jrow = lax.broadcasted_iota(jnp.int32, (MASK_ROWS, qb), 0).astype(F32)
    qlane = lax.broadcasted_iota(jnp.int32, (MASK_ROWS, qb), 1)
    qblk = ((i * qb + qlane) >> int(math.log2(SLC_BLOCK))).astype(F32)
    val = jnp.where(jrow == 0.0, 3e38, jnp.where(jrow == qblk, 3e38, jnp.where(jrow == qblk - 1.0, 3e38, imp_t)))
    val = jnp.where(jrow > qblk, -1.0, val)
    vt_s[...] = val
    nv = MASK_ROWS // SUBLANES
    sub = lax.broadcasted_iota(jnp.int32, (SUBLANES, qb), 0)
    vals = [val[k * SUBLANES:(k + 1) * SUBLANES] for k in range(nv)]
    cnt = [jnp.zeros((SUBLANES, qb), F32) for _ in range(nv)]
    for jp in range(MASK_ROWS):
        rowv = jnp.broadcast_to(vt_s[jp:jp + 1, :], (SUBLANES, qb))
        for k in range(nv):
            if k * SUBLANES > jp:
                beat = jnp.where(rowv >= vals[k], 1.0, 0.0)
            elif (k + 1) * SUBLANES - 1 < jp:
                beat = jnp.where(rowv > vals[k], 1.0, 0.0)
            else:
                beat = jnp.where(sub > jp - k * SUBLANES, jnp.where(rowv >= vals[k], 1.0, 0.0),
                                 jnp.where(rowv > vals[k], 1.0, 0.0))
            cnt[k] = cnt[k] + beat
    selneg_t = jnp.concatenate([jnp.where(c < float(SLC_TOPK), 0.0, NEG) for c in cnt], axis=0)
    selpad = jnp.concatenate([jnp.zeros((DK, qb), F32), selneg_t], axis=0).T

    q_sel = [(qpad[g] + selpad).astype(BF16) for g in range(GQA)]
    step = SEL_TK // qb

    def body(j, carry):
        ms, accs = carry
        k0 = pl.multiple_of(j * SEL_TK, SEL_TK)
        kt = ks_ref[0, 0, :, pl.ds(k0, SEL_TK)]
        vt = vs_ref[0, 0, pl.ds(k0, SEL_TK), :]
        dd = jnp.minimum(i - j * step, SEL_FAR)
        boff = pl.multiple_of((SEL_FAR - dd) * qb, qb)
        new_m, new_acc = [], []
        for g in range(GQA):
            sc = _dot(q_sel[g], kt) + gs_ref[0, g * qb:(g + 1) * qb, pl.ds(boff, SEL_TK)]
            m_new = jnp.maximum(ms[g], jnp.max(sc, axis=-1, keepdims=True))
            alpha = jnp.exp(ms[g] - m_new)
            pt = jnp.exp(sc - m_new).astype(BF16)
            new_m.append(m_new)
            new_acc.append(alpha * accs[g] + _dot(pt, vt))
        return tuple(new_m), tuple(new_acc)

    init = (tuple(jnp.full((qb, 1), NEG, F32) for _ in range(GQA)),
            tuple(jnp.zeros((qb, LANES), F32) for _ in range(GQA)))
    _, acc_s = lax.fori_loop(0, (i * qb) // SEL_TK + 1, body, init)

    wk = WINDOW + qb
    w0 = pl.multiple_of(i * qb, qb)
    kwt = kw_ref[0, 0, :, pl.ds(w0, wk)]
    vwt = vw_ref[0, 0, pl.ds(w0, wk), :]

    gt = _sigmoid(gl_ref[...] + gb_ref[...])
    outs = []
    for g in range(GQA):
        sw = _dot(jnp.where(lane == DK, NEG, qpad[g]).astype(BF16), kwt) + bw_ref[0, g * qb:(g + 1) * qb, :]
        pw = jnp.exp(sw - jnp.max(sw, axis=-1, keepdims=True)).astype(BF16)
        acc_w = _dot(pw, vwt)
        o_w = acc_w[:, :DK] / acc_w[:, DK:DK + 1]
        o_s = acc_s[g][:, :DK] / acc_s[g][:, DK:DK + 1]
        outs.append(gt[:, g:g + 1] * o_c[g] + gt[:, GQA + g:GQA + g + 1] * o_s
                    + gt[:, 2 * GQA + g:2 * GQA + g + 1] * o_w)
    o = jnp.concatenate(outs, axis=1)
    o_ref[...] = (o * _silu(gc_ref[...])).astype(o_ref.dtype)


def _rel_bucket(dist):
    n = jnp.maximum(dist, 0)
    exact = REL_BUCKETS // 2
    nf = jnp.maximum(n, 1).astype(jnp.float32)
    large = exact + (jnp.log(nf / exact) / math.log(REL_MAX_DIST / exact)
                     * (REL_BUCKETS - exact)).astype(jnp.int32)
    return jnp.where(n < exact, n, jnp.minimum(large, REL_BUCKETS - 1))


def nsa_bias_tables(rel_bias, seq):
    qb = Q_BLOCK
    n_pad = seq // CMP_STRIDE
    c0 = n_pad - NA
    wc = 2 * n_pad
    ws = SEL_FAR * qb + SEL_TK
    nmax = max(seq, SEL_FAR * qb + qb)
    padl = CMP_STRIDE * (wc + 2)
    bvec = rel_bias[_rel_bucket(jnp.arange(nmax))].T.astype(F32)
    vext = jnp.concatenate([jnp.full((NSA_HEADS, padl), NEG, F32), bvec], axis=1)
    pos = jnp.arange(padl + nmax) - padl
    vwin = jnp.where(pos < WINDOW, vext, NEG)

    mlo, mhi = c0 - wc + 1, NA - 1 + c0
    start = padl + CMP_STRIDE * mlo - (CMP_BLOCK - 1)
    assert start >= 0
    u = vext[:, start:start + CMP_STRIDE * (mhi - mlo + 1)].reshape(NSA_HEADS, mhi - mlo + 1, CMP_STRIDE)
    urev = u[:, ::-1, :]
    gcb = jnp.stack([urev[:, NA - 1 - a:NA - 1 - a + wc, :] for a in range(NA)], axis=1)
    gcb = gcb.transpose(0, 1, 3, 2).reshape(HKV, GQA * qb, wc)

    def toeplitz(vec, d0, width):
        vrev = vec[:, ::-1]
        starts = [nmax - 1 - d0 - r for r in range(qb)]
        assert min(starts) >= 0 and max(starts) + width <= padl + nmax
        return jnp.stack([vrev[:, s:s + width] for s in starts], axis=1).reshape(HKV, GQA * qb, width)

    gs = toeplitz(vext, SEL_FAR * qb, ws)
    bw = toeplitz(vwin, WINDOW, WINDOW + qb)
    return gcb, gs, bw


def _overlap_matrix_t(n_pad):
    c_start = np.arange(n_pad)[None, :] * CMP_STRIDE
    s_start = np.arange(MASK_ROWS)[:, None] * SLC_BLOCK
    ov = np.clip(np.minimum(c_start + CMP_BLOCK, s_start + SLC_BLOCK) - np.maximum(c_start, s_start), 0, None)
    return ov.astype(np.float32) / CMP_BLOCK


def nsa_branch(h1, h2, hgl, gate_b4, kct, vc, tables, bsz, seq):
    qb = Q_BLOCK
    nqb = seq // qb
    n_pad = seq // CMP_STRIDE
    n_slc = seq // SLC_BLOCK
    assert n_slc <= MASK_ROWS and seq % SEL_TK == 0
    gcb, gs, bw = tables
    m = bsz * seq

    def heads_t(c):
        return h1[:, c:c + KV_W].reshape(bsz, seq, HKV, DK).transpose(0, 2, 3, 1).astype(BF16)

    def heads_ones(c):
        v = h1[:, c:c + KV_W].reshape(bsz, seq, HKV, DK).transpose(0, 2, 1, 3).astype(BF16)
        tail = np.zeros((LANES - DK,), np.float32)
        tail[0] = 1.0
        return jnp.concatenate([v, jnp.broadcast_to(jnp.asarray(tail, BF16), v.shape[:3] + (LANES - DK,))], axis=3)

    e_rows = (np.arange(seq)[None, :] // SLC_BLOCK == np.arange(MASK_ROWS)[:, None])
    e_rows = jnp.broadcast_to(jnp.asarray(e_rows, BF16), (bsz, HKV, MASK_ROWS, seq))
    ks_ext = jnp.concatenate([heads_t(C1_KV + 2 * KV_W), e_rows], axis=2)
    vs_ext = heads_ones(C1_KV + 3 * KV_W)
    invalid = np.zeros((MASK_ROWS, seq + WINDOW), np.float32)
    invalid[0, :WINDOW] = 1.0
    kw_pad = jnp.pad(heads_t(C1_KV + 4 * KV_W), ((0, 0), (0, 0), (0, 0), (WINDOW, 0)))
    kw_ext = jnp.concatenate(
        [kw_pad, jnp.broadcast_to(jnp.asarray(invalid, BF16), (bsz, HKV, MASK_ROWS, seq + WINDOW))], axis=2)
    vw_ext = jnp.pad(heads_ones(C1_KV + 5 * KV_W), ((0, 0), (0, 0), (WINDOW, 0), (0, 0)))
    ovt = jnp.asarray(_overlap_matrix_t(n_pad), BF16)

    gw = GQA * DK
    rowblk = lambda c: pl.BlockSpec((qb, gw), lambda hh, b, i, c=c: (b * nqb + i, c // gw + hh))
    per_bh = lambda a: pl.BlockSpec((1, 1) + a.shape[2:], lambda hh, b, i: (b, hh, 0, 0))
    per_h = lambda a: pl.BlockSpec((1,) + a.shape[1:], lambda hh, b, i: (hh, 0, 0))
    in_specs = [
        rowblk(C1_QC), rowblk(C2_GC),
        pl.BlockSpec((qb, LANES), lambda hh, b, i: (b * nqb + i, hh)),
        pl.BlockSpec((1, LANES), lambda hh, b, i: (0, hh)),
        pl.BlockSpec((1, DK, n_pad), lambda hh, b, i: (b * HKV + hh, 0, 0)),
        pl.BlockSpec((1, n_pad, DK), lambda hh, b, i: (b * HKV + hh, 0, 0)),
        per_h(gcb),
        pl.BlockSpec(ovt.shape, lambda hh, b, i: (0, 0)),
        per_bh(ks_ext), per_bh(vs_ext), per_h(gs),
        per_bh(kw_ext), per_bh(vw_ext), per_h(bw),
    ]
    return pl.pallas_call(
        functools.partial(_nsa_kernel, n_pad=n_pad, c0=n_pad - NA),
        grid=(HKV, bsz, nqb),
        in_specs=in_specs,
        out_specs=pl.BlockSpec((qb, gw), lambda hh, b, i: (b * nqb + i, hh)),
        out_shape=jax.ShapeDtypeStruct((m, D_BR), BF16),
        scratch_shapes=[pltpu.VMEM((MASK_ROWS, qb), F32)],
        compiler_params=_params(),
        name="nsa_attn",
    )(h1, h2, hgl, gate_b4, kct, vc, gcb, ovt, ks_ext, vs_ext, gs, kw_ext, vw_ext, bw)


def _final_kernel(oa_ref, ob_ref, oc_ref, om_ref, *rest):
    gate_refs = rest[:2 * N_BRANCHES]
    x_ref, wb_ref, wo_ref, lng_ref, lnb_ref, y_ref, yb_ref = rest[2 * N_BRANCHES:]
    o_refs = (oa_ref, ob_ref, oc_ref, om_ref)
    halves = []
    for c in range(2):
        cs = slice(c * D_BR, (c + 1) * D_BR)
        acc = None
        for k in range(N_BRANCHES):
            term = _sigmoid(gate_refs[2 * k + c][...]) * _dot(o_refs[k][...], wb_ref[k, :, cs])
            acc = term if acc is None else acc + term
        halves.append(acc.astype(BF16))
    merged = jnp.concatenate(halves, axis=1)
    z = DN_ALPHA * x_ref[...] + _dot(merged, wo_ref[...])
    mu = jnp.mean(z, axis=-1, keepdims=True)
    zc = z - mu
    var = jnp.mean(zc * zc, axis=-1, keepdims=True)
    y = zc * lax.rsqrt(var + LN_EPS) * lng_ref[...] + lnb_ref[...]
    y_ref[...] = y
    yb_ref[...] = y.astype(BF16)


def final_merge(o_a, o_b, o_c, o_m, h2, x, w_branch, w_out, ln_g, ln_b, tm=128):
    m = x.shape[0]
    tm = min(tm, m)
    br = pl.BlockSpec((tm, D_BR), lambda i: (i, 0))
    gate = lambda k: pl.BlockSpec((tm, D_BR), lambda i, k=k: (i, C2_MERGE // D_BR + k))
    xs = pl.BlockSpec((tm, D_MODEL), lambda i: (i, 0))
    resident = lambda a: pl.BlockSpec(a.shape, lambda i: (0,) * a.ndim, pipeline_mode=pl.Buffered(1))
    wb = w_branch.astype(BF16)
    wo = w_out.astype(BF16)
    lg, lb = ln_g.reshape(1, D_MODEL), ln_b.reshape(1, D_MODEL)
    ngate = 2 * N_BRANCHES
    return pl.pallas_call(
        _final_kernel,
        grid=(m // tm,),
        in_specs=[br, br, br, br] + [gate(k) for k in range(ngate)] + [xs]
                 + [resident(wb), resident(wo), resident(lg), resident(lb)],
        out_specs=[xs, xs],
        out_shape=[jax.ShapeDtypeStruct((m, D_MODEL), F32), jax.ShapeDtypeStruct((m, D_MODEL), BF16)],
        compiler_params=_params(),
        name="merge_out_ln",
    )(o_a, o_b, o_c, o_m, *([h2] * ngate), x, wb, wo, lg, lb)


def _gate_col_map():
    src = np.full((HKV * LANES,), -1, np.int64)
    for hh in range(HKV):
        for brn in range(3):
            for g in range(GQA):
                src[hh * LANES + brn * GQA + g] = brn * NSA_HEADS + hh * GQA + g
    return src


def _spread_gate_cols(a):
    src = _gate_col_map()
    return jnp.where(jnp.asarray(src >= 0), a[..., np.maximum(src, 0)], 0)


def layer(l, x, xb, mem_b, tables, bsz, seq, w_in, w_tail, sgu_ln_g, sgu_ln_b, sgu_w, sgu_b, conv_w, conv_b,
          lru_wa, lru_ba, lru_wx, lru_bx, lru_lambda, cmp_pe_k, cmp_pe_v, cmp_w1_k, cmp_w1_v,
          cmp_w2_k, cmp_w2_v, nsa_gate_b, w_mem_kv, w_branch, w_out, ln_g, ln_b):
    m = bsz * seq
    tm = min(1024, m)
    h1 = matmul_f32w(xb, w_in, l, N1, F32, tm, 512)
    h2 = matmul_f32w(xb, w_tail, l, N2, F32, tm, 512)
    w_gl = _spread_gate_cols(w_in[l, :, GL_OFF:GL_OFF + 3 * NSA_HEADS]).astype(BF16)
    hgl = matmul(xb, w_gl, F32, tm, HKV * LANES)
    gate_b4 = _spread_gate_cols(nsa_gate_b[l]).reshape(1, HKV * LANES)

    o_a = gmlp_branch(h1, sgu_ln_g[l], sgu_ln_b[l], sgu_w[l], sgu_b[l])
    o_b = lru_branch(h1, bsz, seq, conv_w[l], conv_b[l], lru_wa[l], lru_ba[l], lru_wx[l], lru_bx[l],
                     lru_lambda[l])

    def stride_blocks(c):
        t = h1[:, c:c + KV_W].reshape(bsz, seq // CMP_STRIDE, CMP_STRIDE, HKV, DK)
        return t.transpose(0, 3, 1, 2, 4).reshape(bsz * HKV, seq // CMP_STRIDE, CMP_STRIDE * DK)

    kct, vc = nsa_compress(stride_blocks(C1_KV), stride_blocks(C1_KV + KV_W), cmp_pe_k[l], cmp_pe_v[l],
                           cmp_w1_k[l], cmp_w1_v[l], cmp_w2_k[l], cmp_w2_v[l])
    o_c = nsa_branch(h1, h2, hgl, gate_b4, kct, vc, tables, bsz, seq)

    mrows = mem_b.shape[0]
    kv = matmul_f32w(mem_b, w_mem_kv, l, 2 * D_BR, BF16, min(512, mrows), 512)
    o_m = mem_branch(h2, kv, bsz, seq)
    return final_merge(o_a, o_b, o_c, o_m, h2, x, w_branch[l], w_out[l], ln_g[l], ln_b[l])


def kernel(x, mem, rel_bias, w_in, sgu_ln_g, sgu_ln_b, sgu_w, sgu_b, conv_w, conv_b, lru_wa, lru_ba, lru_wx,
           lru_bx, lru_lambda, cmp_pe_k, cmp_pe_v, cmp_w1_k, cmp_w1_v, cmp_w2_k, cmp_w2_v, nsa_gate_b,
           w_mem_kv, w_branch, w_out, ln_g, ln_b):
    bsz, seq, _ = x.shape
    tables = nsa_bias_tables(rel_bias, seq)
    xf = x.reshape(bsz * seq, D_MODEL)
    xb = xf.astype(BF16)
    mem_b = mem.reshape(-1, D_MODEL).astype(BF16)
    w_tail = w_in[:, :, GL_OFF + 3 * NSA_HEADS:]
    params = (w_in, w_tail, sgu_ln_g, sgu_ln_b, sgu_w, sgu_b, conv_w, conv_b, lru_wa, lru_ba, lru_wx, lru_bx,
              lru_lambda, cmp_pe_k, cmp_pe_v, cmp_w1_k, cmp_w1_v, cmp_w2_k, cmp_w2_v, nsa_gate_b,
              w_mem_kv, w_branch, w_out, ln_g, ln_b)
    for l in range(w_in.shape[0]):
        xf, xb = layer(l, xf, xb, mem_b, tables, bsz, seq, *params)
    return xf.reshape(bsz, seq, D_MODEL)
```

```python
import functools
import math

import numpy as np
import jax
import jax.numpy as jnp
from jax import lax
from jax.experimental import pallas as pl
from jax.experimental.pallas import tpu as pltpu

F32 = jnp.float32
BF16 = jnp.bfloat16

D_MODEL = 2048
DEPTH = 2
D_BR = D_MODEL // 2
N_BRANCHES = 4
GMLP_CHUNK = 128
GMLP_GROUPS = 8
LRU_BLOCKS = 8
LRU_BLOCK_DIM = D_BR // LRU_BLOCKS
CONV_WIDTH = 4
LRU_C = 8.0
DK = 64
NSA_HEADS = D_BR // DK
HKV = NSA_HEADS // 4
GQA = NSA_HEADS // HKV
KV_W = HKV * DK
CMP_BLOCK = 32
CMP_STRIDE = 16
CMP_HIDDEN = 256
SLC_BLOCK = 64
SLC_TOPK = 8
WINDOW = 256
Q_BLOCK = 128
MEM_HEADS = 4
MEM_HEAD_DIM = D_BR // MEM_HEADS
REL_BUCKETS = 32
REL_MAX_DIST = 1024
DN_ALPHA = (2 * DEPTH) ** 0.25
LN_EPS = 1e-5

LANES = 128
SUBLANES = 8
NEG = -1e30
SEL_TK = 512
SEL_FAR = 11
MASK_ROWS = LANES - DK
NA = Q_BLOCK // CMP_STRIDE
SEL_SUB = 4
VMEM_LIMIT = 56 * 1024 * 1024

C1_U, C1_V, C1_GA, C1_XB, C1_GB, C1_QC = (D_BR * k for k in range(6))
C1_KV = 6 * D_BR
N1 = C1_KV + 2 * KV_W
NKV = 4 * KV_W
C2_GC, C2_QM, C2_GM, C2_MERGE = (D_BR * k for k in range(4))
N2 = C2_MERGE + N_BRANCHES * D_MODEL
GL_OFF = N1 + NKV


def _sigmoid(x):
    return 1.0 / (1.0 + jnp.exp(-x))


def _silu(x):
    return x * _sigmoid(x)


def _gelu_tanh(x):
    return 0.5 * x * (1.0 + jnp.tanh(math.sqrt(2.0 / math.pi) * (x + 0.044715 * (x * x * x))))


def _dot(a, b):
    return jnp.dot(a, b, preferred_element_type=F32)


def _dot_nt(a, b):
    return lax.dot_general(a, b, (((1,), (1,)), ((), ())), preferred_element_type=F32)


def _params(**kw):
    return pltpu.CompilerParams(vmem_limit_bytes=VMEM_LIMIT, **kw)


def _mm_kernel(x_ref, w_ref, o_ref):
    o_ref[...] = _dot(x_ref[...], w_ref[...]).astype(o_ref.dtype)


def matmul(x, w, out_dtype, tm, tn):
    m, k = x.shape
    n = w.shape[1]
    assert m % tm == 0 and n % tn == 0, (m, n, tm, tn)
    return pl.pallas_call(
        _mm_kernel,
        grid=(n // tn, m // tm),
        in_specs=[pl.BlockSpec((tm, k), lambda j, i: (i, 0)),
                  pl.BlockSpec((k, tn), lambda j, i: (0, j))],
        out_specs=pl.BlockSpec((tm, tn), lambda j, i: (i, j)),
        out_shape=jax.ShapeDtypeStruct((m, n), out_dtype),
        compiler_params=_params(),
        name="matmul",
    )(x, w)


def _mm_castw_kernel(x_ref, w_ref, o_ref, wb_ref):
    @pl.when(pl.program_id(1) == 0)
    def _():
        wb_ref[...] = w_ref[0].astype(BF16)

    o_ref[...] = _dot(x_ref[...], wb_ref[...]).astype(o_ref.dtype)


def matmul_f32w(x, w3, layer_idx, col0, n, out_dtype, tm, tn):
    m, k = x.shape
    assert m % tm == 0 and n % tn == 0 and col0 % tn == 0 and col0 + n <= w3.shape[2], (m, n, col0, tm, tn)
    j0 = col0 // tn
    return pl.pallas_call(
        _mm_castw_kernel,
        grid=(n // tn, m // tm),
        in_specs=[pl.BlockSpec((tm, k), lambda j, i: (i, 0)),
                  pl.BlockSpec((1, k, tn), lambda j, i: (layer_idx, 0, j0 + j))],
        out_specs=pl.BlockSpec((tm, tn), lambda j, i: (i, j)),
        out_shape=jax.ShapeDtypeStruct((m, n), out_dtype),
        scratch_shapes=[pltpu.VMEM((k, tn), BF16)],
        compiler_params=_params(dimension_semantics=("arbitrary", "arbitrary")),
        name="matmul_f32w",
    )(x, w3)


def _gmlp_kernel(u_ref, v_ref, ga_ref, lng_ref, lnb_ref, w_ref, bs_ref, o_ref, *, rows):
    gd = D_BR // GMLP_GROUPS
    for c in range(rows // GMLP_CHUNK):
        r = slice(c * GMLP_CHUNK, (c + 1) * GMLP_CHUNK)
        v = _gelu_tanh(v_ref[r, :])
        mu = jnp.mean(v, axis=-1, keepdims=True)
        vc = v - mu
        var = jnp.mean(vc * vc, axis=-1, keepdims=True)
        vb = (vc * lax.rsqrt(var + LN_EPS) * lng_ref[...] + lnb_ref[...]).astype(BF16)
        u = _gelu_tanh(u_ref[r, :]) * _silu(ga_ref[r, :])
        for g in range(GMLP_GROUPS):
            cs = slice(g * gd, (g + 1) * gd)
            mixed = _dot(w_ref[g], vb[:, cs]) + bs_ref[:, g:g + 1]
            o_ref[r, cs] = (u[:, cs] * mixed).astype(o_ref.dtype)


def gmlp_branch(h1, ln_g, ln_b, w_s, b_s, rows=512):
    m = h1.shape[0]
    rows = min(rows, m)
    causal = jnp.tril(jnp.ones((GMLP_CHUNK, GMLP_CHUNK), dtype=bool))
    w = jnp.where(causal, w_s, 0).astype(BF16)
    col = lambda c: pl.BlockSpec((rows, D_BR), lambda i, c=c: (i, c // D_BR))
    full = lambda a: pl.BlockSpec(a.shape, lambda i: (0,) * a.ndim)
    args = (ln_g.reshape(1, D_BR), ln_b.reshape(1, D_BR), w, b_s.T)
    return pl.pallas_call(
        functools.partial(_gmlp_kernel, rows=rows),
        grid=(m // rows,),
        in_specs=[col(C1_U), col(C1_V), col(C1_GA)] + [full(a) for a in args],
        out_specs=pl.BlockSpec((rows, D_BR), lambda i: (i, 0)),
        out_shape=jax.ShapeDtypeStruct((m, D_BR), BF16),
        compiler_params=_params(),
        name="gmlp",
    )(h1, h1, h1, *args)


def _lru_kernel(xb_ref, gb_ref, cw_ref, cb_ref, wa_ref, ba_ref, wx_ref, bx_ref, lam_ref, o_ref,
                xbuf, hcarry, a_s, g_s, *, ts):
    @pl.when(pl.program_id(1) == 0)
    def _():
        xbuf[0:8, :] = jnp.zeros((8, D_BR), F32)
        hcarry[...] = jnp.zeros((8, D_BR), F32)

    xbuf[8:8 + ts, :] = xb_ref[...]
    xc = cb_ref[...] + cw_ref[0:1, :] * xbuf[pl.ds(8 - (CONV_WIDTH - 1), ts), :]
    for k in range(1, CONV_WIDTH):
        xc = xc + cw_ref[k:k + 1, :] * xbuf[pl.ds(8 - (CONV_WIDTH - 1) + k, ts), :]
    xbuf[0:8, :] = xbuf[ts:ts + 8, :]

    xcb = xc.astype(BF16)
    bd = LRU_BLOCK_DIM
    r = jnp.concatenate([_dot(xcb[:, n * bd:(n + 1) * bd], wa_ref[n]) for n in range(LRU_BLOCKS)], axis=1)
    i = jnp.concatenate([_dot(xcb[:, n * bd:(n + 1) * bd], wx_ref[n]) for n in range(LRU_BLOCKS)], axis=1)
    r = _sigmoid(r + ba_ref[...])
    i = _sigmoid(i + bx_ref[...])
    nl = -lam_ref[...]
    softplus = jnp.maximum(nl, 0.0) + jnp.log1p(jnp.exp(-jnp.abs(nl)))
    log_a = (-LRU_C * softplus) * r
    a = jnp.exp(log_a)
    a_s[...] = a
    g_s[...] = jnp.sqrt(1.0 - a * a) * i * xc

    row = lax.broadcasted_iota(jnp.int32, (8, D_BR), 0)

    def body(j, carry):
        r0 = pl.multiple_of(j * 8, 8)
        av = a_s[pl.ds(r0, 8), :]
        bv = g_s[pl.ds(r0, 8), :]
        for d in (1, 2, 4):
            keep = row >= d
            a_sh = pltpu.roll(av, d, axis=0)
            b_sh = pltpu.roll(bv, d, axis=0)
            bv = jnp.where(keep, av * b_sh + bv, bv)
            av = jnp.where(keep, av * a_sh, av)
        hv = av * carry + bv
        g_s[pl.ds(r0, 8), :] = hv
        return jnp.broadcast_to(hv[7:8, :], (8, D_BR))

    hcarry[...] = lax.fori_loop(0, ts // 8, body, hcarry[...])
    o_ref[...] = (g_s[...] * _silu(gb_ref[...])).astype(o_ref.dtype)


def lru_branch(h1, bsz, seq, conv_w, conv_b, wa, ba, wx, bx, lam, ts=512):
    ts = min(ts, seq)
    ns = seq // ts
    col = lambda c: pl.BlockSpec((ts, D_BR), lambda b, s, c=c: (b * ns + s, c // D_BR))
    full = lambda a: pl.BlockSpec(a.shape, lambda b, s: (0,) * a.ndim)
    row = lambda a: a.reshape(1, D_BR)
    args = (conv_w, row(conv_b), wa.astype(BF16), row(ba), wx.astype(BF16), row(bx), row(lam))
    return pl.pallas_call(
        functools.partial(_lru_kernel, ts=ts),
        grid=(bsz, ns),
        in_specs=[col(C1_XB), col(C1_GB)] + [full(a) for a in args],
        out_specs=pl.BlockSpec((ts, D_BR), lambda b, s: (b * ns + s, 0)),
        out_shape=jax.ShapeDtypeStruct((bsz * seq, D_BR), BF16),
        scratch_shapes=[pltpu.VMEM((ts + 8, D_BR), F32), pltpu.VMEM((8, D_BR), F32),
                        pltpu.VMEM((ts, D_BR), F32), pltpu.VMEM((ts, D_BR), F32)],
        compiler_params=_params(dimension_semantics=("arbitrary", "arbitrary")),
        name="lru",
    )(h1, h1, *args)


def _mem_kernel(q_ref, g_ref, kv_ref, o_ref):
    hd = MEM_HEAD_DIM
    for hh in range(MEM_HEADS):
        cs = slice(hh * hd, (hh + 1) * hd)
        q = (q_ref[:, cs] * (hd ** -0.5)).astype(BF16)
        s = _dot_nt(q, kv_ref[:, cs])
        p = jnp.exp(s - jnp.max(s, axis=-1, keepdims=True))
        l = jnp.sum(p, axis=-1, keepdims=True)
        o = _dot(p.astype(BF16), kv_ref[:, D_BR + hh * hd:D_BR + (hh + 1) * hd]) / l
        o_ref[:, cs] = (o * _silu(g_ref[:, cs])).astype(o_ref.dtype)


def mem_branch(h2, kv, bsz, seq, tq=512):
    tq = min(tq, seq)
    nq = seq // tq
    mlen = kv.shape[0] // bsz
    col = lambda c: pl.BlockSpec((tq, D_BR), lambda b, i, c=c: (b * nq + i, c // D_BR))
    return pl.pallas_call(
        _mem_kernel,
        grid=(bsz, nq),
        in_specs=[col(C2_QM), col(C2_GM), pl.BlockSpec((mlen, 2 * D_BR), lambda b, i: (b, 0))],
        out_specs=pl.BlockSpec((tq, D_BR), lambda b, i: (b * nq + i, 0)),
        out_shape=jax.ShapeDtypeStruct((bsz * seq, D_BR), BF16),
        compiler_params=_params(),
        name="mem_attn",
    )(h2, h2, kv)


def _cmp_kernel(tk_ref, tv_ref, pek_ref, pev_ref, w1k_ref, w1v_ref, w2kt_ref, w2v_ref, kct_ref, vc_ref):
    n = tk_ref.shape[1]

    def hidden(t_ref, pe_ref, w1_ref):
        t = t_ref[0]
        lo = _dot((t + pe_ref[0:1, :]).astype(BF16), w1_ref[0])
        hi = _dot((t + pe_ref[1:2, :]).astype(BF16), w1_ref[1])
        pre = lo + pltpu.roll(hi, n - 1, axis=0)
        return _silu(pre).astype(BF16)

    kct_ref[0] = _dot_nt(w2kt_ref[...], hidden(tk_ref, pek_ref, w1k_ref)).astype(kct_ref.dtype)
    vc_ref[0] = _dot(hidden(tv_ref, pev_ref, w1v_ref), w2v_ref[...]).astype(vc_ref.dtype)


def nsa_compress(tk, tv, pe_k, pe_v, w1_k, w1_v, w2_k, w2_v):
    nb, n, width = tk.shape
    half = CMP_STRIDE * DK
    args = (pe_k.reshape(2, half), pe_v.reshape(2, half),
            w1_k.reshape(2, half, CMP_HIDDEN).astype(BF16), w1_v.reshape(2, half, CMP_HIDDEN).astype(BF16),
            w2_k.T.astype(BF16), w2_v.astype(BF16))
    full = lambda a: pl.BlockSpec(a.shape, lambda b: (0,) * a.ndim)
    blk = pl.BlockSpec((1, n, width), lambda b: (b, 0, 0))
    return pl.pallas_call(
        _cmp_kernel,
        grid=(nb,),
        in_specs=[blk, blk] + [full(a) for a in args],
        out_specs=[pl.BlockSpec((1, DK, n), lambda b: (b, 0, 0)), pl.BlockSpec((1, n, DK), lambda b: (b, 0, 0))],
        out_shape=[jax.ShapeDtypeStruct((nb, DK, n), BF16), jax.ShapeDtypeStruct((nb, n, DK), BF16)],
        compiler_params=_params(),
        name="nsa_compress",
    )(tk, tv, *args)


def _nsa_select_kernel(q_ref, gl_ref, gb_ref, kct_ref, vc_ref, bc_ref, ovt_ref, qsel_ref, ocg_ref, vt_s):
    qb = Q_BLOCK
    lane = lax.broadcasted_iota(jnp.int32, (qb, LANES), 1)
    jrow = lax.broadcasted_iota(jnp.int32, (MASK_ROWS, qb), 0).astype(F32)
    qlane = lax.broadcasted_iota(jnp.int32, (MASK_ROWS, qb), 1)
    sub = lax.broadcasted_iota(jnp.int32, (SUBLANES, qb), 0)
    nv = MASK_ROWS // SUBLANES
    ovt = ovt_ref[...]
    kct = kct_ref[0]
    vc = vc_ref[0]
    for sb in range(SEL_SUB):
        rows = slice(sb * qb, (sb + 1) * qb)
        blk = pl.program_id(2) * SEL_SUB + sb
        q = q_ref[rows, :] * (DK ** -0.5)
        gt = _sigmoid(gl_ref[rows, :] + gb_ref[...])
        qpad = []
        for g in range(GQA):
            t = q[:, (g // 2) * LANES:(g // 2 + 1) * LANES]
            if g % 2:
                t = pltpu.roll(t, DK, axis=1)
            qpad.append(jnp.where(lane < DK, t, 0.0))

        o_c = []
        psum = None
        for g in range(GQA):
            s = _dot(qpad[g][:, :DK].astype(BF16), kct) + bc_ref[0, sb, g * qb:(g + 1) * qb, :]
            m = jnp.max(s, axis=-1, keepdims=True)
            p = jnp.where(s > 0.1 * NEG, jnp.exp(s - m), 0.0)
            p = p / jnp.maximum(jnp.sum(p, axis=-1, keepdims=True), 1e-30)
            o_c.append(gt[:, g:g + 1] * _dot(p.astype(BF16), vc))
            psum = p if psum is None else psum + p
        ocg_ref[rows, :] = jnp.concatenate(o_c, axis=1)

        p_hi = psum.astype(BF16)
        rem = psum - p_hi.astype(F32)
        p_mid = rem.astype(BF16)
        p_lo = (rem - p_mid.astype(F32)).astype(BF16)
        imp_t = _dot_nt(ovt, p_hi) + _dot_nt(ovt, p_mid) + _dot_nt(ovt, p_lo)

        qblk = ((blk * qb + qlane) >> int(math.log2(SLC_BLOCK))).astype(F32)
        val = jnp.where(jrow == 0.0, 3e38,
                        jnp.where(jrow == qblk, 3e38, jnp.where(jrow == qblk - 1.0, 3e38, imp_t)))
        val = jnp.where(jrow > qblk, -1.0, val)
        vt_s[sb] = val
        vals = [val[k * SUBLANES:(k + 1) * SUBLANES] for k in range(nv)]
        cnt = [jnp.zeros((SUBLANES, qb), F32) for _ in range(nv)]
        for jp in range(MASK_ROWS):
            rowv = jnp.broadcast_to(vt_s[sb, jp:jp + 1, :], (SUBLANES, qb))
            for k in range(nv):
                if k * SUBLANES > jp:
                    beat = jnp.where(rowv >= vals[k], 1.0, 0.0)
                elif (k + 1) * SUBLANES - 1 < jp:
                    beat = jnp.where(rowv > vals[k], 1.0, 0.0)
                else:
                    beat = jnp.where(sub > jp - k * SUBLANES, jnp.where(rowv >= vals[k], 1.0, 0.0),
                                     jnp.where(rowv > vals[k], 1.0, 0.0))
                cnt[k] = cnt[k] + beat
        selneg_t = jnp.concatenate([jnp.where(c < float(SLC_TOPK), 0.0, NEG) for c in cnt], axis=0)
        selpad = jnp.concatenate([jnp.zeros((DK, qb), F32), selneg_t], axis=0).T
        for g in range(GQA):
            qsel_ref[rows, g * LANES:(g + 1) * LANES] = (qpad[g] + selpad).astype(qsel_ref.dtype)


def _nsa_attn_kernel(qsel_ref, ocg_ref, gc_ref, gl_ref, gb_ref, ks_ref, vs_ref, gs_ref, kw_ref, vw_ref, bw_ref,
                     o_ref, s_s, p_s):
    i = pl.program_id(2)
    qb = Q_BLOCK
    step = SEL_TK // qb
    n_tiles = (i * qb) // SEL_TK + 1

    def stage_qk(j):
        k0 = pl.multiple_of(j * SEL_TK, SEL_TK)
        kt = ks_ref[0, 0, :, pl.ds(k0, SEL_TK)]
        dd = jnp.minimum(i - j * step, SEL_FAR)
        boff = pl.multiple_of((SEL_FAR - dd) * qb, qb)
        for g in range(GQA):
            s_s[g] = (_dot(qsel_ref[:, g * LANES:(g + 1) * LANES], kt)
                      + gs_ref[0, g * qb:(g + 1) * qb, pl.ds(boff, SEL_TK)])

    def stage_softmax(ms):
        new_m, alphas = [], []
        for g in range(GQA):
            sc = s_s[g]
            m_new = jnp.maximum(ms[g], jnp.max(sc, axis=-1, keepdims=True))
            alphas.append(jnp.exp(ms[g] - m_new))
            p_s[g] = jnp.exp(sc - m_new).astype(BF16)
            new_m.append(m_new)
        return tuple(new_m), tuple(alphas)

    def stage_pv(j, accs):
        k0 = pl.multiple_of(jnp.maximum(j, 0) * SEL_TK, SEL_TK)
        vt = vs_ref[0, 0, pl.ds(k0, SEL_TK), :]
        return tuple(accs[g] + _dot(p_s[g], vt) for g in range(GQA))

    def advance(j, ms, accs):
        accs = stage_pv(j - 2, accs)
        ms, alphas = stage_softmax(ms)
        return ms, tuple(alphas[g] * accs[g] for g in range(GQA))

    def body(j, carry):
        ms, accs = advance(j, *carry)
        stage_qk(j)
        return ms, accs

    p_s[...] = jnp.zeros(p_s.shape, BF16)
    stage_qk(jnp.int32(0))
    init = (tuple(jnp.full((qb, 1), NEG, F32) for _ in range(GQA)),
            tuple(jnp.zeros((qb, LANES), F32) for _ in range(GQA)))
    ms, accs = lax.fori_loop(1, n_tiles, body, init)
    _, accs = advance(n_tiles, ms, accs)
    acc_s = stage_pv(n_tiles - 1, accs)

    wk = WINDOW + qb
    w0 = pl.multiple_of(i * qb, qb)
    kwt = kw_ref[0, 0, :, pl.ds(w0, wk)]
    vwt = vw_ref[0, 0, pl.ds(w0, wk), :]
    lane = lax.broadcasted_iota(jnp.int32, (qb, LANES), 1)
    win_mask = jnp.where(lane == DK, NEG, 0.0).astype(BF16)

    gt = _sigmoid(gl_ref[...] + gb_ref[...])
    outs = []
    for g in range(GQA):
        q_win = jnp.where(lane < DK, qsel_ref[:, g * LANES:(g + 1) * LANES], win_mask)
        sw = _dot(q_win, kwt) + bw_ref[0, g * qb:(g + 1) * qb, :]
        pw = jnp.exp(sw - jnp.max(sw, axis=-1, keepdims=True)).astype(BF16)
        acc_w = _dot(pw, vwt)
        o_w = acc_w[:, :DK] / acc_w[:, DK:DK + 1]
        o_s = acc_s[g][:, :DK] / acc_s[g][:, DK:DK + 1]
        outs.append(gt[:, GQA + g:GQA + g + 1] * o_s + gt[:, 2 * GQA + g:2 * GQA + g + 1] * o_w)
    o = ocg_ref[...] + jnp.concatenate(outs, axis=1)
    o_ref[...] = (o * _silu(gc_ref[...])).astype(o_ref.dtype)


def _rel_bucket(dist):
    n = jnp.maximum(dist, 0)
    exact = REL_BUCKETS // 2
    nf = jnp.maximum(n, 1).astype(jnp.float32)
    large = exact + (jnp.log(nf / exact) / math.log(REL_MAX_DIST / exact)
                     * (REL_BUCKETS - exact)).astype(jnp.int32)
    return jnp.where(n < exact, n, jnp.minimum(large, REL_BUCKETS - 1))


def nsa_bias_tables(rel_bias, seq):
    qb = Q_BLOCK
    n_pad = seq // CMP_STRIDE
    c0 = n_pad - NA
    wc = 2 * n_pad
    ws = SEL_FAR * qb + SEL_TK
    nmax = max(seq, SEL_FAR * qb + qb)
    padl = CMP_STRIDE * (wc + 2)
    bvec = rel_bias[_rel_bucket(jnp.arange(nmax))].T.astype(F32)
    vext = jnp.concatenate([jnp.full((NSA_HEADS, padl), NEG, F32), bvec], axis=1)
    pos = jnp.arange(padl + nmax) - padl
    vwin = jnp.where(pos < WINDOW, vext, NEG)

    mlo, mhi = c0 - wc + 1, NA - 1 + c0
    start = padl + CMP_STRIDE * mlo - (CMP_BLOCK - 1)
    assert start >= 0
    u = vext[:, start:start + CMP_STRIDE * (mhi - mlo + 1)].reshape(NSA_HEADS, mhi - mlo + 1, CMP_STRIDE)
    urev = u[:, ::-1, :]
    gcb = jnp.stack([urev[:, NA - 1 - a:NA - 1 - a + wc, :] for a in range(NA)], axis=1)
    gcb = gcb.transpose(0, 1, 3, 2).reshape(HKV, GQA * qb, wc)
    nqb = seq // qb
    bc = jnp.stack([gcb[:, :, c0 - NA * i:c0 - NA * i + n_pad] for i in range(nqb)], axis=1)

    def toeplitz(vec, d0, width):
        vrev = vec[:, ::-1]
        s0 = nmax - d0 - qb
        period = width + qb - 1
        assert s0 >= 0 and s0 + period <= padl + nmax
        sl = vrev[:, s0:s0 + period]
        y = jnp.concatenate([sl[:, qb - 1:], sl[:, :qb - 1]], axis=1)
        z = jnp.tile(y, (1, qb))[:, :qb * (period - 1)].reshape(NSA_HEADS, qb, period - 1)
        return z[:, :, :width].reshape(HKV, GQA * qb, width)

    gs = toeplitz(vext, SEL_FAR * qb, ws)
    bw = toeplitz(vwin, WINDOW, WINDOW + qb)
    return bc, gs, bw


def _overlap_matrix_t(n_pad):
    c_start = np.arange(n_pad)[None, :] * CMP_STRIDE
    s_start = np.arange(MASK_ROWS)[:, None] * SLC_BLOCK
    ov = np.clip(np.minimum(c_start + CMP_BLOCK, s_start + SLC_BLOCK) - np.maximum(c_start, s_start), 0, None)
    return ov.astype(np.float32) / CMP_BLOCK


def nsa_branch(h1, h2, hkv, hgl, gate_b4, kct, vc, tables, bsz, seq):
    qb = Q_BLOCK
    nqb = seq // qb
    n_pad = seq // CMP_STRIDE
    n_slc = seq // SLC_BLOCK
    assert n_slc <= MASK_ROWS and seq % SEL_TK == 0 and nqb % SEL_SUB == 0
    bc, gs, bw = tables
    m = bsz * seq
    gw = GQA * DK

    sq = SEL_SUB * qb
    nsq = seq // sq
    ovt = jnp.asarray(_overlap_matrix_t(n_pad), BF16)
    qsel, ocg = pl.pallas_call(
        _nsa_select_kernel,
        grid=(HKV, bsz, nsq),
        in_specs=[
            pl.BlockSpec((sq, gw), lambda hh, b, i: (b * nsq + i, C1_QC // gw + hh)),
            pl.BlockSpec((sq, LANES), lambda hh, b, i: (b * nsq + i, hh)),
            pl.BlockSpec((1, LANES), lambda hh, b, i: (0, hh)),
            pl.BlockSpec((1, DK, n_pad), lambda hh, b, i: (b * HKV + hh, 0, 0)),
            pl.BlockSpec((1, n_pad, DK), lambda hh, b, i: (b * HKV + hh, 0, 0)),
            pl.BlockSpec((1, SEL_SUB) + bc.shape[2:], lambda hh, b, i: (hh, i, 0, 0)),
            pl.BlockSpec(ovt.shape, lambda hh, b, i: (0, 0)),
        ],
        out_specs=[pl.BlockSpec((sq, GQA * LANES), lambda hh, b, i: (b * nsq + i, hh)),
                   pl.BlockSpec((sq, gw), lambda hh, b, i: (b * nsq + i, hh))],
        out_shape=[jax.ShapeDtypeStruct((m, HKV * GQA * LANES), BF16), jax.ShapeDtypeStruct((m, D_BR), F32)],
        scratch_shapes=[pltpu.VMEM((SEL_SUB, MASK_ROWS, qb), F32)],
        compiler_params=_params(),
        name="nsa_select",
    )(h1, hgl, gate_b4, kct, vc, bc, ovt)

    def heads_t(c):
        return hkv[:, c:c + KV_W].reshape(bsz, seq, HKV, DK).transpose(0, 2, 3, 1)

    def heads_ones(c):
        v = hkv[:, c:c + KV_W].reshape(bsz, seq, HKV, DK).transpose(0, 2, 1, 3)
        tail = np.zeros((LANES - DK,), np.float32)
        tail[0] = 1.0
        return jnp.concatenate([v, jnp.broadcast_to(jnp.asarray(tail, BF16), v.shape[:3] + (LANES - DK,))], axis=3)

    e_rows = (np.arange(seq)[None, :] // SLC_BLOCK == np.arange(MASK_ROWS)[:, None])
    e_rows = jnp.broadcast_to(jnp.asarray(e_rows, BF16), (bsz, HKV, MASK_ROWS, seq))
    ks_ext = jnp.concatenate([heads_t(0), e_rows], axis=2)
    vs_ext = heads_ones(KV_W)
    invalid = np.zeros((MASK_ROWS, seq + WINDOW), np.float32)
    invalid[0, :WINDOW] = 1.0
    kw_pad = jnp.pad(heads_t(2 * KV_W), ((0, 0), (0, 0), (0, 0), (WINDOW, 0)))
    kw_ext = jnp.concatenate(
        [kw_pad, jnp.broadcast_to(jnp.asarray(invalid, BF16), (bsz, HKV, MASK_ROWS, seq + WINDOW))], axis=2)
    vw_ext = jnp.pad(heads_ones(3 * KV_W), ((0, 0), (0, 0), (WINDOW, 0), (0, 0)))

    rowblk = lambda width, c0: pl.BlockSpec((qb, width), lambda hh, b, i: (b * nqb + i, c0 // width + hh))
    per_bh = lambda a: pl.BlockSpec((1, 1) + a.shape[2:], lambda hh, b, i: (b, hh, 0, 0))
    per_h = lambda a: pl.BlockSpec((1,) + a.shape[1:], lambda hh, b, i: (hh, 0, 0))
    return pl.pallas_call(
        _nsa_attn_kernel,
        grid=(HKV, bsz, nqb),
        in_specs=[
            rowblk(GQA * LANES, 0), rowblk(gw, 0), rowblk(gw, C2_GC),
            pl.BlockSpec((qb, LANES), lambda hh, b, i: (b * nqb + i, hh)),
            pl.BlockSpec((1, LANES), lambda hh, b, i: (0, hh)),
            per_bh(ks_ext), per_bh(vs_ext), per_h(gs),
            per_bh(kw_ext), per_bh(vw_ext), per_h(bw),
        ],
        out_specs=pl.BlockSpec((qb, gw), lambda hh, b, i: (b * nqb + i, hh)),
        out_shape=jax.ShapeDtypeStruct((m, D_BR), BF16),
        scratch_shapes=[pltpu.VMEM((GQA, qb, SEL_TK), F32), pltpu.VMEM((GQA, qb, SEL_TK), BF16)],
        compiler_params=_params(),
        name="nsa_attn",
    )(qsel, ocg, h2, hgl, gate_b4, ks_ext, vs_ext, gs, kw_ext, vw_ext, bw)


def _final_kernel(oa_ref, ob_ref, oc_ref, om_ref, *rest):
    gate_refs = rest[:2 * N_BRANCHES]
    x_ref, wb_ref, wo_ref, lng_ref, lnb_ref, y_ref, yb_ref = rest[2 * N_BRANCHES:]
    o_refs = (oa_ref, ob_ref, oc_ref, om_ref)
    halves = []
    for c in range(2):
        cs = slice(c * D_BR, (c + 1) * D_BR)
        acc = None
        for k in range(N_BRANCHES):
            term = _sigmoid(gate_refs[2 * k + c][...]) * _dot(o_refs[k][...], wb_ref[k, :, cs])
            acc = term if acc is None else acc + term
        halves.append(acc.astype(BF16))
    merged = jnp.concatenate(halves, axis=1)
    z = DN_ALPHA * x_ref[...] + _dot(merged, wo_ref[...])
    mu = jnp.mean(z, axis=-1, keepdims=True)
    zc = z - mu
    var = jnp.mean(zc * zc, axis=-1, keepdims=True)
    y = zc * lax.rsqrt(var + LN_EPS) * lng_ref[...] + lnb_ref[...]
    y_ref[...] = y
    yb_ref[...] = y.astype(BF16)


def final_merge(o_a, o_b, o_c, o_m, h2, x, w_branch, w_out, ln_g, ln_b, tm=128):
    m = x.shape[0]
    tm = min(tm, m)
    br = pl.BlockSpec((tm, D_BR), lambda i: (i, 0))
    gate = lambda k: pl.BlockSpec((tm, D_BR), lambda i, k=k: (i, C2_MERGE // D_BR + k))
    xs = pl.BlockSpec((tm, D_MODEL), lambda i: (i, 0))
    resident = lambda a: pl.BlockSpec(a.shape, lambda i: (0,) * a.ndim, pipeline_mode=pl.Buffered(1))
    wb = w_branch.astype(BF16)
    wo = w_out.astype(BF16)
    lg, lb = ln_g.reshape(1, D_MODEL), ln_b.reshape(1, D_MODEL)
    ngate = 2 * N_BRANCHES
    return pl.pallas_call(
        _final_kernel,
        grid=(m // tm,),
        in_specs=[br, br, br, br] + [gate(k) for k in range(ngate)] + [xs]
                 + [resident(wb), resident(wo), resident(lg), resident(lb)],
        out_specs=[xs, xs],
        out_shape=[jax.ShapeDtypeStruct((m, D_MODEL), F32), jax.ShapeDtypeStruct((m, D_MODEL), BF16)],
        compiler_params=_params(),
        name="merge_out_ln",
    )(o_a, o_b, o_c, o_m, *([h2] * ngate), x, wb, wo, lg, lb)


def _gate_col_map():
    src = np.full((HKV * LANES,), -1, np.int64)
    for hh in range(HKV):
        for brn in range(3):
            for g in range(GQA):
                src[hh * LANES + brn * GQA + g] = brn * NSA_HEADS + hh * GQA + g
    return src


def _spread_gate_cols(a):
    src = _gate_col_map()
    return jnp.where(jnp.asarray(src >= 0), a[..., np.maximum(src, 0)], 0)


def layer(l, x, xb, mem_b, tables, bsz, seq, w_in, w_tail, sgu_ln_g, sgu_ln_b, sgu_w, sgu_b, conv_w, conv_b,
          lru_wa, lru_ba, lru_wx, lru_bx, lru_lambda, cmp_pe_k, cmp_pe_v, cmp_w1_k, cmp_w1_v,
          cmp_w2_k, cmp_w2_v, nsa_gate_b, w_mem_kv, w_branch, w_out, ln_g, ln_b):
    m = bsz * seq
    tm = min(1024, m)
    h1 = matmul_f32w(xb, w_in, l, 0, N1, F32, tm, 512)
    hkv = matmul_f32w(xb, w_in, l, N1, NKV, BF16, tm, 512)
    h2 = matmul_f32w(xb, w_tail, l, 0, N2, F32, tm, 512)
    w_gl = _spread_gate_cols(w_in[l, :, GL_OFF:GL_OFF + 3 * NSA_HEADS]).astype(BF16)
    hgl = matmul(xb, w_gl, F32, tm, HKV * LANES)
    gate_b4 = _spread_gate_cols(nsa_gate_b[l]).reshape(1, HKV * LANES)

    o_a = gmlp_branch(h1, sgu_ln_g[l], sgu_ln_b[l], sgu_w[l], sgu_b[l])
    o_b = lru_branch(h1, bsz, seq, conv_w[l], conv_b[l], lru_wa[l], lru_ba[l], lru_wx[l], lru_bx[l],
                     lru_lambda[l])

    def stride_blocks(c):
        t = h1[:, c:c + KV_W].reshape(bsz, seq // CMP_STRIDE, CMP_STRIDE, HKV, DK)
        return t.transpose(0, 3, 1, 2, 4).reshape(bsz * HKV, seq // CMP_STRIDE, CMP_STRIDE * DK)

    kct, vc = nsa_compress(stride_blocks(C1_KV), stride_blocks(C1_KV + KV_W), cmp_pe_k[l], cmp_pe_v[l],
                           cmp_w1_k[l], cmp_w1_v[l], cmp_w2_k[l], cmp_w2_v[l])
    o_c = nsa_branch(h1, h2, hkv, hgl, gate_b4, kct, vc, tables, bsz, seq)

    mrows = mem_b.shape[0]
    kv = matmul_f32w(mem_b, w_mem_kv, l, 0, 2 * D_BR, BF16, min(512, mrows), 512)
    o_m = mem_branch(h2, kv, bsz, seq)
    return final_merge(o_a, o_b, o_c, o_m, h2, x, w_branch[l], w_out[l], ln_g[l], ln_b[l])


def kernel(x, mem, rel_bias, w_in, sgu_ln_g, sgu_ln_b, sgu_w, sgu_b, conv_w, conv_b, lru_wa, lru_ba, lru_wx,
           lru_bx, lru_lambda, cmp_pe_k, cmp_pe_v, cmp_w1_k, cmp_w1_v, cmp_w2_k, cmp_w2_v, nsa_gate_b,
           w_mem_kv, w_branch, w_out, ln_g, ln_b):
    bsz, seq, _ = x.shape
    tables = nsa_bias_tables(rel_bias, seq)
    xf = x.reshape(bsz * seq, D_MODEL)
    xb = xf.astype(BF16)
    mem_b = mem.reshape(-1, D_MODEL).astype(BF16)
    w_tail = w_in[:, :, GL_OFF + 3 * NSA_HEADS:]
    params = (w_in, w_tail, sgu_ln_g, sgu_ln_b, sgu_w, sgu_b, conv_w, conv_b, lru_wa, lru_ba, lru_wx, lru_bx,
              lru_lambda, cmp_pe_k, cmp_pe_v, cmp_w1_k, cmp_w1_v, cmp_w2_k, cmp_w2_v, nsa_gate_b,
              w_mem_kv, w_branch, w_out, ln_g, ln_b)
    for l in range(w_in.shape[0]):
        xf, xb = layer(l, xf, xb, mem_b, tables, bsz, seq, *params)
    return xf.reshape(bsz, seq, D_MODEL)
```

```python
import functools
import math

import numpy as np
import jax
import jax.numpy as jnp
from jax import lax
from jax.experimental import pallas as pl
from jax.experimental.pallas import tpu as pltpu

F32 = jnp.float32
BF16 = jnp.bfloat16

D_MODEL = 2048
DEPTH = 2
D_BR = D_MODEL // 2
N_BRANCHES = 4
GMLP_CHUNK = 128
GMLP_GROUPS = 8
LRU_BLOCKS = 8
LRU_BLOCK_DIM = D_BR // LRU_BLOCKS
CONV_WIDTH = 4
LRU_C = 8.0
DK = 64
NSA_HEADS = D_BR // DK
HKV = NSA_HEADS // 4
GQA = NSA_HEADS // HKV
KV_W = HKV * DK
CMP_BLOCK = 32
CMP_STRIDE = 16
CMP_HIDDEN = 256
SLC_BLOCK = 64
SLC_TOPK = 8
WINDOW = 256
Q_BLOCK = 128
MEM_HEADS = 4
MEM_HEAD_DIM = D_BR // MEM_HEADS
REL_BUCKETS = 32
REL_MAX_DIST = 1024
DN_ALPHA = (2 * DEPTH) ** 0.25
LN_EPS = 1e-5

LANES = 128
SUBLANES = 8
NEG = -1e30
SEL_TK = 512
SEL_FAR = 11
MASK_ROWS = LANES - DK
NA = Q_BLOCK // CMP_STRIDE
SEL_SUB = 4
ATT_NQ = 2
VMEM_LIMIT = 56 * 1024 * 1024

C1_U, C1_V, C1_GA, C1_XB, C1_GB, C1_QC = (D_BR * k for k in range(6))
C1_KV = 6 * D_BR
N1 = C1_KV + 2 * KV_W
NKV = 4 * KV_W
C2_GC, C2_QM, C2_GM, C2_MERGE = (D_BR * k for k in range(4))
N2 = C2_MERGE + N_BRANCHES * D_MODEL
GL_OFF = N1 + NKV


def _sigmoid(x):
    return 1.0 / (1.0 + jnp.exp(-x))


def _silu(x):
    return x * _sigmoid(x)


def _gelu_tanh(x):
    return 0.5 * x * (1.0 + jnp.tanh(math.sqrt(2.0 / math.pi) * (x + 0.044715 * (x * x * x))))


def _dot(a, b):
    return jnp.dot(a, b, preferred_element_type=F32)


def _dot_nt(a, b):
    return lax.dot_general(a, b, (((1,), (1,)), ((), ())), preferred_element_type=F32)


def _params(**kw):
    return pltpu.CompilerParams(vmem_limit_bytes=VMEM_LIMIT, **kw)


def _mm_kernel(x_ref, w_ref, o_ref):
    o_ref[...] = _dot(x_ref[...], w_ref[...]).astype(o_ref.dtype)


def matmul(x, w, out_dtype, tm, tn):
    m, k = x.shape
    n = w.shape[1]
    assert m % tm == 0 and n % tn == 0, (m, n, tm, tn)
    return pl.pallas_call(
        _mm_kernel,
        grid=(n // tn, m // tm),
        in_specs=[pl.BlockSpec((tm, k), lambda j, i: (i, 0)),
                  pl.BlockSpec((k, tn), lambda j, i: (0, j))],
        out_specs=pl.BlockSpec((tm, tn), lambda j, i: (i, j)),
        out_shape=jax.ShapeDtypeStruct((m, n), out_dtype),
        compiler_params=_params(),
        name="matmul",
    )(x, w)


def _mm_castw_kernel(x_ref, w_ref, o_ref, wb_ref, *, w_is_nk):
    @pl.when(pl.program_id(1) == 0)
    def _():
        w = w_ref[0]
        wb_ref[...] = (w.T if w_is_nk else w).astype(BF16)

    o_ref[...] = _dot(x_ref[...], wb_ref[...]).astype(o_ref.dtype)


def matmul_f32w(x, w3, layer_idx, col0, n, out_dtype, tm, tn, w_is_nk=False):
    m, k = x.shape
    ncols = w3.shape[1] if w_is_nk else w3.shape[2]
    assert m % tm == 0 and n % tn == 0 and col0 % tn == 0 and col0 + n <= ncols, (m, n, col0, tm, tn)
    j0 = col0 // tn
    if w_is_nk:
        w_spec = pl.BlockSpec((1, tn, k), lambda j, i: (layer_idx, j0 + j, 0))
    else:
        w_spec = pl.BlockSpec((1, k, tn), lambda j, i: (layer_idx, 0, j0 + j))
    return pl.pallas_call(
        functools.partial(_mm_castw_kernel, w_is_nk=w_is_nk),
        grid=(n // tn, m // tm),
        in_specs=[pl.BlockSpec((tm, k), lambda j, i: (i, 0)), w_spec],
        out_specs=pl.BlockSpec((tm, tn), lambda j, i: (i, j)),
        out_shape=jax.ShapeDtypeStruct((m, n), out_dtype),
        scratch_shapes=[pltpu.VMEM((k, tn), BF16)],
        compiler_params=_params(dimension_semantics=("arbitrary", "arbitrary")),
        name="matmul_f32w",
    )(x, w3)


def _gmlp_kernel(u_ref, v_ref, ga_ref, lng_ref, lnb_ref, w_ref, bs_ref, o_ref, *, rows):
    gd = D_BR // GMLP_GROUPS
    for c in range(rows // GMLP_CHUNK):
        r = slice(c * GMLP_CHUNK, (c + 1) * GMLP_CHUNK)
        v = _gelu_tanh(v_ref[r, :])
        mu = jnp.mean(v, axis=-1, keepdims=True)
        vc = v - mu
        var = jnp.mean(vc * vc, axis=-1, keepdims=True)
        vb = (vc * lax.rsqrt(var + LN_EPS) * lng_ref[...] + lnb_ref[...]).astype(BF16)
        u = _gelu_tanh(u_ref[r, :]) * _silu(ga_ref[r, :])
        for g in range(GMLP_GROUPS):
            cs = slice(g * gd, (g + 1) * gd)
            mixed = _dot(w_ref[g], vb[:, cs]) + bs_ref[:, g:g + 1]
            o_ref[r, cs] = (u[:, cs] * mixed).astype(o_ref.dtype)


def gmlp_branch(h1, ln_g, ln_b, w_s, b_s, rows=512):
    m = h1.shape[0]
    rows = min(rows, m)
    causal = jnp.tril(jnp.ones((GMLP_CHUNK, GMLP_CHUNK), dtype=bool))
    w = jnp.where(causal, w_s, 0).astype(BF16)
    col = lambda c: pl.BlockSpec((rows, D_BR), lambda i, c=c: (i, c // D_BR))
    full = lambda a: pl.BlockSpec(a.shape, lambda i: (0,) * a.ndim)
    args = (ln_g.reshape(1, D_BR), ln_b.reshape(1, D_BR), w, b_s.T)
    return pl.pallas_call(
        functools.partial(_gmlp_kernel, rows=rows),
        grid=(m // rows,),
        in_specs=[col(C1_U), col(C1_V), col(C1_GA)] + [full(a) for a in args],
        out_specs=pl.BlockSpec((rows, D_BR), lambda i: (i, 0)),
        out_shape=jax.ShapeDtypeStruct((m, D_BR), BF16),
        compiler_params=_params(),
        name="gmlp",
    )(h1, h1, h1, *args)


def _lru_kernel(xb_ref, gb_ref, cw_ref, cb_ref, wa_ref, ba_ref, wx_ref, bx_ref, lam_ref, o_ref,
                xbuf, hcarry, a_s, g_s, *, ts):
    @pl.when(pl.program_id(1) == 0)
    def _():
        xbuf[0:8, :] = jnp.zeros((8, D_BR), F32)
        hcarry[...] = jnp.zeros((8, D_BR), F32)

    xbuf[8:8 + ts, :] = xb_ref[...]
    xc = cb_ref[...] + cw_ref[0:1, :] * xbuf[pl.ds(8 - (CONV_WIDTH - 1), ts), :]
    for k in range(1, CONV_WIDTH):
        xc = xc + cw_ref[k:k + 1, :] * xbuf[pl.ds(8 - (CONV_WIDTH - 1) + k, ts), :]
    xbuf[0:8, :] = xbuf[ts:ts + 8, :]

    xcb = xc.astype(BF16)
    bd = LRU_BLOCK_DIM
    r = jnp.concatenate([_dot(xcb[:, n * bd:(n + 1) * bd], wa_ref[n]) for n in range(LRU_BLOCKS)], axis=1)
    i = jnp.concatenate([_dot(xcb[:, n * bd:(n + 1) * bd], wx_ref[n]) for n in range(LRU_BLOCKS)], axis=1)
    r = _sigmoid(r + ba_ref[...])
    i = _sigmoid(i + bx_ref[...])
    nl = -lam_ref[...]
    softplus = jnp.maximum(nl, 0.0) + jnp.log1p(jnp.exp(-jnp.abs(nl)))
    log_a = (-LRU_C * softplus) * r
    a = jnp.exp(log_a)
    a_s[...] = a
    g_s[...] = jnp.sqrt(1.0 - a * a) * i * xc

    row = lax.broadcasted_iota(jnp.int32, (8, D_BR), 0)

    def body(j, carry):
        r0 = pl.multiple_of(j * 8, 8)
        av = a_s[pl.ds(r0, 8), :]
        bv = g_s[pl.ds(r0, 8), :]
        for d in (1, 2, 4):
            keep = row >= d
            a_sh = pltpu.roll(av, d, axis=0)
            b_sh = pltpu.roll(bv, d, axis=0)
            bv = jnp.where(keep, av * b_sh + bv, bv)
            av = jnp.where(keep, av * a_sh, av)
        hv = av * carry + bv
        g_s[pl.ds(r0, 8), :] = hv
        return jnp.broadcast_to(hv[7:8, :], (8, D_BR))

    hcarry[...] = lax.fori_loop(0, ts // 8, body, hcarry[...])
    o_ref[...] = (g_s[...] * _silu(gb_ref[...])).astype(o_ref.dtype)


def lru_branch(h1, bsz, seq, conv_w, conv_b, wa, ba, wx, bx, lam, ts=512):
    ts = min(ts, seq)
    ns = seq // ts
    col = lambda c: pl.BlockSpec((ts, D_BR), lambda b, s, c=c: (b * ns + s, c // D_BR))
    full = lambda a: pl.BlockSpec(a.shape, lambda b, s: (0,) * a.ndim)
    row = lambda a: a.reshape(1, D_BR)
    args = (conv_w, row(conv_b), wa.astype(BF16), row(ba), wx.astype(BF16), row(bx), row(lam))
    return pl.pallas_call(
        functools.partial(_lru_kernel, ts=ts),
        grid=(bsz, ns),
        in_specs=[col(C1_XB), col(C1_GB)] + [full(a) for a in args],
        out_specs=pl.BlockSpec((ts, D_BR), lambda b, s: (b * ns + s, 0)),
        out_shape=jax.ShapeDtypeStruct((bsz * seq, D_BR), BF16),
        scratch_shapes=[pltpu.VMEM((ts + 8, D_BR), F32), pltpu.VMEM((8, D_BR), F32),
                        pltpu.VMEM((ts, D_BR), F32), pltpu.VMEM((ts, D_BR), F32)],
        compiler_params=_params(dimension_semantics=("arbitrary", "arbitrary")),
        name="lru",
    )(h1, h1, *args)


def _mem_kernel(q_ref, g_ref, kv_ref, o_ref):
    hd = MEM_HEAD_DIM
    for hh in range(MEM_HEADS):
        cs = slice(hh * hd, (hh + 1) * hd)
        q = (q_ref[:, cs] * (hd ** -0.5)).astype(BF16)
        s = _dot_nt(q, kv_ref[:, cs])
        p = jnp.exp(s - jnp.max(s, axis=-1, keepdims=True))
        l = jnp.sum(p, axis=-1, keepdims=True)
        o = _dot(p.astype(BF16), kv_ref[:, D_BR + hh * hd:D_BR + (hh + 1) * hd]) / l
        o_ref[:, cs] = (o * _silu(g_ref[:, cs])).astype(o_ref.dtype)


def mem_branch(h2, kv, bsz, seq, tq=512):
    tq = min(tq, seq)
    nq = seq // tq
    mlen = kv.shape[0] // bsz
    col = lambda c: pl.BlockSpec((tq, D_BR), lambda b, i, c=c: (b * nq + i, c // D_BR))
    return pl.pallas_call(
        _mem_kernel,
        grid=(bsz, nq),
        in_specs=[col(C2_QM), col(C2_GM), pl.BlockSpec((mlen, 2 * D_BR), lambda b, i: (b, 0))],
        out_specs=pl.BlockSpec((tq, D_BR), lambda b, i: (b * nq + i, 0)),
        out_shape=jax.ShapeDtypeStruct((bsz * seq, D_BR), BF16),
        compiler_params=_params(),
        name="mem_attn",
    )(h2, h2, kv)


def _cmp_kernel(tk_ref, tv_ref, pek_ref, pev_ref, w1k_ref, w1v_ref, w2kt_ref, w2v_ref, kct_ref, vc_ref):
    n = tk_ref.shape[1]

    def hidden(t_ref, pe_ref, w1_ref):
        t = t_ref[0]
        lo = _dot((t + pe_ref[0:1, :]).astype(BF16), w1_ref[0])
        hi = _dot((t + pe_ref[1:2, :]).astype(BF16), w1_ref[1])
        pre = lo + pltpu.roll(hi, n - 1, axis=0)
        return _silu(pre).astype(BF16)

    kct_ref[0] = _dot_nt(w2kt_ref[...], hidden(tk_ref, pek_ref, w1k_ref)).astype(kct_ref.dtype)
    vc_ref[0] = _dot(hidden(tv_ref, pev_ref, w1v_ref), w2v_ref[...]).astype(vc_ref.dtype)


def nsa_compress(tk, tv, pe_k, pe_v, w1_k, w1_v, w2_k, w2_v):
    nb, n, width = tk.shape
    half = CMP_STRIDE * DK
    args = (pe_k.reshape(2, half), pe_v.reshape(2, half),
            w1_k.reshape(2, half, CMP_HIDDEN).astype(BF16), w1_v.reshape(2, half, CMP_HIDDEN).astype(BF16),
            w2_k.T.astype(BF16), w2_v.astype(BF16))
    full = lambda a: pl.BlockSpec(a.shape, lambda b: (0,) * a.ndim)
    blk = pl.BlockSpec((1, n, width), lambda b: (b, 0, 0))
    return pl.pallas_call(
        _cmp_kernel,
        grid=(nb,),
        in_specs=[blk, blk] + [full(a) for a in args],
        out_specs=[pl.BlockSpec((1, DK, n), lambda b: (b, 0, 0)), pl.BlockSpec((1, n, DK), lambda b: (b, 0, 0))],
        out_shape=[jax.ShapeDtypeStruct((nb, DK, n), BF16), jax.ShapeDtypeStruct((nb, n, DK), BF16)],
        compiler_params=_params(),
        name="nsa_compress",
    )(tk, tv, *args)


def _nsa_select_kernel(q_ref, gl_ref, gb_ref, kct_ref, vc_ref, bc_ref, ovt_ref, qsel_ref, ocg_ref, vt_s):
    qb = Q_BLOCK
    lane = lax.broadcasted_iota(jnp.int32, (qb, LANES), 1)
    jrow = lax.broadcasted_iota(jnp.int32, (MASK_ROWS, qb), 0).astype(F32)
    qlane = lax.broadcasted_iota(jnp.int32, (MASK_ROWS, qb), 1)
    sub = lax.broadcasted_iota(jnp.int32, (SUBLANES, qb), 0)
    nv = MASK_ROWS // SUBLANES
    ovt = ovt_ref[...]
    kct = kct_ref[0]
    vc = vc_ref[0]
    for sb in range(SEL_SUB):
        rows = slice(sb * qb, (sb + 1) * qb)
        blk = pl.program_id(2) * SEL_SUB + sb
        q = q_ref[rows, :] * (DK ** -0.5)
        gt = _sigmoid(gl_ref[rows, :] + gb_ref[...])
        qpad = []
        for g in range(GQA):
            t = q[:, (g // 2) * LANES:(g // 2 + 1) * LANES]
            if g % 2:
                t = pltpu.roll(t, DK, axis=1)
            qpad.append(jnp.where(lane < DK, t, 0.0))

        o_c = []
        psum = None
        for g in range(GQA):
            s = _dot(qpad[g][:, :DK].astype(BF16), kct) + bc_ref[0, sb, g * qb:(g + 1) * qb, :]
            m = jnp.max(s, axis=-1, keepdims=True)
            p = jnp.where(s > 0.1 * NEG, jnp.exp(s - m), 0.0)
            p = p / jnp.maximum(jnp.sum(p, axis=-1, keepdims=True), 1e-30)
            o_c.append(gt[:, g:g + 1] * _dot(p.astype(BF16), vc))
            psum = p if psum is None else psum + p
        ocg_ref[rows, :] = jnp.concatenate(o_c, axis=1)

        p_hi = psum.astype(BF16)
        rem = psum - p_hi.astype(F32)
        p_mid = rem.astype(BF16)
        p_lo = (rem - p_mid.astype(F32)).astype(BF16)
        imp_t = _dot_nt(ovt, p_hi) + _dot_nt(ovt, p_mid) + _dot_nt(ovt, p_lo)

        qblk = ((blk * qb + qlane) >> int(math.log2(SLC_BLOCK))).astype(F32)
        val = jnp.where(jrow == 0.0, 3e38,
                        jnp.where(jrow == qblk, 3e38, jnp.where(jrow == qblk - 1.0, 3e38, imp_t)))
        val = jnp.where(jrow > qblk, -1.0, val)
        vt_s[sb] = val
        vals = [val[k * SUBLANES:(k + 1) * SUBLANES] for k in range(nv)]
        cnt = [jnp.zeros((SUBLANES, qb), F32) for _ in range(nv)]
        for jp in range(MASK_ROWS):
            rowv = jnp.broadcast_to(vt_s[sb, jp:jp + 1, :], (SUBLANES, qb))
            for k in range(nv):
                if k * SUBLANES > jp:
                    beat = jnp.where(rowv >= vals[k], 1.0, 0.0)
                elif (k + 1) * SUBLANES - 1 < jp:
                    beat = jnp.where(rowv > vals[k], 1.0, 0.0)
                else:
                    beat = jnp.where(sub > jp - k * SUBLANES, jnp.where(rowv >= vals[k], 1.0, 0.0),
                                     jnp.where(rowv > vals[k], 1.0, 0.0))
                cnt[k] = cnt[k] + beat
        selneg_t = jnp.concatenate([jnp.where(c < float(SLC_TOPK), 0.0, NEG) for c in cnt], axis=0)
        for g in range(GQA):
            qt = jnp.concatenate([qpad[g].T[:DK], selneg_t], axis=0)
            qsel_ref[0, sb, :, g * qb:(g + 1) * qb] = qt.astype(qsel_ref.dtype)


def _nsa_attn_kernel(qsel_ref, ocg_ref, gc_ref, gl_ref, gb_ref, ks_ref, vs_ref, gs_ref, kw_ref, vw_ref, bw_ref,
                     o_ref, s_s, p_s):
    qb = Q_BLOCK
    hq = GQA * qb
    nq = ATT_NQ * hq
    i = pl.program_id(2) * ATT_NQ
    step = SEL_TK // qb
    n_tiles = (i * qb) // SEL_TK + 1
    qt = jnp.concatenate([qsel_ref[0, b] for b in range(ATT_NQ)], axis=1)

    wk = WINDOW + ATT_NQ * qb
    w0 = pl.multiple_of(i * qb, qb)
    rowi = lax.broadcasted_iota(jnp.int32, (LANES, nq), 0)
    q_win = jnp.where(rowi < DK, qt, jnp.where(rowi == DK, NEG, 0.0).astype(BF16))
    sw = _dot(kw_ref[0, 0, pl.ds(w0, wk), :], q_win) + bw_ref[0]
    pw = jnp.exp(sw - jnp.max(sw, axis=0, keepdims=True)).astype(BF16)
    acc_w = _dot(vw_ref[0, 0, :, pl.ds(w0, wk)], pw)

    def stage_qk(j):
        k0 = pl.multiple_of(j * SEL_TK, SEL_TK)
        sc = _dot(ks_ref[0, 0, pl.ds(k0, SEL_TK), :], qt)
        for b in range(ATT_NQ):
            dd = jnp.minimum(i + b - j * step, SEL_FAR)
            boff = pl.multiple_of((SEL_FAR - dd) * qb, qb)
            s_s[:, b * hq:(b + 1) * hq] = sc[:, b * hq:(b + 1) * hq] + gs_ref[0, pl.ds(boff, SEL_TK), :]

    def stage_softmax(m):
        m_new, alpha = [], []
        for g in range(ATT_NQ * GQA):
            cs = slice(g * qb, (g + 1) * qb)
            sc = s_s[:, cs]
            mg = jnp.maximum(m[:, cs], jnp.max(sc, axis=0, keepdims=True))
            alpha.append(jnp.exp(m[:, cs] - mg))
            p_s[:, cs] = jnp.exp(sc - mg).astype(BF16)
            m_new.append(mg)
        return jnp.concatenate(m_new, axis=1), jnp.concatenate(alpha, axis=1)

    def stage_pv(j, acc):
        k0 = pl.multiple_of(jnp.maximum(j, 0) * SEL_TK, SEL_TK)
        return acc + _dot(vs_ref[0, 0, :, pl.ds(k0, SEL_TK)], p_s[...])

    def advance(j, m, acc):
        acc = stage_pv(j - 2, acc)
        m, alpha = stage_softmax(m)
        return m, alpha * acc

    def body(j, carry):
        m, acc = advance(j, *carry)
        stage_qk(j)
        return m, acc

    p_s[...] = jnp.zeros(p_s.shape, BF16)
    stage_qk(jnp.int32(0))
    init = (jnp.full((1, nq), NEG, F32), jnp.zeros((LANES, nq), F32))
    m, acc = lax.fori_loop(1, n_tiles, body, init)
    _, acc = advance(n_tiles, m, acc)
    acc_s = stage_pv(n_tiles - 1, acc)

    for b in range(ATT_NQ):
        rows = slice(b * qb, (b + 1) * qb)
        gt = _sigmoid(gl_ref[rows, :] + gb_ref[...])
        outs = []
        for g in range(GQA):
            cs = slice(b * hq + g * qb, b * hq + (g + 1) * qb)
            a_w = acc_w[:, cs].T
            a_s = acc_s[:, cs].T
            o_w = a_w[:, :DK] / a_w[:, DK:DK + 1]
            o_s = a_s[:, :DK] / a_s[:, DK:DK + 1]
            outs.append(gt[:, GQA + g:GQA + g + 1] * o_s + gt[:, 2 * GQA + g:2 * GQA + g + 1] * o_w)
        o = ocg_ref[rows, :] + jnp.concatenate(outs, axis=1)
        o_ref[rows, :] = (o * _silu(gc_ref[rows, :])).astype(o_ref.dtype)


def _rel_bucket(dist):
    n = jnp.maximum(dist, 0)
    exact = REL_BUCKETS // 2
    nf = jnp.maximum(n, 1).astype(jnp.float32)
    large = exact + (jnp.log(nf / exact) / math.log(REL_MAX_DIST / exact)
                     * (REL_BUCKETS - exact)).astype(jnp.int32)
    return jnp.where(n < exact, n, jnp.minimum(large, REL_BUCKETS - 1))


def nsa_bias_tables(rel_bias, seq):
    qb = Q_BLOCK
    n_pad = seq // CMP_STRIDE
    c0 = n_pad - NA
    wc = 2 * n_pad
    ws = SEL_FAR * qb + SEL_TK
    nmax = max(seq, SEL_FAR * qb + qb)
    padl = CMP_STRIDE * (wc + 2)
    bvec = rel_bias[_rel_bucket(jnp.arange(nmax))].T.astype(F32)
    vext = jnp.concatenate([jnp.full((NSA_HEADS, padl), NEG, F32), bvec], axis=1)
    pos = jnp.arange(padl + nmax) - padl
    vwin = jnp.where(pos < WINDOW, vext, NEG)

    mlo, mhi = c0 - wc + 1, NA - 1 + c0
    start = padl + CMP_STRIDE * mlo - (CMP_BLOCK - 1)
    assert start >= 0
    u = vext[:, start:start + CMP_STRIDE * (mhi - mlo + 1)].reshape(NSA_HEADS, mhi - mlo + 1, CMP_STRIDE)
    urev = u[:, ::-1, :]
    gcb = jnp.stack([urev[:, NA - 1 - a:NA - 1 - a + wc, :] for a in range(NA)], axis=1)
    gcb = gcb.transpose(0, 1, 3, 2).reshape(HKV, GQA * qb, wc)
    nqb = seq // qb
    bc = jnp.stack([gcb[:, :, c0 - NA * i:c0 - NA * i + n_pad] for i in range(nqb)], axis=1)

    def toeplitz(vec, d0, width):
        vrev = vec[:, ::-1]
        s0 = nmax - d0 - qb
        period = width + qb - 1
        assert s0 >= 0 and s0 + period <= padl + nmax
        sl = vrev[:, s0:s0 + period]
        y = jnp.concatenate([sl[:, qb - 1:], sl[:, :qb - 1]], axis=1)
        z = jnp.tile(y, (1, qb))[:, :qb * (period - 1)].reshape(NSA_HEADS, qb, period - 1)
        return z[:, :, :width].reshape(HKV, GQA * qb, width)

    gs = toeplitz(vext, SEL_FAR * qb, ws)
    bw = tuple(toeplitz(vwin, WINDOW + b * qb, WINDOW + ATT_NQ * qb) for b in range(ATT_NQ))
    return bc, gs, bw


def _overlap_matrix_t(n_pad):
    c_start = np.arange(n_pad)[None, :] * CMP_STRIDE
    s_start = np.arange(MASK_ROWS)[:, None] * SLC_BLOCK
    ov = np.clip(np.minimum(c_start + CMP_BLOCK, s_start + SLC_BLOCK) - np.maximum(c_start, s_start), 0, None)
    return ov.astype(np.float32) / CMP_BLOCK


def nsa_branch(h1, h2, hkv, hgl, gate_b4, kct, vc, tables, bsz, seq):
    qb = Q_BLOCK
    nqb = seq // qb
    n_pad = seq // CMP_STRIDE
    n_slc = seq // SLC_BLOCK
    assert n_slc <= MASK_ROWS and seq % SEL_TK == 0 and nqb % SEL_SUB == 0
    assert (SEL_TK // qb) % ATT_NQ == 0 and nqb % ATT_NQ == 0
    bc, gs, bw = tables
    m = bsz * seq
    gw = GQA * DK

    sq = SEL_SUB * qb
    nsq = seq // sq
    ovt = jnp.asarray(_overlap_matrix_t(n_pad), BF16)
    qsel, ocg = pl.pallas_call(
        _nsa_select_kernel,
        grid=(HKV, bsz, nsq),
        in_specs=[
            pl.BlockSpec((sq, gw), lambda hh, b, i: (b * nsq + i, C1_QC // gw + hh)),
            pl.BlockSpec((sq, LANES), lambda hh, b, i: (b * nsq + i, hh)),
            pl.BlockSpec((1, LANES), lambda hh, b, i: (0, hh)),
            pl.BlockSpec((1, DK, n_pad), lambda hh, b, i: (b * HKV + hh, 0, 0)),
            pl.BlockSpec((1, n_pad, DK), lambda hh, b, i: (b * HKV + hh, 0, 0)),
            pl.BlockSpec((1, SEL_SUB) + bc.shape[2:], lambda hh, b, i: (hh, i, 0, 0)),
            pl.BlockSpec(ovt.shape, lambda hh, b, i: (0, 0)),
        ],
        out_specs=[pl.BlockSpec((1, SEL_SUB, LANES, GQA * qb), lambda hh, b, i: (hh, b * nsq + i, 0, 0)),
                   pl.BlockSpec((sq, gw), lambda hh, b, i: (b * nsq + i, hh))],
        out_shape=[jax.ShapeDtypeStruct((HKV, bsz * nqb, LANES, GQA * qb), BF16),
                   jax.ShapeDtypeStruct((m, D_BR), F32)],
        scratch_shapes=[pltpu.VMEM((SEL_SUB, MASK_ROWS, qb), F32)],
        compiler_params=_params(),
        name="nsa_select",
    )(h1, hgl, gate_b4, kct, vc, bc, ovt)

    def key_rows(c, mask_cols, pad):
        k = hkv[:, c:c + KV_W].reshape(bsz, seq, HKV, DK).transpose(0, 2, 1, 3)
        k = jnp.pad(k, ((0, 0), (0, 0), (pad, 0), (0, 0)))
        mc = jnp.broadcast_to(jnp.asarray(mask_cols, BF16), k.shape[:3] + (MASK_ROWS,))
        return jnp.concatenate([k, mc], axis=3)

    def values_t(c, pad):
        v = hkv[:, c:c + KV_W].reshape(bsz, seq, HKV, DK).transpose(0, 2, 3, 1)
        v = jnp.pad(v, ((0, 0), (0, 0), (0, 0), (pad, 0)))
        tail = np.zeros((LANES - DK, 1), np.float32)
        tail[0] = 1.0
        ones = jnp.broadcast_to(jnp.asarray(tail, BF16), v.shape[:2] + (LANES - DK, v.shape[3]))
        return jnp.concatenate([v, ones], axis=2)

    sel_cols = (np.arange(seq)[:, None] // SLC_BLOCK == np.arange(MASK_ROWS)[None, :]).astype(np.float32)
    invalid = np.zeros((seq + WINDOW, MASK_ROWS), np.float32)
    invalid[:WINDOW, 0] = 1.0
    ks_ext = key_rows(0, sel_cols, 0)
    vs_ext = values_t(KV_W, 0)
    kw_ext = key_rows(2 * KV_W, invalid, WINDOW)
    vw_ext = values_t(3 * KV_W, WINDOW)
    gs_t = gs.transpose(0, 2, 1)
    bw_t = jnp.concatenate([w.transpose(0, 2, 1) for w in bw], axis=2)

    aq = ATT_NQ * qb
    npair = nqb // ATT_NQ
    rowblk = lambda width, c0: pl.BlockSpec((aq, width), lambda hh, b, i: (b * npair + i, c0 // width + hh))
    per_bh = lambda a: pl.BlockSpec((1, 1) + a.shape[2:], lambda hh, b, i: (b, hh, 0, 0))
    per_h = lambda a: pl.BlockSpec((1,) + a.shape[1:], lambda hh, b, i: (hh, 0, 0))
    return pl.pallas_call(
        _nsa_attn_kernel,
        grid=(HKV, bsz, npair),
        in_specs=[
            pl.BlockSpec((1, ATT_NQ, LANES, GQA * qb), lambda hh, b, i: (hh, b * npair + i, 0, 0)),
            rowblk(gw, 0), rowblk(gw, C2_GC),
            pl.BlockSpec((aq, LANES), lambda hh, b, i: (b * npair + i, hh)),
            pl.BlockSpec((1, LANES), lambda hh, b, i: (0, hh)),
            per_bh(ks_ext), per_bh(vs_ext), per_h(gs_t),
            per_bh(kw_ext), per_bh(vw_ext), per_h(bw_t),
        ],
        out_specs=pl.BlockSpec((aq, gw), lambda hh, b, i: (b * npair + i, hh)),
        out_shape=jax.ShapeDtypeStruct((m, D_BR), BF16),
        scratch_shapes=[pltpu.VMEM((SEL_TK, ATT_NQ * GQA * qb), F32), pltpu.VMEM((SEL_TK, ATT_NQ * GQA * qb), BF16)],
        compiler_params=_params(),
        name="nsa_attn",
    )(qsel, ocg, h2, hgl, gate_b4, ks_ext, vs_ext, gs_t, kw_ext, vw_ext, bw_t)


def _final_kernel(oa_ref, ob_ref, oc_ref, om_ref, *rest):
    gate_refs = rest[:2 * N_BRANCHES]
    x_ref, wb_ref, wo_ref, lng_ref, lnb_ref, y_ref, yb_ref = rest[2 * N_BRANCHES:]
    o_refs = (oa_ref, ob_ref, oc_ref, om_ref)
    halves = []
    for c in range(2):
        cs = slice(c * D_BR, (c + 1) * D_BR)
        acc = None
        for k in range(N_BRANCHES):
            term = _sigmoid(gate_refs[2 * k + c][...]) * _dot(o_refs[k][...], wb_ref[k, :, cs])
            acc = term if acc is None else acc + term
        halves.append(acc.astype(BF16))
    merged = jnp.concatenate(halves, axis=1)
    z = DN_ALPHA * x_ref[...] + _dot(merged, wo_ref[...])
    mu = jnp.mean(z, axis=-1, keepdims=True)
    zc = z - mu
    var = jnp.mean(zc * zc, axis=-1, keepdims=True)
    y = zc * lax.rsqrt(var + LN_EPS) * lng_ref[...] + lnb_ref[...]
    y_ref[...] = y
    yb_ref[...] = y.astype(BF16)


def final_merge(o_a, o_b, o_c, o_m, h2, x, w_branch, w_out, ln_g, ln_b, tm=128):
    m = x.shape[0]
    tm = min(tm, m)
    br = pl.BlockSpec((tm, D_BR), lambda i: (i, 0))
    gate = lambda k: pl.BlockSpec((tm, D_BR), lambda i, k=k: (i, C2_MERGE // D_BR + k))
    xs = pl.BlockSpec((tm, D_MODEL), lambda i: (i, 0))
    resident = lambda a: pl.BlockSpec(a.shape, lambda i: (0,) * a.ndim, pipeline_mode=pl.Buffered(1))
    wb = w_branch.astype(BF16)
    wo = w_out.astype(BF16)
    lg, lb = ln_g.reshape(1, D_MODEL), ln_b.reshape(1, D_MODEL)
    ngate = 2 * N_BRANCHES
    return pl.pallas_call(
        _final_kernel,
        grid=(m // tm,),
        in_specs=[br, br, br, br] + [gate(k) for k in range(ngate)] + [xs]
                 + [resident(wb), resident(wo), resident(lg), resident(lb)],
        out_specs=[xs, xs],
        out_shape=[jax.ShapeDtypeStruct((m, D_MODEL), F32), jax.ShapeDtypeStruct((m, D_MODEL), BF16)],
        compiler_params=_params(),
        name="merge_out_ln",
    )(o_a, o_b, o_c, o_m, *([h2] * ngate), x, wb, wo, lg, lb)


def _gate_col_map():
    src = np.full((HKV * LANES,), -1, np.int64)
    for hh in range(HKV):
        for brn in range(3):
            for g in range(GQA):
                src[hh * LANES + brn * GQA + g] = brn * NSA_HEADS + hh * GQA + g
    return src


def _spread_gate_cols(a):
    src = _gate_col_map()
    return jnp.where(jnp.asarray(src >= 0), a[..., np.maximum(src, 0)], 0)


def layer(l, x, xb, mem_b, tables, bsz, seq, w_in, w_tail, sgu_ln_g, sgu_ln_b, sgu_w, sgu_b, conv_w, conv_b,
          lru_wa, lru_ba, lru_wx, lru_bx, lru_lambda, cmp_pe_k, cmp_pe_v, cmp_w1_k, cmp_w1_v,
          cmp_w2_k, cmp_w2_v, nsa_gate_b, w_mem_kv, w_branch, w_out, ln_g, ln_b):
    m = bsz * seq
    tm = min(1024, m)
    h1 = matmul_f32w(xb, w_in, l, 0, N1, F32, tm, 512, w_is_nk=True)
    hkv = matmul_f32w(xb, w_in, l, N1, NKV, BF16, tm, 512, w_is_nk=True)
    h2 = matmul_f32w(xb, w_tail, l, 0, N2, F32, tm, 512, w_is_nk=True)
    w_gl = _spread_gate_cols(w_in[l, GL_OFF:GL_OFF + 3 * NSA_HEADS, :].T).astype(BF16)
    hgl = matmul(xb, w_gl, F32, tm, HKV * LANES)
    gate_b4 = _spread_gate_cols(nsa_gate_b[l]).reshape(1, HKV * LANES)

    o_a = gmlp_branch(h1, sgu_ln_g[l], sgu_ln_b[l], sgu_w[l], sgu_b[l])
    o_b = lru_branch(h1, bsz, seq, conv_w[l], conv_b[l], lru_wa[l], lru_ba[l], lru_wx[l], lru_bx[l],
                     lru_lambda[l])

    def stride_blocks(c):
        t = h1[:, c:c + KV_W].reshape(bsz, seq // CMP_STRIDE, CMP_STRIDE, HKV, DK)
        return t.transpose(0, 3, 1, 2, 4).reshape(bsz * HKV, seq // CMP_STRIDE, CMP_STRIDE * DK)

    kct, vc = nsa_compress(stride_blocks(C1_KV), stride_blocks(C1_KV + KV_W), cmp_pe_k[l], cmp_pe_v[l],
                           cmp_w1_k[l], cmp_w1_v[l], cmp_w2_k[l], cmp_w2_v[l])
    o_c = nsa_branch(h1, h2, hkv, hgl, gate_b4, kct, vc, tables, bsz, seq)

    mrows = mem_b.shape[0]
    kv = matmul_f32w(mem_b, w_mem_kv, l, 0, 2 * D_BR, BF16, min(512, mrows), 512)
    o_m = mem_branch(h2, kv, bsz, seq)
    return final_merge(o_a, o_b, o_c, o_m, h2, x, w_branch[l], w_out[l], ln_g[l], ln_b[l])


def kernel(x, mem, rel_bias, w_in, sgu_ln_g, sgu_ln_b, sgu_w, sgu_b, conv_w, conv_b, lru_wa, lru_ba, lru_wx,
           lru_bx, lru_lambda, cmp_pe_k, cmp_pe_v, cmp_w1_k, cmp_w1_v, cmp_w2_k, cmp_w2_v, nsa_gate_b,
           w_mem_kv, w_branch, w_out, ln_g, ln_b):
    bsz, seq, _ = x.shape
    tables = nsa_bias_tables(rel_bias, seq)
    xf = x.reshape(bsz * seq, D_MODEL)
    xb = xf.astype(BF16)
    mem_b = mem.reshape(-1, D_MODEL).astype(BF16)
    w_in_t = jnp.swapaxes(w_in, 1, 2)
    w_tail = w_in_t[:, GL_OFF + 3 * NSA_HEADS:, :]
    params = (w_in_t, w_tail, sgu_ln_g, sgu_ln_b, sgu_w, sgu_b, conv_w, conv_b, lru_wa, lru_ba, lru_wx, lru_bx,
              lru_lambda, cmp_pe_k, cmp_pe_v, cmp_w1_k, cmp_w1_v, cmp_w2_k, cmp_w2_v, nsa_gate_b,
              w_mem_kv, w_branch, w_out, ln_g, ln_b)
    for l in range(w_in.shape[0]):
        xf, xb = layer(l, xf, xb, mem_b, tables, bsz, seq, *params)
    return xf.reshape(bsz, seq, D_MODEL)
```

```python
import functools
import math

import numpy as np
import jax
import jax.numpy as jnp
from jax import lax
from jax.experimental import pallas as pl
from jax.experimental.pallas import tpu as pltpu

F32 = jnp.float32
BF16 = jnp.bfloat16

D_MODEL = 2048
DEPTH = 2
D_BR = D_MODEL // 2
N_BRANCHES = 4
GMLP_CHUNK = 128
GMLP_GROUPS = 8
LRU_BLOCKS = 8
LRU_BLOCK_DIM = D_BR // LRU_BLOCKS
CONV_WIDTH = 4
LRU_C = 8.0
DK = 64
NSA_HEADS = D_BR // DK
HKV = NSA_HEADS // 4
GQA = NSA_HEADS // HKV
KV_W = HKV * DK
CMP_BLOCK = 32
CMP_STRIDE = 16
CMP_HIDDEN = 256
SLC_BLOCK = 64
SLC_TOPK = 8
WINDOW = 256
Q_BLOCK = 128
MEM_HEADS = 4
MEM_HEAD_DIM = D_BR // MEM_HEADS
REL_BUCKETS = 32
REL_MAX_DIST = 1024
DN_ALPHA = (2 * DEPTH) ** 0.25
LN_EPS = 1e-5

LANES = 128
SUBLANES = 8
NEG = -1e30
SEL_TK = 512
SEL_FAR = 11
MASK_ROWS = LANES - DK
NA = Q_BLOCK // CMP_STRIDE
SEL_SUB = 4
ATT_NQ = 2
VMEM_LIMIT = 56 * 1024 * 1024

C1_U, C1_V, C1_GA, C1_XB, C1_GB, C1_QC = (D_BR * k for k in range(6))
C1_KV = 6 * D_BR
N1 = C1_KV + 2 * KV_W
NKV = 4 * KV_W
C2_GC, C2_QM, C2_GM, C2_MERGE = (D_BR * k for k in range(4))
N2 = C2_MERGE + N_BRANCHES * D_MODEL
GL_OFF = N1 + NKV


def _sigmoid(x):
    return 1.0 / (1.0 + jnp.exp(-x))


def _silu(x):
    return x * _sigmoid(x)


def _gelu_tanh(x):
    return 0.5 * x * (1.0 + jnp.tanh(math.sqrt(2.0 / math.pi) * (x + 0.044715 * (x * x * x))))


def _dot(a, b):
    return jnp.dot(a, b, preferred_element_type=F32)


def _dot_nt(a, b):
    return lax.dot_general(a, b, (((1,), (1,)), ((), ())), preferred_element_type=F32)


def _params(**kw):
    return pltpu.CompilerParams(vmem_limit_bytes=VMEM_LIMIT, **kw)


def _mm_kernel(x_ref, w_ref, o_ref):
    o_ref[...] = _dot(x_ref[...], w_ref[...]).astype(o_ref.dtype)


def matmul(x, w, out_dtype, tm, tn):
    m, k = x.shape
    n = w.shape[1]
    assert m % tm == 0 and n % tn == 0, (m, n, tm, tn)
    return pl.pallas_call(
        _mm_kernel,
        grid=(n // tn, m // tm),
        in_specs=[pl.BlockSpec((tm, k), lambda j, i: (i, 0)),
                  pl.BlockSpec((k, tn), lambda j, i: (0, j))],
        out_specs=pl.BlockSpec((tm, tn), lambda j, i: (i, j)),
        out_shape=jax.ShapeDtypeStruct((m, n), out_dtype),
        compiler_params=_params(),
        name="matmul",
    )(x, w)


def _mm_castw_kernel(x_ref, *rest, w_is_nk, shift):
    w_refs, (o_ref, wb_ref) = rest[:-2], rest[-2:]

    @pl.when(pl.program_id(1) == 0)
    def _():
        if shift:
            w = jnp.concatenate([w_refs[0][0, shift:, :], w_refs[1][0, :shift, :]], axis=0)
        else:
            w = w_refs[0][0]
        wb_ref[...] = (w.T if w_is_nk else w).astype(BF16)

    o_ref[...] = _dot(x_ref[...], wb_ref[...]).astype(o_ref.dtype)


def matmul_f32w(x, w3, layer_idx, col0, n, out_dtype, tm, tn, w_is_nk=False):
    m, k = x.shape
    ncols = w3.shape[1] if w_is_nk else w3.shape[2]
    shift = col0 % tn
    assert m % tm == 0 and n % tn == 0 and col0 + n <= ncols, (m, n, col0, tm, tn)
    assert shift == 0 or (w_is_nk and shift % SUBLANES == 0), (col0, tn)
    j0 = col0 // tn
    if w_is_nk:
        w_specs = [pl.BlockSpec((1, tn, k), lambda j, i, d=d: (layer_idx, j0 + j + d, 0))
                   for d in range(2 if shift else 1)]
    else:
        w_specs = [pl.BlockSpec((1, k, tn), lambda j, i: (layer_idx, 0, j0 + j))]
    return pl.pallas_call(
        functools.partial(_mm_castw_kernel, w_is_nk=w_is_nk, shift=shift),
        grid=(n // tn, m // tm),
        in_specs=[pl.BlockSpec((tm, k), lambda j, i: (i, 0))] + w_specs,
        out_specs=pl.BlockSpec((tm, tn), lambda j, i: (i, j)),
        out_shape=jax.ShapeDtypeStruct((m, n), out_dtype),
        scratch_shapes=[pltpu.VMEM((k, tn), BF16)],
        compiler_params=_params(dimension_semantics=("arbitrary", "arbitrary")),
        name="matmul_f32w",
    )(x, *([w3] * len(w_specs)))


def _kv_proj_kernel(x_ref, ws_ref, ww_ref, ks_ref, kw_ref, vs_ref, vw_ref, wk_s, wv_s, *, tm, seq):
    @pl.when(pl.program_id(0) == 0)
    def _():
        wk = jnp.concatenate([ws_ref[0, 0:KV_W], ww_ref[0, 0:KV_W]], axis=0)
        wk_s[...] = wk.T.astype(BF16)
        wv_s[...] = jnp.concatenate([ws_ref[0, KV_W:2 * KV_W], ww_ref[0, KV_W:2 * KV_W]], axis=0).astype(BF16)

    x = x_ref[...]
    kk = _dot(x, wk_s[...])
    vt = _dot_nt(wv_s[...], x)
    pos = (pl.program_id(0) * tm + lax.broadcasted_iota(jnp.int32, (tm, MASK_ROWS), 0)) % seq
    blk = lax.broadcasted_iota(jnp.int32, (tm, MASK_ROWS), 1)
    onehot = jnp.where((pos >> int(math.log2(SLC_BLOCK))) == blk, 1.0, 0.0)
    zeros = jnp.zeros((tm, MASK_ROWS), F32)
    tail = jnp.where(lax.broadcasted_iota(jnp.int32, (LANES - DK, tm), 0) == 0, 1.0, 0.0)
    for h in range(HKV):
        ks_ref[:, h * LANES:(h + 1) * LANES] = jnp.concatenate(
            [kk[:, h * DK:(h + 1) * DK], onehot], axis=1).astype(ks_ref.dtype)
        kw_ref[:, h * LANES:(h + 1) * LANES] = jnp.concatenate(
            [kk[:, KV_W + h * DK:KV_W + (h + 1) * DK], zeros], axis=1).astype(kw_ref.dtype)
        vs_ref[h] = jnp.concatenate([vt[h * DK:(h + 1) * DK], tail], axis=0).astype(vs_ref.dtype)
        vw_ref[h] = jnp.concatenate([vt[KV_W + h * DK:KV_W + (h + 1) * DK], tail], axis=0).astype(vw_ref.dtype)


def kv_projection(x, wt3, layer_idx, row0, seq, tm):
    m, k = x.shape
    nrows = 2 * KV_W
    assert m % tm == 0 and row0 % nrows == 0 and seq % tm == 0
    wspec = lambda d: pl.BlockSpec((1, nrows, k), lambda i: (layer_idx, row0 // nrows + d, 0),
                                   pipeline_mode=pl.Buffered(1))
    keys = jax.ShapeDtypeStruct((m, HKV * LANES), BF16)
    vals = jax.ShapeDtypeStruct((HKV, LANES, m), BF16)
    return pl.pallas_call(
        functools.partial(_kv_proj_kernel, tm=tm, seq=seq),
        grid=(m // tm,),
        in_specs=[pl.BlockSpec((tm, k), lambda i: (i, 0)), wspec(0), wspec(1)],
        out_specs=[pl.BlockSpec((tm, HKV * LANES), lambda i: (i, 0))] * 2
                  + [pl.BlockSpec((HKV, LANES, tm), lambda i: (0, 0, i))] * 2,
        out_shape=[keys, keys, vals, vals],
        scratch_shapes=[pltpu.VMEM((k, 2 * KV_W), BF16), pltpu.VMEM((2 * KV_W, k), BF16)],
        compiler_params=_params(dimension_semantics=("arbitrary",)),
        name="kv_projection",
    )(x, wt3, wt3)


def _gmlp_kernel(u_ref, v_ref, ga_ref, lng_ref, lnb_ref, w_ref, bs_ref, o_ref, *, rows):
    gd = D_BR // GMLP_GROUPS
    for c in range(rows // GMLP_CHUNK):
        r = slice(c * GMLP_CHUNK, (c + 1) * GMLP_CHUNK)
        v = _gelu_tanh(v_ref[r, :])
        mu = jnp.mean(v, axis=-1, keepdims=True)
        vc = v - mu
        var = jnp.mean(vc * vc, axis=-1, keepdims=True)
        vb = (vc * lax.rsqrt(var + LN_EPS) * lng_ref[...] + lnb_ref[...]).astype(BF16)
        u = _gelu_tanh(u_ref[r, :]) * _silu(ga_ref[r, :])
        for g in range(GMLP_GROUPS):
            cs = slice(g * gd, (g + 1) * gd)
            mixed = _dot(w_ref[g], vb[:, cs]) + bs_ref[:, g:g + 1]
            o_ref[r, cs] = (u[:, cs] * mixed).astype(o_ref.dtype)


def gmlp_branch(h1, ln_g, ln_b, w_s, b_s, rows=512):
    m = h1.shape[0]
    rows = min(rows, m)
    causal = jnp.tril(jnp.ones((GMLP_CHUNK, GMLP_CHUNK), dtype=bool))
    w = jnp.where(causal, w_s, 0).astype(BF16)
    col = lambda c: pl.BlockSpec((rows, D_BR), lambda i, c=c: (i, c // D_BR))
    full = lambda a: pl.BlockSpec(a.shape, lambda i: (0,) * a.ndim)
    args = (ln_g.reshape(1, D_BR), ln_b.reshape(1, D_BR), w, b_s.T)
    return pl.pallas_call(
        functools.partial(_gmlp_kernel, rows=rows),
        grid=(m // rows,),
        in_specs=[col(C1_U), col(C1_V), col(C1_GA)] + [full(a) for a in args],
        out_specs=pl.BlockSpec((rows, D_BR), lambda i: (i, 0)),
        out_shape=jax.ShapeDtypeStruct((m, D_BR), BF16),
        compiler_params=_params(),
        name="gmlp",
    )(h1, h1, h1, *args)


def _lru_kernel(xb_ref, gb_ref, cw_ref, cb_ref, wa_ref, ba_ref, wx_ref, bx_ref, lam_ref, o_ref,
                xbuf, hcarry, a_s, g_s, *, ts):
    @pl.when(pl.program_id(1) == 0)
    def _():
        xbuf[0:8, :] = jnp.zeros((8, D_BR), F32)
        hcarry[...] = jnp.zeros((8, D_BR), F32)

    xbuf[8:8 + ts, :] = xb_ref[...]
    xc = cb_ref[...] + cw_ref[0:1, :] * xbuf[pl.ds(8 - (CONV_WIDTH - 1), ts), :]
    for k in range(1, CONV_WIDTH):
        xc = xc + cw_ref[k:k + 1, :] * xbuf[pl.ds(8 - (CONV_WIDTH - 1) + k, ts), :]
    xbuf[0:8, :] = xbuf[ts:ts + 8, :]

    xcb = xc.astype(BF16)
    bd = LRU_BLOCK_DIM
    r = jnp.concatenate([_dot(xcb[:, n * bd:(n + 1) * bd], wa_ref[n]) for n in range(LRU_BLOCKS)], axis=1)
    i = jnp.concatenate([_dot(xcb[:, n * bd:(n + 1) * bd], wx_ref[n]) for n in range(LRU_BLOCKS)], axis=1)
    r = _sigmoid(r + ba_ref[...])
    i = _sigmoid(i + bx_ref[...])
    nl = -lam_ref[...]
    softplus = jnp.maximum(nl, 0.0) + jnp.log1p(jnp.exp(-jnp.abs(nl)))
    log_a = (-LRU_C * softplus) * r
    a = jnp.exp(log_a)
    a_s[...] = a
    g_s[...] = jnp.sqrt(1.0 - a * a) * i * xc

    row = lax.broadcasted_iota(jnp.int32, (8, D_BR), 0)

    def body(j, carry):
        r0 = pl.multiple_of(j * 8, 8)
        av = a_s[pl.ds(r0, 8), :]
        bv = g_s[pl.ds(r0, 8), :]
        for d in (1, 2, 4):
            keep = row >= d
            a_sh = pltpu.roll(av, d, axis=0)
            b_sh = pltpu.roll(bv, d, axis=0)
            bv = jnp.where(keep, av * b_sh + bv, bv)
            av = jnp.where(keep, av * a_sh, av)
        hv = av * carry + bv
        g_s[pl.ds(r0, 8), :] = hv
        return jnp.broadcast_to(hv[7:8, :], (8, D_BR))

    hcarry[...] = lax.fori_loop(0, ts // 8, body, hcarry[...])
    o_ref[...] = (g_s[...] * _silu(gb_ref[...])).astype(o_ref.dtype)


def lru_branch(h1, bsz, seq, conv_w, conv_b, wa, ba, wx, bx, lam, ts=512):
    ts = min(ts, seq)
    ns = seq // ts
    col = lambda c: pl.BlockSpec((ts, D_BR), lambda b, s, c=c: (b * ns + s, c // D_BR))
    full = lambda a: pl.BlockSpec(a.shape, lambda b, s: (0,) * a.ndim)
    row = lambda a: a.reshape(1, D_BR)
    args = (conv_w, row(conv_b), wa.astype(BF16), row(ba), wx.astype(BF16), row(bx), row(lam))
    return pl.pallas_call(
        functools.partial(_lru_kernel, ts=ts),
        grid=(bsz, ns),
        in_specs=[col(C1_XB), col(C1_GB)] + [full(a) for a in args],
        out_specs=pl.BlockSpec((ts, D_BR), lambda b, s: (b * ns + s, 0)),
        out_shape=jax.ShapeDtypeStruct((bsz * seq, D_BR), BF16),
        scratch_shapes=[pltpu.VMEM((ts + 8, D_BR), F32), pltpu.VMEM((8, D_BR), F32),
                        pltpu.VMEM((ts, D_BR), F32), pltpu.VMEM((ts, D_BR), F32)],
        compiler_params=_params(dimension_semantics=("arbitrary", "arbitrary")),
        name="lru",
    )(h1, h1, *args)


def _mem_kernel(q_ref, g_ref, kv_ref, o_ref):
    hd = MEM_HEAD_DIM
    for hh in range(MEM_HEADS):
        cs = slice(hh * hd, (hh + 1) * hd)
        q = (q_ref[:, cs] * (hd ** -0.5)).astype(BF16)
        s = _dot_nt(q, kv_ref[:, cs])
        p = jnp.exp(s - jnp.max(s, axis=-1, keepdims=True))
        l = jnp.sum(p, axis=-1, keepdims=True)
        o = _dot(p.astype(BF16), kv_ref[:, D_BR + hh * hd:D_BR + (hh + 1) * hd]) / l
        o_ref[:, cs] = (o * _silu(g_ref[:, cs])).astype(o_ref.dtype)


def mem_branch(h2, kv, bsz, seq, tq=512):
    tq = min(tq, seq)
    nq = seq // tq
    mlen = kv.shape[0] // bsz
    col = lambda c: pl.BlockSpec((tq, D_BR), lambda b, i, c=c: (b * nq + i, c // D_BR))
    return pl.pallas_call(
        _mem_kernel,
        grid=(bsz, nq),
        in_specs=[col(C2_QM), col(C2_GM), pl.BlockSpec((mlen, 2 * D_BR), lambda b, i: (b, 0))],
        out_specs=pl.BlockSpec((tq, D_BR), lambda b, i: (b * nq + i, 0)),
        out_shape=jax.ShapeDtypeStruct((bsz * seq, D_BR), BF16),
        compiler_params=_params(),
        name="mem_attn",
    )(h2, h2, kv)


def _cmp_kernel(tk_ref, tv_ref, pek_ref, pev_ref, w1k_ref, w1v_ref, w2kt_ref, w2v_ref, kct_ref, vc_ref):
    n = tk_ref.shape[1]

    def hidden(t_ref, pe_ref, w1_ref):
        t = t_ref[0]
        lo = _dot((t + pe_ref[0:1, :]).astype(BF16), w1_ref[0])
        hi = _dot((t + pe_ref[1:2, :]).astype(BF16), w1_ref[1])
        pre = lo + pltpu.roll(hi, n - 1, axis=0)
        return _silu(pre).astype(BF16)

    kct_ref[0] = _dot_nt(w2kt_ref[...], hidden(tk_ref, pek_ref, w1k_ref)).astype(kct_ref.dtype)
    vc_ref[0] = _dot(hidden(tv_ref, pev_ref, w1v_ref), w2v_ref[...]).astype(vc_ref.dtype)


def nsa_compress(tk, tv, pe_k, pe_v, w1_k, w1_v, w2_k, w2_v):
    nb, n, width = tk.shape
    half = CMP_STRIDE * DK
    args = (pe_k.reshape(2, half), pe_v.reshape(2, half),
            w1_k.reshape(2, half, CMP_HIDDEN).astype(BF16), w1_v.reshape(2, half, CMP_HIDDEN).astype(BF16),
            w2_k.T.astype(BF16), w2_v.astype(BF16))
    full = lambda a: pl.BlockSpec(a.shape, lambda b: (0,) * a.ndim)
    blk = pl.BlockSpec((1, n, width), lambda b: (b, 0, 0))
    return pl.pallas_call(
        _cmp_kernel,
        grid=(nb,),
        in_specs=[blk, blk] + [full(a) for a in args],
        out_specs=[pl.BlockSpec((1, DK, n), lambda b: (b, 0, 0)), pl.BlockSpec((1, n, DK), lambda b: (b, 0, 0))],
        out_shape=[jax.ShapeDtypeStruct((nb, DK, n), BF16), jax.ShapeDtypeStruct((nb, n, DK), BF16)],
        compiler_params=_params(),
        name="nsa_compress",
    )(tk, tv, *args)


def _nsa_select_kernel(q_ref, gl_ref, gb_ref, kct_ref, vc_ref, bc_ref, ovt_ref, qsel_ref, ocg_ref, vt_s):
    qb = Q_BLOCK
    lane = lax.broadcasted_iota(jnp.int32, (qb, LANES), 1)
    jrow = lax.broadcasted_iota(jnp.int32, (MASK_ROWS, qb), 0).astype(F32)
    qlane = lax.broadcasted_iota(jnp.int32, (MASK_ROWS, qb), 1)
    sub = lax.broadcasted_iota(jnp.int32, (SUBLANES, qb), 0)
    nv = MASK_ROWS // SUBLANES
    ovt = ovt_ref[...]
    kct = kct_ref[0]
    vc = vc_ref[0]
    for sb in range(SEL_SUB):
        rows = slice(sb * qb, (sb + 1) * qb)
        blk = pl.program_id(2) * SEL_SUB + sb
        q = q_ref[rows, :] * (DK ** -0.5)
        gt = _sigmoid(gl_ref[rows, :] + gb_ref[...])
        qpad = []
        for g in range(GQA):
            t = q[:, (g // 2) * LANES:(g // 2 + 1) * LANES]
            if g % 2:
                t = pltpu.roll(t, DK, axis=1)
            qpad.append(jnp.where(lane < DK, t, 0.0))

        has_keys = blk * qb + lax.broadcasted_iota(jnp.int32, (qb, 1), 0) >= CMP_BLOCK - 1
        o_c = []
        psum = None
        for g in range(GQA):
            s = _dot(qpad[g][:, :DK].astype(BF16), kct) + bc_ref[0, sb, g * qb:(g + 1) * qb, :]
            m = jnp.max(s, axis=-1, keepdims=True)
            p = jnp.exp(s - m)
            p = p * jnp.where(has_keys, 1.0 / jnp.maximum(jnp.sum(p, axis=-1, keepdims=True), 1e-30), 0.0)
            o_c.append(gt[:, g:g + 1] * _dot(p.astype(BF16), vc))
            psum = p if psum is None else psum + p
        ocg_ref[rows, :] = jnp.concatenate(o_c, axis=1)

        p_hi = psum.astype(BF16)
        rem = psum - p_hi.astype(F32)
        p_mid = rem.astype(BF16)
        p_lo = (rem - p_mid.astype(F32)).astype(BF16)
        imp_t = _dot_nt(ovt, p_hi) + _dot_nt(ovt, p_mid) + _dot_nt(ovt, p_lo)

        qblk = ((blk * qb + qlane) >> int(math.log2(SLC_BLOCK))).astype(F32)
        val = jnp.where(jrow == 0.0, 3e38,
                        jnp.where(jrow == qblk, 3e38, jnp.where(jrow == qblk - 1.0, 3e38, imp_t)))
        val = jnp.where(jrow > qblk, -1.0, val)
        vt_s[sb] = val
        vals = [val[k * SUBLANES:(k + 1) * SUBLANES] for k in range(nv)]
        cnt = [jnp.zeros((SUBLANES, qb), F32) for _ in range(nv)]
        for jp in range(MASK_ROWS):
            rowv = jnp.broadcast_to(vt_s[sb, jp:jp + 1, :], (SUBLANES, qb))
            for k in range(nv):
                if k * SUBLANES > jp:
                    beat = jnp.where(rowv >= vals[k], 1.0, 0.0)
                elif (k + 1) * SUBLANES - 1 < jp:
                    beat = jnp.where(rowv > vals[k], 1.0, 0.0)
                else:
                    beat = jnp.where(sub > jp - k * SUBLANES, jnp.where(rowv >= vals[k], 1.0, 0.0),
                                     jnp.where(rowv > vals[k], 1.0, 0.0))
                cnt[k] = cnt[k] + beat
        selneg_t = jnp.concatenate([jnp.where(c < float(SLC_TOPK), 0.0, NEG) for c in cnt], axis=0)
        for g in range(GQA):
            qt = jnp.concatenate([qpad[g].T[:DK], selneg_t], axis=0)
            qsel_ref[0, sb, :, g * qb:(g + 1) * qb] = qt.astype(qsel_ref.dtype)


def _nsa_attn_kernel(qsel_ref, ocg_ref, gc_ref, gl_ref, gb_ref, ks_ref, vs_ref, gs_ref, kw_ref, vw_ref, bw_ref,
                     o_ref, s_s, p_s):
    qb = Q_BLOCK
    hq = GQA * qb
    nq = ATT_NQ * hq
    i = pl.program_id(2) * ATT_NQ
    step = SEL_TK // qb
    n_tiles = (i * qb) // SEL_TK + 1
    qt = jnp.concatenate([qsel_ref[0, b] for b in range(ATT_NQ)], axis=1)

    def stage_qk(j):
        k0 = pl.multiple_of(j * SEL_TK, SEL_TK)
        sc = _dot(ks_ref[pl.ds(k0, SEL_TK), :], qt)
        for b in range(ATT_NQ):
            dd = jnp.minimum(i + b - j * step, SEL_FAR)
            boff = pl.multiple_of((SEL_FAR - dd) * qb, qb)
            s_s[:, b * hq:(b + 1) * hq] = sc[:, b * hq:(b + 1) * hq] + gs_ref[0, pl.ds(boff, SEL_TK), :]

    def stage_softmax(m):
        m_new, alpha = [], []
        for g in range(ATT_NQ * GQA):
            cs = slice(g * qb, (g + 1) * qb)
            sc = s_s[:, cs]
            mg = jnp.maximum(m[:, cs], jnp.max(sc, axis=0, keepdims=True))
            alpha.append(jnp.exp(m[:, cs] - mg))
            p_s[:, cs] = jnp.exp(sc - mg).astype(BF16)
            m_new.append(mg)
        return jnp.concatenate(m_new, axis=1), jnp.concatenate(alpha, axis=1)

    def stage_pv(j, acc):
        k0 = pl.multiple_of(jnp.maximum(j, 0) * SEL_TK, SEL_TK)
        return acc + _dot(vs_ref[0, :, pl.ds(k0, SEL_TK)], p_s[...])

    def advance(j, m, acc):
        acc = stage_pv(j - 2, acc)
        m, alpha = stage_softmax(m)
        return m, alpha * acc

    def body(j, carry):
        m, acc = advance(j, *carry)
        stage_qk(j)
        return m, acc

    p_s[...] = jnp.zeros(p_s.shape, BF16)
    stage_qk(jnp.int32(0))
    init = (jnp.full((1, nq), NEG, F32), jnp.zeros((LANES, nq), F32))
    m, acc = lax.fori_loop(1, n_tiles, body, init)
    _, acc = advance(n_tiles, m, acc)
    acc_s = stage_pv(n_tiles - 1, acc)

    wk = WINDOW + ATT_NQ * qb
    w0 = pl.multiple_of(jnp.maximum(i * qb - WINDOW, 0), qb)
    rowi = lax.broadcasted_iota(jnp.int32, (LANES, nq), 0)
    q_win = jnp.where(rowi < DK, qt, jnp.zeros_like(qt))
    sw = _dot(kw_ref[pl.ds(w0, wk), :], q_win) + bw_ref[0, 0]
    pw = jnp.exp(sw - jnp.max(sw, axis=0, keepdims=True)).astype(BF16)
    acc_w = _dot(vw_ref[0, :, pl.ds(w0, wk)], pw)

    for b in range(ATT_NQ):
        rows = slice(b * qb, (b + 1) * qb)
        gt = _sigmoid(gl_ref[rows, :] + gb_ref[...])
        outs = []
        for g in range(GQA):
            cs = slice(b * hq + g * qb, b * hq + (g + 1) * qb)
            a_w = acc_w[:, cs].T
            a_s = acc_s[:, cs].T
            o_w = a_w[:, :DK] / a_w[:, DK:DK + 1]
            o_s = a_s[:, :DK] / a_s[:, DK:DK + 1]
            outs.append(gt[:, GQA + g:GQA + g + 1] * o_s + gt[:, 2 * GQA + g:2 * GQA + g + 1] * o_w)
        o = ocg_ref[rows, :] + jnp.concatenate(outs, axis=1)
        o_ref[rows, :] = (o * _silu(gc_ref[rows, :])).astype(o_ref.dtype)


def _rel_bucket(dist):
    n = jnp.maximum(dist, 0)
    exact = REL_BUCKETS // 2
    nf = jnp.maximum(n, 1).astype(jnp.float32)
    large = exact + (jnp.log(nf / exact) / math.log(REL_MAX_DIST / exact)
                     * (REL_BUCKETS - exact)).astype(jnp.int32)
    return jnp.where(n < exact, n, jnp.minimum(large, REL_BUCKETS - 1))


def nsa_bias_tables(rel_bias, seq):
    qb = Q_BLOCK
    n_pad = seq // CMP_STRIDE
    c0 = n_pad - NA
    wc = 2 * n_pad
    ws = SEL_FAR * qb + SEL_TK
    nmax = max(seq, SEL_FAR * qb + qb)
    padl = CMP_STRIDE * (wc + 2)
    bvec = rel_bias[_rel_bucket(jnp.arange(nmax))].T.astype(F32)
    vext = jnp.concatenate([jnp.full((NSA_HEADS, padl), NEG, F32), bvec], axis=1)
    pos = jnp.arange(padl + nmax) - padl
    vwin = jnp.where(pos < WINDOW, vext, NEG)

    mlo, mhi = c0 - wc + 1, NA - 1 + c0
    start = padl + CMP_STRIDE * mlo - (CMP_BLOCK - 1)
    assert start >= 0
    u = vext[:, start:start + CMP_STRIDE * (mhi - mlo + 1)].reshape(NSA_HEADS, mhi - mlo + 1, CMP_STRIDE)
    urev = u[:, ::-1, :]
    gcb = jnp.stack([urev[:, NA - 1 - a:NA - 1 - a + wc, :] for a in range(NA)], axis=1)
    gcb = gcb.transpose(0, 1, 3, 2).reshape(HKV, GQA * qb, wc)
    nqb = seq // qb
    bc = jnp.stack([gcb[:, :, c0 - NA * i:c0 - NA * i + n_pad] for i in range(nqb)], axis=1)

    def toeplitz(vec, d0, width):
        vrev = vec[:, ::-1]
        s0 = nmax - d0 - qb
        period = width + qb - 1
        assert s0 >= 0 and s0 + period <= padl + nmax
        sl = vrev[:, s0:s0 + period]
        y = jnp.concatenate([sl[:, qb - 1:], sl[:, :qb - 1]], axis=1)
        z = jnp.tile(y, (1, qb))[:, :qb * (period - 1)].reshape(NSA_HEADS, qb, period - 1)
        return z[:, :, :width].reshape(HKV, GQA * qb, width)

    gs = toeplitz(vext, SEL_FAR * qb, ws)
    bw = tuple(tuple(toeplitz(vwin, first + b * qb, WINDOW + ATT_NQ * qb) for b in range(ATT_NQ))
               for first in (0, WINDOW))
    return bc, gs, bw


def _overlap_matrix_t(n_pad):
    c_start = np.arange(n_pad)[None, :] * CMP_STRIDE
    s_start = np.arange(MASK_ROWS)[:, None] * SLC_BLOCK
    ov = np.clip(np.minimum(c_start + CMP_BLOCK, s_start + SLC_BLOCK) - np.maximum(c_start, s_start), 0, None)
    return ov.astype(np.float32) / CMP_BLOCK


def nsa_branch(h1, h2, kv, hgl, gate_b4, kct, vc, tables, bsz, seq):
    qb = Q_BLOCK
    nqb = seq // qb
    n_pad = seq // CMP_STRIDE
    n_slc = seq // SLC_BLOCK
    assert n_slc <= MASK_ROWS and seq % SEL_TK == 0 and nqb % SEL_SUB == 0
    assert (SEL_TK // qb) % ATT_NQ == 0 and nqb % ATT_NQ == 0
    bc, gs, bw = tables
    m = bsz * seq
    gw = GQA * DK

    sq = SEL_SUB * qb
    nsq = seq // sq
    ovt = jnp.asarray(_overlap_matrix_t(n_pad), BF16)
    qsel, ocg = pl.pallas_call(
        _nsa_select_kernel,
        grid=(HKV, bsz, nsq),
        in_specs=[
            pl.BlockSpec((sq, gw), lambda hh, b, i: (b * nsq + i, C1_QC // gw + hh)),
            pl.BlockSpec((sq, LANES), lambda hh, b, i: (b * nsq + i, hh)),
            pl.BlockSpec((1, LANES), lambda hh, b, i: (0, hh)),
            pl.BlockSpec((1, DK, n_pad), lambda hh, b, i: (b * HKV + hh, 0, 0)),
            pl.BlockSpec((1, n_pad, DK), lambda hh, b, i: (b * HKV + hh, 0, 0)),
            pl.BlockSpec((1, SEL_SUB) + bc.shape[2:], lambda hh, b, i: (hh, i, 0, 0)),
            pl.BlockSpec(ovt.shape, lambda hh, b, i: (0, 0)),
        ],
        out_specs=[pl.BlockSpec((1, SEL_SUB, LANES, GQA * qb), lambda hh, b, i: (hh, b * nsq + i, 0, 0)),
                   pl.BlockSpec((sq, gw), lambda hh, b, i: (b * nsq + i, hh))],
        out_shape=[jax.ShapeDtypeStruct((HKV, bsz * nqb, LANES, GQA * qb), BF16),
                   jax.ShapeDtypeStruct((m, D_BR), F32)],
        scratch_shapes=[pltpu.VMEM((SEL_SUB, MASK_ROWS, qb), F32)],
        compiler_params=_params(),
        name="nsa_select",
    )(h1, hgl, gate_b4, kct, vc, bc, ovt)

    gs_t = gs.transpose(0, 2, 1)
    bw_t = jnp.stack([jnp.concatenate([w.transpose(0, 2, 1) for w in variant], axis=2) for variant in bw],
                     axis=1)
    ks_ext, kw_ext, vs_ext, vw_ext = kv

    aq = ATT_NQ * qb
    npair = nqb // ATT_NQ
    rowblk = lambda width, c0: pl.BlockSpec((aq, width), lambda hh, b, i: (b * npair + i, c0 // width + hh))
    keys = pl.BlockSpec((seq, LANES), lambda hh, b, i: (b, hh))
    vals = pl.BlockSpec((1, LANES, seq), lambda hh, b, i: (hh, 0, b))
    return pl.pallas_call(
        _nsa_attn_kernel,
        grid=(HKV, bsz, npair),
        in_specs=[
            pl.BlockSpec((1, ATT_NQ, LANES, GQA * qb), lambda hh, b, i: (hh, b * npair + i, 0, 0)),
            rowblk(gw, 0), rowblk(gw, C2_GC),
            pl.BlockSpec((aq, LANES), lambda hh, b, i: (b * npair + i, hh)),
            pl.BlockSpec((1, LANES), lambda hh, b, i: (0, hh)),
            keys, vals, pl.BlockSpec((1,) + gs_t.shape[1:], lambda hh, b, i: (hh, 0, 0)),
            keys, vals, pl.BlockSpec((1, 1) + bw_t.shape[2:], lambda hh, b, i: (hh, jnp.minimum(i, 1), 0, 0)),
        ],
        out_specs=pl.BlockSpec((aq, gw), lambda hh, b, i: (b * npair + i, hh)),
        out_shape=jax.ShapeDtypeStruct((m, D_BR), BF16),
        scratch_shapes=[pltpu.VMEM((SEL_TK, ATT_NQ * GQA * qb), F32), pltpu.VMEM((SEL_TK, ATT_NQ * GQA * qb), BF16)],
        compiler_params=_params(),
        name="nsa_attn",
    )(qsel, ocg, h2, hgl, gate_b4, ks_ext, vs_ext, gs_t, kw_ext, vw_ext, bw_t)


def _final_kernel(oa_ref, ob_ref, oc_ref, om_ref, *rest):
    gate_refs = rest[:2 * N_BRANCHES]
    x_ref, wb_ref, wo_ref, lng_ref, lnb_ref, y_ref, yb_ref = rest[2 * N_BRANCHES:]
    o_refs = (oa_ref, ob_ref, oc_ref, om_ref)
    halves = []
    for c in range(2):
        cs = slice(c * D_BR, (c + 1) * D_BR)
        acc = None
        for k in range(N_BRANCHES):
            term = _sigmoid(gate_refs[2 * k + c][...]) * _dot(o_refs[k][...], wb_ref[k, :, cs])
            acc = term if acc is None else acc + term
        halves.append(acc.astype(BF16))
    merged = jnp.concatenate(halves, axis=1)
    z = DN_ALPHA * x_ref[...] + _dot(merged, wo_ref[...])
    mu = jnp.mean(z, axis=-1, keepdims=True)
    zc = z - mu
    var = jnp.mean(zc * zc, axis=-1, keepdims=True)
    y = zc * lax.rsqrt(var + LN_EPS) * lng_ref[...] + lnb_ref[...]
    y_ref[...] = y
    yb_ref[...] = y.astype(BF16)


def final_merge(o_a, o_b, o_c, o_m, h2, x, w_branch, w_out, ln_g, ln_b, tm=128):
    m = x.shape[0]
    tm = min(tm, m)
    br = pl.BlockSpec((tm, D_BR), lambda i: (i, 0))
    gate = lambda k: pl.BlockSpec((tm, D_BR), lambda i, k=k: (i, C2_MERGE // D_BR + k))
    xs = pl.BlockSpec((tm, D_MODEL), lambda i: (i, 0))
    resident = lambda a: pl.BlockSpec(a.shape, lambda i: (0,) * a.ndim, pipeline_mode=pl.Buffered(1))
    wb = w_branch.astype(BF16)
    wo = w_out.astype(BF16)
    lg, lb = ln_g.reshape(1, D_MODEL), ln_b.reshape(1, D_MODEL)
    ngate = 2 * N_BRANCHES
    return pl.pallas_call(
        _final_kernel,
        grid=(m // tm,),
        in_specs=[br, br, br, br] + [gate(k) for k in range(ngate)] + [xs]
                 + [resident(wb), resident(wo), resident(lg), resident(lb)],
        out_specs=[xs, xs],
        out_shape=[jax.ShapeDtypeStruct((m, D_MODEL), F32), jax.ShapeDtypeStruct((m, D_MODEL), BF16)],
        compiler_params=_params(),
        name="merge_out_ln",
    )(o_a, o_b, o_c, o_m, *([h2] * ngate), x, wb, wo, lg, lb)


def _gate_spread_matrix():
    p = np.zeros((3 * NSA_HEADS, HKV * LANES), np.float32)
    for hh in range(HKV):
        for brn in range(3):
            for g in range(GQA):
                p[brn * NSA_HEADS + hh * GQA + g, hh * LANES + brn * GQA + g] = 1.0
    return p


def _spread_gate_cols(a):
    return jnp.dot(a, jnp.asarray(_gate_spread_matrix()), precision=lax.Precision.HIGHEST)


def layer(l, x, xb, mem_b, tables, bsz, seq, w_in, sgu_ln_g, sgu_ln_b, sgu_w, sgu_b, conv_w, conv_b,
          lru_wa, lru_ba, lru_wx, lru_bx, lru_lambda, cmp_pe_k, cmp_pe_v, cmp_w1_k, cmp_w1_v,
          cmp_w2_k, cmp_w2_v, nsa_gate_b, w_mem_kv, w_branch, w_out, ln_g, ln_b):
    m = bsz * seq
    tm = min(1024, m)
    h1 = matmul_f32w(xb, w_in, l, 0, N1, F32, tm, 512, w_is_nk=True)
    kv = kv_projection(xb, w_in, l, N1, seq, tm)
    h2 = matmul_f32w(xb, w_in, l, GL_OFF + 3 * NSA_HEADS, N2, F32, tm, 512, w_is_nk=True)
    w_gl = _spread_gate_cols(w_in[l, GL_OFF:GL_OFF + 3 * NSA_HEADS, :].T).astype(BF16)
    hgl = matmul(xb, w_gl, F32, tm, HKV * LANES)
    gate_b4 = _spread_gate_cols(nsa_gate_b[l].reshape(1, 3 * NSA_HEADS))

    o_a = gmlp_branch(h1, sgu_ln_g[l], sgu_ln_b[l], sgu_w[l], sgu_b[l])
    o_b = lru_branch(h1, bsz, seq, conv_w[l], conv_b[l], lru_wa[l], lru_ba[l], lru_wx[l], lru_bx[l],
                     lru_lambda[l])

    def stride_blocks(c):
        t = h1[:, c:c + KV_W].reshape(bsz, seq // CMP_STRIDE, CMP_STRIDE, HKV, DK)
        return t.transpose(0, 3, 1, 2, 4).reshape(bsz * HKV, seq // CMP_STRIDE, CMP_STRIDE * DK)

    kct, vc = nsa_compress(stride_blocks(C1_KV), stride_blocks(C1_KV + KV_W), cmp_pe_k[l], cmp_pe_v[l],
                           cmp_w1_k[l], cmp_w1_v[l], cmp_w2_k[l], cmp_w2_v[l])
    o_c = nsa_branch(h1, h2, kv, hgl, gate_b4, kct, vc, tables, bsz, seq)

    mrows = mem_b.shape[0]
    kv = matmul_f32w(mem_b, w_mem_kv, l, 0, 2 * D_BR, BF16, min(512, mrows), 512)
    o_m = mem_branch(h2, kv, bsz, seq)
    return final_merge(o_a, o_b, o_c, o_m, h2, x, w_branch[l], w_out[l], ln_g[l], ln_b[l])


def kernel(x, mem, rel_bias, w_in, sgu_ln_g, sgu_ln_b, sgu_w, sgu_b, conv_w, conv_b, lru_wa, lru_ba, lru_wx,
           lru_bx, lru_lambda, cmp_pe_k, cmp_pe_v, cmp_w1_k, cmp_w1_v, cmp_w2_k, cmp_w2_v, nsa_gate_b,
           w_mem_kv, w_branch, w_out, ln_g, ln_b):
    bsz, seq, _ = x.shape
    tables = nsa_bias_tables(rel_bias, seq)
    xf = x.reshape(bsz * seq, D_MODEL)
    xb = xf.astype(BF16)
    mem_b = mem.reshape(-1, D_MODEL).astype(BF16)
    w_in_t = jnp.swapaxes(w_in, 1, 2)
    params = (w_in_t, sgu_ln_g, sgu_ln_b, sgu_w, sgu_b, conv_w, conv_b, lru_wa, lru_ba, lru_wx, lru_bx,
              lru_lambda, cmp_pe_k, cmp_pe_v, cmp_w1_k, cmp_w1_v, cmp_w2_k, cmp_w2_v, nsa_gate_b,
              w_mem_kv, w_branch, w_out, ln_g, ln_b)
    for l in range(w_in.shape[0]):
        xf, xb = layer(l, xf, xb, mem_b, tables, bsz, seq, *params)
    return xf.reshape(bsz, seq, D_MODEL)
```

```python
import functools
import math

import numpy as np
import jax
import jax.numpy as jnp
from jax import lax
from jax.experimental import pallas as pl
from jax.experimental.pallas import tpu as pltpu

F32 = jnp.float32
BF16 = jnp.bfloat16

D_MODEL = 2048
DEPTH = 2
D_BR = D_MODEL // 2
N_BRANCHES = 4
GMLP_CHUNK = 128
GMLP_GROUPS = 8
LRU_BLOCKS = 8
LRU_BLOCK_DIM = D_BR // LRU_BLOCKS
CONV_WIDTH = 4
LRU_C = 8.0
DK = 64
NSA_HEADS = D_BR // DK
HKV = NSA_HEADS // 4
GQA = NSA_HEADS // HKV
KV_W = HKV * DK
CMP_BLOCK = 32
CMP_STRIDE = 16
CMP_HIDDEN = 256
SLC_BLOCK = 64
SLC_TOPK = 8
WINDOW = 256
Q_BLOCK = 128
MEM_HEADS = 4
MEM_HEAD_DIM = D_BR // MEM_HEADS
REL_BUCKETS = 32
REL_MAX_DIST = 1024
DN_ALPHA = (2 * DEPTH) ** 0.25
LN_EPS = 1e-5

LANES = 128
SUBLANES = 8
NEG = -1e30
SEL_TK = 512
SEL_FAR = 11
MASK_ROWS = LANES - DK
NA = Q_BLOCK // CMP_STRIDE
SEL_SUB = 4
ATT_NQ = 2
V_ROWS = DK + 16
VMEM_LIMIT = 56 * 1024 * 1024

C1_U, C1_V, C1_GA, C1_XB, C1_GB, C1_QC = (D_BR * k for k in range(6))
C1_KV = 6 * D_BR
N1 = C1_KV + 2 * KV_W
NKV = 4 * KV_W
C2_GC, C2_QM, C2_GM, C2_MERGE = (D_BR * k for k in range(4))
N2 = C2_MERGE + N_BRANCHES * D_MODEL
GL_OFF = N1 + NKV


def _sigmoid(x):
    return 1.0 / (1.0 + jnp.exp(-x))


def _silu(x):
    return x * _sigmoid(x)


def _gelu_tanh(x):
    return 0.5 * x * (1.0 + jnp.tanh(math.sqrt(2.0 / math.pi) * (x + 0.044715 * (x * x * x))))


def _dot(a, b):
    return jnp.dot(a, b, preferred_element_type=F32)


def _dot_nt(a, b):
    return lax.dot_general(a, b, (((1,), (1,)), ((), ())), preferred_element_type=F32)


def _params(**kw):
    return pltpu.CompilerParams(vmem_limit_bytes=VMEM_LIMIT, **kw)


def _mm_kernel(x_ref, w_ref, o_ref):
    o_ref[...] = _dot(x_ref[...], w_ref[...]).astype(o_ref.dtype)


def matmul(x, w, out_dtype, tm, tn):
    m, k = x.shape
    n = w.shape[1]
    assert m % tm == 0 and n % tn == 0, (m, n, tm, tn)
    return pl.pallas_call(
        _mm_kernel,
        grid=(n // tn, m // tm),
        in_specs=[pl.BlockSpec((tm, k), lambda j, i: (i, 0)),
                  pl.BlockSpec((k, tn), lambda j, i: (0, j))],
        out_specs=pl.BlockSpec((tm, tn), lambda j, i: (i, j)),
        out_shape=jax.ShapeDtypeStruct((m, n), out_dtype),
        compiler_params=_params(),
        name="matmul",
    )(x, w)


def _mm_castw_kernel(x_ref, *rest, w_is_nk, shift):
    w_refs, (o_ref, wb_ref) = rest[:-2], rest[-2:]

    @pl.when(pl.program_id(1) == 0)
    def _():
        if shift:
            w = jnp.concatenate([w_refs[0][0, shift:, :], w_refs[1][0, :shift, :]], axis=0)
        else:
            w = w_refs[0][0]
        wb_ref[...] = (w.T if w_is_nk else w).astype(BF16)

    o_ref[...] = _dot(x_ref[...], wb_ref[...]).astype(o_ref.dtype)


def matmul_f32w(x, w3, layer_idx, col0, n, out_dtype, tm, tn, w_is_nk=False):
    m, k = x.shape
    ncols = w3.shape[1] if w_is_nk else w3.shape[2]
    shift = col0 % tn
    assert m % tm == 0 and n % tn == 0 and col0 + n <= ncols, (m, n, col0, tm, tn)
    assert shift == 0 or (w_is_nk and shift % SUBLANES == 0), (col0, tn)
    j0 = col0 // tn
    if w_is_nk:
        w_specs = [pl.BlockSpec((1, tn, k), lambda j, i, d=d: (layer_idx, j0 + j + d, 0))
                   for d in range(2 if shift else 1)]
    else:
        w_specs = [pl.BlockSpec((1, k, tn), lambda j, i: (layer_idx, 0, j0 + j))]
    return pl.pallas_call(
        functools.partial(_mm_castw_kernel, w_is_nk=w_is_nk, shift=shift),
        grid=(n // tn, m // tm),
        in_specs=[pl.BlockSpec((tm, k), lambda j, i: (i, 0))] + w_specs,
        out_specs=pl.BlockSpec((tm, tn), lambda j, i: (i, j)),
        out_shape=jax.ShapeDtypeStruct((m, n), out_dtype),
        scratch_shapes=[pltpu.VMEM((k, tn), BF16)],
        compiler_params=_params(dimension_semantics=("arbitrary", "arbitrary")),
        name="matmul_f32w",
    )(x, *([w3] * len(w_specs)))


def _kv_proj_kernel(x_ref, ws_ref, ww_ref, ks_ref, kw_ref, vs_ref, vw_ref, wk_s, wv_s, *, tm, seq):
    @pl.when(pl.program_id(0) == 0)
    def _():
        wk = jnp.concatenate([ws_ref[0, 0:KV_W], ww_ref[0, 0:KV_W]], axis=0)
        wk_s[...] = wk.T.astype(BF16)
        wv_s[...] = jnp.concatenate([ws_ref[0, KV_W:2 * KV_W], ww_ref[0, KV_W:2 * KV_W]], axis=0).astype(BF16)

    x = x_ref[...]
    kk = _dot(x, wk_s[...])
    vt = _dot_nt(wv_s[...], x)
    pos = (pl.program_id(0) * tm + lax.broadcasted_iota(jnp.int32, (tm, MASK_ROWS), 0)) % seq
    blk = lax.broadcasted_iota(jnp.int32, (tm, MASK_ROWS), 1)
    onehot = jnp.where((pos >> int(math.log2(SLC_BLOCK))) == blk, 1.0, 0.0)
    zeros = jnp.zeros((tm, MASK_ROWS), F32)
    tail = jnp.where(lax.broadcasted_iota(jnp.int32, (LANES - DK, tm), 0) == 0, 1.0, 0.0)
    for h in range(HKV):
        ks_ref[:, h * LANES:(h + 1) * LANES] = jnp.concatenate(
            [kk[:, h * DK:(h + 1) * DK], onehot], axis=1).astype(ks_ref.dtype)
        kw_ref[:, h * LANES:(h + 1) * LANES] = jnp.concatenate(
            [kk[:, KV_W + h * DK:KV_W + (h + 1) * DK], zeros], axis=1).astype(kw_ref.dtype)
        vs_ref[h] = jnp.concatenate([vt[h * DK:(h + 1) * DK], tail], axis=0).astype(vs_ref.dtype)
        vw_ref[h] = jnp.concatenate([vt[KV_W + h * DK:KV_W + (h + 1) * DK], tail], axis=0).astype(vw_ref.dtype)


def kv_projection(x, wt3, layer_idx, row0, seq, tm):
    m, k = x.shape
    nrows = 2 * KV_W
    assert m % tm == 0 and row0 % nrows == 0 and seq % tm == 0
    wspec = lambda d: pl.BlockSpec((1, nrows, k), lambda i: (layer_idx, row0 // nrows + d, 0),
                                   pipeline_mode=pl.Buffered(1))
    keys = jax.ShapeDtypeStruct((m, HKV * LANES), BF16)
    vals = jax.ShapeDtypeStruct((HKV, LANES, m), BF16)
    return pl.pallas_call(
        functools.partial(_kv_proj_kernel, tm=tm, seq=seq),
        grid=(m // tm,),
        in_specs=[pl.BlockSpec((tm, k), lambda i: (i, 0)), wspec(0), wspec(1)],
        out_specs=[pl.BlockSpec((tm, HKV * LANES), lambda i: (i, 0))] * 2
                  + [pl.BlockSpec((HKV, LANES, tm), lambda i: (0, 0, i))] * 2,
        out_shape=[keys, keys, vals, vals],
        scratch_shapes=[pltpu.VMEM((k, 2 * KV_W), BF16), pltpu.VMEM((2 * KV_W, k), BF16)],
        compiler_params=_params(dimension_semantics=("arbitrary",)),
        name="kv_projection",
    )(x, wt3, wt3)


def _gmlp_kernel(u_ref, v_ref, ga_ref, lng_ref, lnb_ref, w_ref, bs_ref, o_ref, *, rows):
    gd = D_BR // GMLP_GROUPS
    for c in range(rows // GMLP_CHUNK):
        r = slice(c * GMLP_CHUNK, (c + 1) * GMLP_CHUNK)
        v = _gelu_tanh(v_ref[r, :])
        mu = jnp.mean(v, axis=-1, keepdims=True)
        vc = v - mu
        var = jnp.mean(vc * vc, axis=-1, keepdims=True)
        vb = (vc * lax.rsqrt(var + LN_EPS) * lng_ref[...] + lnb_ref[...]).astype(BF16)
        u = _gelu_tanh(u_ref[r, :]) * _silu(ga_ref[r, :])
        for g in range(GMLP_GROUPS):
            cs = slice(g * gd, (g + 1) * gd)
            mixed = _dot(w_ref[g], vb[:, cs]) + bs_ref[:, g:g + 1]
            o_ref[r, cs] = (u[:, cs] * mixed).astype(o_ref.dtype)


def gmlp_branch(h1, ln_g, ln_b, w_s, b_s, rows=512):
    m = h1.shape[0]
    rows = min(rows, m)
    causal = jnp.tril(jnp.ones((GMLP_CHUNK, GMLP_CHUNK), dtype=bool))
    w = jnp.where(causal, w_s, 0).astype(BF16)
    col = lambda c: pl.BlockSpec((rows, D_BR), lambda i, c=c: (i, c // D_BR))
    full = lambda a: pl.BlockSpec(a.shape, lambda i: (0,) * a.ndim)
    args = (ln_g.reshape(1, D_BR), ln_b.reshape(1, D_BR), w, b_s.T)
    return pl.pallas_call(
        functools.partial(_gmlp_kernel, rows=rows),
        grid=(m // rows,),
        in_specs=[col(C1_U), col(C1_V), col(C1_GA)] + [full(a) for a in args],
        out_specs=pl.BlockSpec((rows, D_BR), lambda i: (i, 0)),
        out_shape=jax.ShapeDtypeStruct((m, D_BR), BF16),
        compiler_params=_params(),
        name="gmlp",
    )(h1, h1, h1, *args)


def _lru_kernel(xb_ref, gb_ref, cw_ref, cb_ref, wa_ref, ba_ref, wx_ref, bx_ref, lam_ref, o_ref,
                xbuf, hcarry, a_s, g_s, *, ts):
    @pl.when(pl.program_id(1) == 0)
    def _():
        xbuf[0:8, :] = jnp.zeros((8, D_BR), F32)
        hcarry[...] = jnp.zeros((8, D_BR), F32)

    xbuf[8:8 + ts, :] = xb_ref[...]
    xc = cb_ref[...] + cw_ref[0:1, :] * xbuf[pl.ds(8 - (CONV_WIDTH - 1), ts), :]
    for k in range(1, CONV_WIDTH):
        xc = xc + cw_ref[k:k + 1, :] * xbuf[pl.ds(8 - (CONV_WIDTH - 1) + k, ts), :]
    xbuf[0:8, :] = xbuf[ts:ts + 8, :]

    xcb = xc.astype(BF16)
    bd = LRU_BLOCK_DIM
    r = jnp.concatenate([_dot(xcb[:, n * bd:(n + 1) * bd], wa_ref[n]) for n in range(LRU_BLOCKS)], axis=1)
    i = jnp.concatenate([_dot(xcb[:, n * bd:(n + 1) * bd], wx_ref[n]) for n in range(LRU_BLOCKS)], axis=1)
    r = _sigmoid(r + ba_ref[...])
    i = _sigmoid(i + bx_ref[...])
    nl = -lam_ref[...]
    softplus = jnp.maximum(nl, 0.0) + jnp.log1p(jnp.exp(-jnp.abs(nl)))
    log_a = (-LRU_C * softplus) * r
    a = jnp.exp(log_a)
    a_s[...] = a
    g_s[...] = jnp.sqrt(1.0 - a * a) * i * xc

    row = lax.broadcasted_iota(jnp.int32, (8, D_BR), 0)

    def body(j, carry):
        r0 = pl.multiple_of(j * 8, 8)
        av = a_s[pl.ds(r0, 8), :]
        bv = g_s[pl.ds(r0, 8), :]
        for d in (1, 2, 4):
            keep = row >= d
            a_sh = pltpu.roll(av, d, axis=0)
            b_sh = pltpu.roll(bv, d, axis=0)
            bv = jnp.where(keep, av * b_sh + bv, bv)
            av = jnp.where(keep, av * a_sh, av)
        hv = av * carry + bv
        g_s[pl.ds(r0, 8), :] = hv
        return jnp.broadcast_to(hv[7:8, :], (8, D_BR))

    hcarry[...] = lax.fori_loop(0, ts // 8, body, hcarry[...])
    o_ref[...] = (g_s[...] * _silu(gb_ref[...])).astype(o_ref.dtype)


def lru_branch(h1, bsz, seq, conv_w, conv_b, wa, ba, wx, bx, lam, ts=512):
    ts = min(ts, seq)
    ns = seq // ts
    col = lambda c: pl.BlockSpec((ts, D_BR), lambda b, s, c=c: (b * ns + s, c // D_BR))
    full = lambda a: pl.BlockSpec(a.shape, lambda b, s: (0,) * a.ndim)
    row = lambda a: a.reshape(1, D_BR)
    args = (conv_w, row(conv_b), wa.astype(BF16), row(ba), wx.astype(BF16), row(bx), row(lam))
    return pl.pallas_call(
        functools.partial(_lru_kernel, ts=ts),
        grid=(bsz, ns),
        in_specs=[col(C1_XB), col(C1_GB)] + [full(a) for a in args],
        out_specs=pl.BlockSpec((ts, D_BR), lambda b, s: (b * ns + s, 0)),
        out_shape=jax.ShapeDtypeStruct((bsz * seq, D_BR), BF16),
        scratch_shapes=[pltpu.VMEM((ts + 8, D_BR), F32), pltpu.VMEM((8, D_BR), F32),
                        pltpu.VMEM((ts, D_BR), F32), pltpu.VMEM((ts, D_BR), F32)],
        compiler_params=_params(dimension_semantics=("arbitrary", "arbitrary")),
        name="lru",
    )(h1, h1, *args)


def _mem_kernel(q_ref, g_ref, kv_ref, o_ref):
    hd = MEM_HEAD_DIM
    for hh in range(MEM_HEADS):
        cs = slice(hh * hd, (hh + 1) * hd)
        q = (q_ref[:, cs] * (hd ** -0.5)).astype(BF16)
        s = _dot_nt(q, kv_ref[:, cs])
        p = jnp.exp(s - jnp.max(s, axis=-1, keepdims=True))
        l = jnp.sum(p, axis=-1, keepdims=True)
        o = _dot(p.astype(BF16), kv_ref[:, D_BR + hh * hd:D_BR + (hh + 1) * hd]) / l
        o_ref[:, cs] = (o * _silu(g_ref[:, cs])).astype(o_ref.dtype)


def mem_branch(h2, kv, bsz, seq, tq=512):
    tq = min(tq, seq)
    nq = seq // tq
    mlen = kv.shape[0] // bsz
    col = lambda c: pl.BlockSpec((tq, D_BR), lambda b, i, c=c: (b * nq + i, c // D_BR))
    return pl.pallas_call(
        _mem_kernel,
        grid=(bsz, nq),
        in_specs=[col(C2_QM), col(C2_GM), pl.BlockSpec((mlen, 2 * D_BR), lambda b, i: (b, 0))],
        out_specs=pl.BlockSpec((tq, D_BR), lambda b, i: (b * nq + i, 0)),
        out_shape=jax.ShapeDtypeStruct((bsz * seq, D_BR), BF16),
        compiler_params=_params(),
        name="mem_attn",
    )(h2, h2, kv)


def _cmp_kernel(*refs):
    ngrp = 2 * KV_W // LANES
    t_refs = refs[:ngrp]
    pek_ref, pev_ref, w1k_ref, w1v_ref, w2kt_ref, w2v_ref, kct_ref, vc_ref = refs[ngrp:]
    n = t_refs[0].shape[0] // CMP_STRIDE
    for h in range(HKV):
        for is_v, (pe_ref, w1_ref) in enumerate(((pek_ref, w1k_ref), (pev_ref, w1v_ref))):
            grp, off = divmod(is_v * KV_W + h * DK, LANES)
            lo = hi = None
            for tok in range(CMP_STRIDE):
                x = t_refs[grp][pl.ds(tok, n, stride=CMP_STRIDE), :][:, off:off + DK]
                t2 = CMP_STRIDE + tok
                a = _dot((x + pe_ref[tok:tok + 1, :]).astype(BF16), w1_ref[tok * DK:(tok + 1) * DK, :])
                b = _dot((x + pe_ref[t2:t2 + 1, :]).astype(BF16), w1_ref[t2 * DK:(t2 + 1) * DK, :])
                lo = a if lo is None else lo + a
                hi = b if hi is None else hi + b
            hidden = _silu(lo + pltpu.roll(hi, n - 1, axis=0)).astype(BF16)
            if is_v:
                vc_ref[h] = _dot(hidden, w2v_ref[...]).astype(vc_ref.dtype)
            else:
                kct_ref[h] = _dot_nt(w2kt_ref[...], hidden).astype(kct_ref.dtype)


def nsa_compress(h1, bsz, seq, pe_k, pe_v, w1_k, w1_v, w2_k, w2_v):
    n = seq // CMP_STRIDE
    args = (pe_k, pe_v, w1_k.astype(BF16), w1_v.astype(BF16), w2_k.T.astype(BF16), w2_v.astype(BF16))
    full = lambda a: pl.BlockSpec(a.shape, lambda b: (0,) * a.ndim)
    return pl.pallas_call(
        _cmp_kernel,
        grid=(bsz,),
        in_specs=[pl.BlockSpec((seq, LANES), lambda b, g=g: (b, C1_KV // LANES + g))
                  for g in range(2 * KV_W // LANES)] + [full(a) for a in args],
        out_specs=[pl.BlockSpec((HKV, DK, n), lambda b: (b, 0, 0)), pl.BlockSpec((HKV, n, DK), lambda b: (b, 0, 0))],
        out_shape=[jax.ShapeDtypeStruct((bsz * HKV, DK, n), BF16), jax.ShapeDtypeStruct((bsz * HKV, n, DK), BF16)],
        compiler_params=_params(),
        name="nsa_compress",
    )(*([h1] * (2 * KV_W // LANES)), *args)


def _nsa_select_kernel(q_ref, gl_ref, gb_ref, kct_ref, vc_ref, bc_ref, ovt_ref, qsel_ref, ocg_ref, vt_s):
    qb = Q_BLOCK
    lane = lax.broadcasted_iota(jnp.int32, (qb, LANES), 1)
    jrow = lax.broadcasted_iota(jnp.int32, (MASK_ROWS, qb), 0).astype(F32)
    qlane = lax.broadcasted_iota(jnp.int32, (MASK_ROWS, qb), 1)
    sub = lax.broadcasted_iota(jnp.int32, (SUBLANES, qb), 0)
    nv = MASK_ROWS // SUBLANES
    ovt = ovt_ref[...]
    kct = kct_ref[0]
    vc = vc_ref[0]
    for sb in range(SEL_SUB):
        rows = slice(sb * qb, (sb + 1) * qb)
        blk = pl.program_id(2) * SEL_SUB + sb
        q = q_ref[rows, :] * (DK ** -0.5)
        gt = _sigmoid(gl_ref[rows, :] + gb_ref[...])
        qpad = []
        for g in range(GQA):
            t = q[:, (g // 2) * LANES:(g // 2 + 1) * LANES]
            if g % 2:
                t = pltpu.roll(t, DK, axis=1)
            qpad.append(jnp.where(lane < DK, t, 0.0))

        has_keys = blk * qb + lax.broadcasted_iota(jnp.int32, (qb, 1), 0) >= CMP_BLOCK - 1
        o_c = []
        psum = None
        for g in range(GQA):
            s = _dot(qpad[g][:, :DK].astype(BF16), kct) + bc_ref[0, sb, g * qb:(g + 1) * qb, :]
            m = jnp.max(s, axis=-1, keepdims=True)
            p = jnp.exp(s - m)
            p = p * jnp.where(has_keys, 1.0 / jnp.maximum(jnp.sum(p, axis=-1, keepdims=True), 1e-30), 0.0)
            o_c.append(gt[:, g:g + 1] * _dot(p.astype(BF16), vc))
            psum = p if psum is None else psum + p
        ocg_ref[rows, :] = jnp.concatenate(o_c, axis=1)

        p_hi = psum.astype(BF16)
        rem = psum - p_hi.astype(F32)
        p_mid = rem.astype(BF16)
        p_lo = (rem - p_mid.astype(F32)).astype(BF16)
        imp_t = _dot_nt(ovt, p_hi) + _dot_nt(ovt, p_mid) + _dot_nt(ovt, p_lo)

        qblk = ((blk * qb + qlane) >> int(math.log2(SLC_BLOCK))).astype(F32)
        val = jnp.where(jrow == 0.0, 3e38,
                        jnp.where(jrow == qblk, 3e38, jnp.where(jrow == qblk - 1.0, 3e38, imp_t)))
        val = jnp.where(jrow > qblk, -1.0, val)
        vt_s[sb] = val
        vals = [val[k * SUBLANES:(k + 1) * SUBLANES] for k in range(nv)]
        cnt = [jnp.zeros((SUBLANES, qb), F32) for _ in range(nv)]
        for jp in range(MASK_ROWS):
            rowv = jnp.broadcast_to(vt_s[sb, jp:jp + 1, :], (SUBLANES, qb))
            for k in range(nv):
                if k * SUBLANES > jp:
                    beat = jnp.where(rowv >= vals[k], 1.0, 0.0)
                elif (k + 1) * SUBLANES - 1 < jp:
                    beat = jnp.where(rowv > vals[k], 1.0, 0.0)
                else:
                    beat = jnp.where(sub > jp - k * SUBLANES, jnp.where(rowv >= vals[k], 1.0, 0.0),
                                     jnp.where(rowv > vals[k], 1.0, 0.0))
                cnt[k] = cnt[k] + beat
        selneg_t = jnp.concatenate([jnp.where(c < float(SLC_TOPK), 0.0, NEG) for c in cnt], axis=0)
        for g in range(GQA):
            qt = jnp.concatenate([qpad[g].T[:DK], selneg_t], axis=0)
            qsel_ref[0, sb, :, g * qb:(g + 1) * qb] = qt.astype(qsel_ref.dtype)


def _nsa_attn_kernel(qsel_ref, ocg_ref, gc_ref, gl_ref, gb_ref, ks_ref, vs_ref, gs_ref, kw_ref, vw_ref, bw_ref,
                     o_ref, s_s, p_s):
    qb = Q_BLOCK
    hq = GQA * qb
    nq = ATT_NQ * hq
    i = pl.program_id(2) * ATT_NQ
    step = SEL_TK // qb
    n_tiles = (i * qb) // SEL_TK + 1
    qt = jnp.concatenate([qsel_ref[0, b] for b in range(ATT_NQ)], axis=1)

    def stage_qk(j):
        k0 = pl.multiple_of(j * SEL_TK, SEL_TK)
        sc = _dot(ks_ref[pl.ds(k0, SEL_TK), :], qt)
        for b in range(ATT_NQ):
            dd = jnp.minimum(i + b - j * step, SEL_FAR)
            boff = pl.multiple_of((SEL_FAR - dd) * qb, qb)
            s_s[:, b * hq:(b + 1) * hq] = (sc[:, b * hq:(b + 1) * hq].astype(BF16)
                                           + gs_ref[0, pl.ds(boff, SEL_TK), :])

    def stage_softmax(m):
        m_new, alpha = [], []
        for g in range(ATT_NQ * GQA):
            cs = slice(g * qb, (g + 1) * qb)
            sc = s_s[:, cs]
            mg = jnp.maximum(m[:, cs], jnp.max(sc, axis=0, keepdims=True).astype(F32))
            alpha.append(jnp.exp(m[:, cs] - mg))
            p_s[:, cs] = jnp.exp(sc - mg.astype(BF16))
            m_new.append(mg)
        return jnp.concatenate(m_new, axis=1), jnp.concatenate(alpha, axis=1)

    def stage_pv(j, acc):
        k0 = pl.multiple_of(jnp.maximum(j, 0) * SEL_TK, SEL_TK)
        return acc + _dot(vs_ref[0, :V_ROWS, pl.ds(k0, SEL_TK)], p_s[...])

    def advance(j, m, acc):
        acc = stage_pv(j - 2, acc)
        m, alpha = stage_softmax(m)
        return m, alpha * acc

    def body(j, carry):
        m, acc = advance(j, *carry)
        stage_qk(j)
        return m, acc

    p_s[...] = jnp.zeros(p_s.shape, BF16)
    stage_qk(jnp.int32(0))
    init = (jnp.full((1, nq), NEG, F32), jnp.zeros((V_ROWS, nq), F32))
    m, acc = lax.fori_loop(1, n_tiles, body, init)
    _, acc = advance(n_tiles, m, acc)
    acc_s = stage_pv(n_tiles - 1, acc)

    wk = WINDOW + ATT_NQ * qb
    w0 = pl.multiple_of(jnp.maximum(i * qb - WINDOW, 0), qb)
    rowi = lax.broadcasted_iota(jnp.int32, (LANES, nq), 0)
    q_win = jnp.where(rowi < DK, qt, jnp.zeros_like(qt))
    sw = _dot(kw_ref[pl.ds(w0, wk), :], q_win).astype(BF16) + bw_ref[0, 0]
    pw = jnp.exp(sw - jnp.max(sw, axis=0, keepdims=True))
    acc_w = _dot(vw_ref[0, :V_ROWS, pl.ds(w0, wk)], pw)

    pad_rows = jnp.zeros((LANES - V_ROWS, qb), F32)
    for b in range(ATT_NQ):
        rows = slice(b * qb, (b + 1) * qb)
        gt = _sigmoid(gl_ref[rows, :] + gb_ref[...])
        outs = []
        for g in range(GQA):
            cs = slice(b * hq + g * qb, b * hq + (g + 1) * qb)
            a_w = jnp.concatenate([acc_w[:, cs], pad_rows], axis=0).T
            a_s = jnp.concatenate([acc_s[:, cs], pad_rows], axis=0).T
            o_w = a_w[:, :DK] / a_w[:, DK:DK + 1]
            o_s = a_s[:, :DK] / a_s[:, DK:DK + 1]
            outs.append(gt[:, GQA + g:GQA + g + 1] * o_s + gt[:, 2 * GQA + g:2 * GQA + g + 1] * o_w)
        o = ocg_ref[rows, :] + jnp.concatenate(outs, axis=1)
        o_ref[rows, :] = (o * _silu(gc_ref[rows, :])).astype(o_ref.dtype)


def _rel_bucket(dist):
    n = jnp.maximum(dist, 0)
    exact = REL_BUCKETS // 2
    nf = jnp.maximum(n, 1).astype(jnp.float32)
    large = exact + (jnp.log(nf / exact) / math.log(REL_MAX_DIST / exact)
                     * (REL_BUCKETS - exact)).astype(jnp.int32)
    return jnp.where(n < exact, n, jnp.minimum(large, REL_BUCKETS - 1))


def nsa_bias_tables(rel_bias, seq):
    qb = Q_BLOCK
    n_pad = seq // CMP_STRIDE
    c0 = n_pad - NA
    wc = 2 * n_pad
    ws = SEL_FAR * qb + SEL_TK
    nmax = max(seq, SEL_FAR * qb + qb)
    padl = CMP_STRIDE * (wc + 2)
    bvec = rel_bias[_rel_bucket(jnp.arange(nmax))].T.astype(F32)
    vext = jnp.concatenate([jnp.full((NSA_HEADS, padl), NEG, F32), bvec], axis=1)
    pos = jnp.arange(padl + nmax) - padl
    vwin = jnp.where(pos < WINDOW, vext, NEG)

    mlo, mhi = c0 - wc + 1, NA - 1 + c0
    start = padl + CMP_STRIDE * mlo - (CMP_BLOCK - 1)
    assert start >= 0
    u = vext[:, start:start + CMP_STRIDE * (mhi - mlo + 1)].reshape(NSA_HEADS, mhi - mlo + 1, CMP_STRIDE)
    urev = u[:, ::-1, :]
    gcb = jnp.stack([urev[:, NA - 1 - a:NA - 1 - a + wc, :] for a in range(NA)], axis=1)
    gcb = gcb.transpose(0, 1, 3, 2).reshape(HKV, GQA * qb, wc)
    nqb = seq // qb
    bc = jnp.stack([gcb[:, :, c0 - NA * i:c0 - NA * i + n_pad] for i in range(nqb)], axis=1)

    def toeplitz(vec, d0, width):
        vrev = vec[:, ::-1]
        s0 = nmax - d0 - qb
        period = width + qb - 1
        assert s0 >= 0 and s0 + period <= padl + nmax
        sl = vrev[:, s0:s0 + period]
        y = jnp.concatenate([sl[:, qb - 1:], sl[:, :qb - 1]], axis=1)
        z = jnp.tile(y, (1, qb))[:, :qb * (period - 1)].reshape(NSA_HEADS, qb, period - 1)
        return z[:, :, :width].reshape(HKV, GQA * qb, width)

    gs = toeplitz(vext, SEL_FAR * qb, ws)
    bw = tuple(tuple(toeplitz(vwin, first + b * qb, WINDOW + ATT_NQ * qb) for b in range(ATT_NQ))
               for first in (0, WINDOW))
    return bc, gs, bw


def _overlap_matrix_t(n_pad):
    c_start = np.arange(n_pad)[None, :] * CMP_STRIDE
    s_start = np.arange(MASK_ROWS)[:, None] * SLC_BLOCK
    ov = np.clip(np.minimum(c_start + CMP_BLOCK, s_start + SLC_BLOCK) - np.maximum(c_start, s_start), 0, None)
    return ov.astype(np.float32) / CMP_BLOCK


def nsa_branch(h1, h2, kv, hgl, gate_b4, kct, vc, tables, bsz, seq):
    qb = Q_BLOCK
    nqb = seq // qb
    n_pad = seq // CMP_STRIDE
    n_slc = seq // SLC_BLOCK
    assert n_slc <= MASK_ROWS and seq % SEL_TK == 0 and nqb % SEL_SUB == 0
    assert (SEL_TK // qb) % ATT_NQ == 0 and nqb % ATT_NQ == 0
    bc, gs, bw = tables
    m = bsz * seq
    gw = GQA * DK

    sq = SEL_SUB * qb
    nsq = seq // sq
    ovt = jnp.asarray(_overlap_matrix_t(n_pad), BF16)
    qsel, ocg = pl.pallas_call(
        _nsa_select_kernel,
        grid=(HKV, bsz, nsq),
        in_specs=[
            pl.BlockSpec((sq, gw), lambda hh, b, i: (b * nsq + i, C1_QC // gw + hh)),
            pl.BlockSpec((sq, LANES), lambda hh, b, i: (b * nsq + i, hh)),
            pl.BlockSpec((1, LANES), lambda hh, b, i: (0, hh)),
            pl.BlockSpec((1, DK, n_pad), lambda hh, b, i: (b * HKV + hh, 0, 0)),
            pl.BlockSpec((1, n_pad, DK), lambda hh, b, i: (b * HKV + hh, 0, 0)),
            pl.BlockSpec((1, SEL_SUB) + bc.shape[2:], lambda hh, b, i: (hh, i, 0, 0)),
            pl.BlockSpec(ovt.shape, lambda hh, b, i: (0, 0)),
        ],
        out_specs=[pl.BlockSpec((1, SEL_SUB, LANES, GQA * qb), lambda hh, b, i: (hh, b * nsq + i, 0, 0)),
                   pl.BlockSpec((sq, gw), lambda hh, b, i: (b * nsq + i, hh))],
        out_shape=[jax.ShapeDtypeStruct((HKV, bsz * nqb, LANES, GQA * qb), BF16),
                   jax.ShapeDtypeStruct((m, D_BR), F32)],
        scratch_shapes=[pltpu.VMEM((SEL_SUB, MASK_ROWS, qb), F32)],
        compiler_params=_params(),
        name="nsa_select",
    )(h1, hgl, gate_b4, kct, vc, bc, ovt)

    gs_t = gs.transpose(0, 2, 1).astype(BF16)
    bw_t = jnp.stack([jnp.concatenate([w.transpose(0, 2, 1) for w in variant], axis=2) for variant in bw],
                     axis=1).astype(BF16)
    ks_ext, kw_ext, vs_ext, vw_ext = kv

    aq = ATT_NQ * qb
    npair = nqb // ATT_NQ
    rowblk = lambda width, c0: pl.BlockSpec((aq, width), lambda hh, b, i: (b * npair + i, c0 // width + hh))
    keys = pl.BlockSpec((seq, LANES), lambda hh, b, i: (b, hh))
    vals = pl.BlockSpec((1, LANES, seq), lambda hh, b, i: (hh, 0, b))
    return pl.pallas_call(
        _nsa_attn_kernel,
        grid=(HKV, bsz, npair),
        in_specs=[
            pl.BlockSpec((1, ATT_NQ, LANES, GQA * qb), lambda hh, b, i: (hh, b * npair + i, 0, 0)),
            rowblk(gw, 0), rowblk(gw, C2_GC),
            pl.BlockSpec((aq, LANES), lambda hh, b, i: (b * npair + i, hh)),
            pl.BlockSpec((1, LANES), lambda hh, b, i: (0, hh)),
            keys, vals, pl.BlockSpec((1,) + gs_t.shape[1:], lambda hh, b, i: (hh, 0, 0)),
            keys, vals, pl.BlockSpec((1, 1) + bw_t.shape[2:], lambda hh, b, i: (hh, jnp.minimum(i, 1), 0, 0)),
        ],
        out_specs=pl.BlockSpec((aq, gw), lambda hh, b, i: (b * npair + i, hh)),
        out_shape=jax.ShapeDtypeStruct((m, D_BR), BF16),
        scratch_shapes=[pltpu.VMEM((SEL_TK, ATT_NQ * GQA * qb), BF16)] * 2,
        compiler_params=_params(),
        name="nsa_attn",
    )(qsel, ocg, h2, hgl, gate_b4, ks_ext, vs_ext, gs_t, kw_ext, vw_ext, bw_t)


def _final_kernel(oa_ref, ob_ref, oc_ref, om_ref, *rest):
    gate_refs = rest[:2 * N_BRANCHES]
    x_ref, wb_ref, wo_ref, lng_ref, lnb_ref, y_ref, yb_ref = rest[2 * N_BRANCHES:]
    o_refs = (oa_ref, ob_ref, oc_ref, om_ref)
    halves = []
    for c in range(2):
        cs = slice(c * D_BR, (c + 1) * D_BR)
        acc = None
        for k in range(N_BRANCHES):
            term = _sigmoid(gate_refs[2 * k + c][...]) * _dot(o_refs[k][...], wb_ref[k, :, cs])
            acc = term if acc is None else acc + term
        halves.append(acc.astype(BF16))
    merged = jnp.concatenate(halves, axis=1)
    z = DN_ALPHA * x_ref[...] + _dot(merged, wo_ref[...])
    mu = jnp.mean(z, axis=-1, keepdims=True)
    zc = z - mu
    var = jnp.mean(zc * zc, axis=-1, keepdims=True)
    y = zc * lax.rsqrt(var + LN_EPS) * lng_ref[...] + lnb_ref[...]
    y_ref[...] = y
    yb_ref[...] = y.astype(BF16)


def final_merge(o_a, o_b, o_c, o_m, h2, x, w_branch, w_out, ln_g, ln_b, tm=128):
    m = x.shape[0]
    tm = min(tm, m)
    br = pl.BlockSpec((tm, D_BR), lambda i: (i, 0))
    gate = lambda k: pl.BlockSpec((tm, D_BR), lambda i, k=k: (i, C2_MERGE // D_BR + k))
    xs = pl.BlockSpec((tm, D_MODEL), lambda i: (i, 0))
    resident = lambda a: pl.BlockSpec(a.shape, lambda i: (0,) * a.ndim, pipeline_mode=pl.Buffered(1))
    wb = w_branch.astype(BF16)
    wo = w_out.astype(BF16)
    lg, lb = ln_g.reshape(1, D_MODEL), ln_b.reshape(1, D_MODEL)
    ngate = 2 * N_BRANCHES
    return pl.pallas_call(
        _final_kernel,
        grid=(m // tm,),
        in_specs=[br, br, br, br] + [gate(k) for k in range(ngate)] + [xs]
                 + [resident(wb), resident(wo), resident(lg), resident(lb)],
        out_specs=[xs, xs],
        out_shape=[jax.ShapeDtypeStruct((m, D_MODEL), F32), jax.ShapeDtypeStruct((m, D_MODEL), BF16)],
        compiler_params=_params(),
        name="merge_out_ln",
    )(o_a, o_b, o_c, o_m, *([h2] * ngate), x, wb, wo, lg, lb)


def _gate_spread_matrix():
    p = np.zeros((3 * NSA_HEADS, HKV * LANES), np.float32)
    for hh in range(HKV):
        for brn in range(3):
            for g in range(GQA):
                p[brn * NSA_HEADS + hh * GQA + g, hh * LANES + brn * GQA + g] = 1.0
    return p


def _spread_gate_cols(a):
    return jnp.dot(a, jnp.asarray(_gate_spread_matrix()), precision=lax.Precision.HIGHEST)


def layer(l, x, xb, mem_b, tables, bsz, seq, w_in, sgu_ln_g, sgu_ln_b, sgu_w, sgu_b, conv_w, conv_b,
          lru_wa, lru_ba, lru_wx, lru_bx, lru_lambda, cmp_pe_k, cmp_pe_v, cmp_w1_k, cmp_w1_v,
          cmp_w2_k, cmp_w2_v, nsa_gate_b, w_mem_kv, w_branch, w_out, ln_g, ln_b):
    m = bsz * seq
    tm = min(1024, m)
    h1 = matmul_f32w(xb, w_in, l, 0, N1, F32, tm, 512, w_is_nk=True)
    kv = kv_projection(xb, w_in, l, N1, seq, tm)
    h2 = matmul_f32w(xb, w_in, l, GL_OFF + 3 * NSA_HEADS, N2, F32, tm, 512, w_is_nk=True)
    w_gl = _spread_gate_cols(w_in[l, GL_OFF:GL_OFF + 3 * NSA_HEADS, :].T).astype(BF16)
    hgl = matmul(xb, w_gl, F32, tm, HKV * LANES)
    gate_b4 = _spread_gate_cols(nsa_gate_b[l].reshape(1, 3 * NSA_HEADS))

    o_a = gmlp_branch(h1, sgu_ln_g[l], sgu_ln_b[l], sgu_w[l], sgu_b[l])
    o_b = lru_branch(h1, bsz, seq, conv_w[l], conv_b[l], lru_wa[l], lru_ba[l], lru_wx[l], lru_bx[l],
                     lru_lambda[l])

    kct, vc = nsa_compress(h1, bsz, seq, cmp_pe_k[l], cmp_pe_v[l],
                           cmp_w1_k[l], cmp_w1_v[l], cmp_w2_k[l], cmp_w2_v[l])
    o_c = nsa_branch(h1, h2, kv, hgl, gate_b4, kct, vc, tables, bsz, seq)

    mrows = mem_b.shape[0]
    kv = matmul_f32w(mem_b, w_mem_kv, l, 0, 2 * D_BR, BF16, min(512, mrows), 512)
    o_m = mem_branch(h2, kv, bsz, seq)
    return final_merge(o_a, o_b, o_c, o_m, h2, x, w_branch[l], w_out[l], ln_g[l], ln_b[l])


def kernel(x, mem, rel_bias, w_in, sgu_ln_g, sgu_ln_b, sgu_w, sgu_b, conv_w, conv_b, lru_wa, lru_ba, lru_wx,
           lru_bx, lru_lambda, cmp_pe_k, cmp_pe_v, cmp_w1_k, cmp_w1_v, cmp_w2_k, cmp_w2_v, nsa_gate_b,
           w_mem_kv, w_branch, w_out, ln_g, ln_b):
    bsz, seq, _ = x.shape
    tables = nsa_bias_tables(rel_bias, seq)
    xf = x.reshape(bsz * seq, D_MODEL)
    xb = xf.astype(BF16)
    mem_b = mem.reshape(-1, D_MODEL).astype(BF16)
    w_in_t = jnp.swapaxes(w_in, 1, 2)
    params = (w_in_t, sgu_ln_g, sgu_ln_b, sgu_w, sgu_b, conv_w, conv_b, lru_wa, lru_ba, lru_wx, lru_bx,
              lru_lambda, cmp_pe_k, cmp_pe_v, cmp_w1_k, cmp_w1_v, cmp_w2_k, cmp_w2_v, nsa_gate_b,
              w_mem_kv, w_branch, w_out, ln_g, ln_b)
    for l in range(w_in.shape[0]):
        xf, xb = layer(l, xf, xb, mem_b, tables, bsz, seq, *params)
    return xf.reshape(bsz, seq, D_MODEL)
```

```python
import functools
import math

import numpy as np
import jax
import jax.numpy as jnp
from jax import lax
from jax.experimental import pallas as pl
from jax.experimental.pallas import tpu as pltpu

F32 = jnp.float32
BF16 = jnp.bfloat16

D_MODEL = 2048
DEPTH = 2
D_BR = D_MODEL // 2
N_BRANCHES = 4
GMLP_CHUNK = 128
GMLP_GROUPS = 8
LRU_BLOCKS = 8
LRU_BLOCK_DIM = D_BR // LRU_BLOCKS
CONV_WIDTH = 4
LRU_C = 8.0
DK = 64
NSA_HEADS = D_BR // DK
HKV = NSA_HEADS // 4
GQA = NSA_HEADS // HKV
KV_W = HKV * DK
CMP_BLOCK = 32
CMP_STRIDE = 16
CMP_HIDDEN = 256
SLC_BLOCK = 64
SLC_TOPK = 8
WINDOW = 256
Q_BLOCK = 128
MEM_HEADS = 4
MEM_HEAD_DIM = D_BR // MEM_HEADS
REL_BUCKETS = 32
REL_MAX_DIST = 1024
DN_ALPHA = (2 * DEPTH) ** 0.25
LN_EPS = 1e-5

LANES = 128
SUBLANES = 8
NEG = -1e30
SEL_TK = 512
SEL_FAR = 11
MASK_ROWS = LANES - DK
NA = Q_BLOCK // CMP_STRIDE
SEL_SUB = 4
ATT_NQ = 2
V_ROWS = DK + 16
VMEM_LIMIT = 56 * 1024 * 1024

C1_U, C1_V, C1_GA, C1_XB, C1_GB, C1_QC = (D_BR * k for k in range(6))
C1_KV = 6 * D_BR
N1 = C1_KV + 2 * KV_W
NKV = 4 * KV_W
C2_GC, C2_QM, C2_GM, C2_MERGE = (D_BR * k for k in range(4))
N2 = C2_MERGE + N_BRANCHES * D_MODEL
GL_OFF = N1 + NKV


def _sigmoid(x):
    return 1.0 / (1.0 + jnp.exp(-x))


def _silu(x):
    return x * _sigmoid(x)


def _gelu_tanh(x):
    return 0.5 * x * (1.0 + jnp.tanh(math.sqrt(2.0 / math.pi) * (x + 0.044715 * (x * x * x))))


def _dot(a, b):
    return jnp.dot(a, b, preferred_element_type=F32)


def _dot_nt(a, b):
    return lax.dot_general(a, b, (((1,), (1,)), ((), ())), preferred_element_type=F32)


def _params(**kw):
    return pltpu.CompilerParams(vmem_limit_bytes=VMEM_LIMIT, **kw)


def _mm_kernel(x_ref, w_ref, o_ref):
    o_ref[...] = _dot(x_ref[...], w_ref[...]).astype(o_ref.dtype)


def matmul(x, w, out_dtype, tm, tn):
    m, k = x.shape
    n = w.shape[1]
    assert m % tm == 0 and n % tn == 0, (m, n, tm, tn)
    return pl.pallas_call(
        _mm_kernel,
        grid=(n // tn, m // tm),
        in_specs=[pl.BlockSpec((tm, k), lambda j, i: (i, 0)),
                  pl.BlockSpec((k, tn), lambda j, i: (0, j))],
        out_specs=pl.BlockSpec((tm, tn), lambda j, i: (i, j)),
        out_shape=jax.ShapeDtypeStruct((m, n), out_dtype),
        compiler_params=_params(),
        name="matmul",
    )(x, w)


def _mm_castw_kernel(x_ref, *rest, w_is_nk, shift):
    w_refs, (o_ref, wb_ref) = rest[:-2], rest[-2:]

    @pl.when(pl.program_id(1) == 0)
    def _():
        if shift:
            w = jnp.concatenate([w_refs[0][0, shift:, :], w_refs[1][0, :shift, :]], axis=0)
        else:
            w = w_refs[0][0]
        wb_ref[...] = (w.T if w_is_nk else w).astype(BF16)

    o_ref[...] = _dot(x_ref[...], wb_ref[...]).astype(o_ref.dtype)


def matmul_f32w(x, w3, layer_idx, col0, n, out_dtype, tm, tn, w_is_nk=False):
    m, k = x.shape
    ncols = w3.shape[1] if w_is_nk else w3.shape[2]
    shift = col0 % tn
    assert m % tm == 0 and n % tn == 0 and col0 + n <= ncols, (m, n, col0, tm, tn)
    assert shift == 0 or (w_is_nk and shift % SUBLANES == 0), (col0, tn)
    j0 = col0 // tn
    if w_is_nk:
        w_specs = [pl.BlockSpec((1, tn, k), lambda j, i, d=d: (layer_idx, j0 + j + d, 0))
                   for d in range(2 if shift else 1)]
    else:
        w_specs = [pl.BlockSpec((1, k, tn), lambda j, i: (layer_idx, 0, j0 + j))]
    return pl.pallas_call(
        functools.partial(_mm_castw_kernel, w_is_nk=w_is_nk, shift=shift),
        grid=(n // tn, m // tm),
        in_specs=[pl.BlockSpec((tm, k), lambda j, i: (i, 0))] + w_specs,
        out_specs=pl.BlockSpec((tm, tn), lambda j, i: (i, j)),
        out_shape=jax.ShapeDtypeStruct((m, n), out_dtype),
        scratch_shapes=[pltpu.VMEM((k, tn), BF16)],
        compiler_params=_params(dimension_semantics=("arbitrary", "arbitrary")),
        name="matmul_f32w",
    )(x, *([w3] * len(w_specs)))


def _kv_proj_kernel(x_ref, ws_ref, ww_ref, ks_ref, kw_ref, vs_ref, vw_ref, wk_s, wv_s, *, tm, seq):
    @pl.when(pl.program_id(0) == 0)
    def _():
        wk = jnp.concatenate([ws_ref[0, 0:KV_W], ww_ref[0, 0:KV_W]], axis=0)
        wk_s[...] = wk.T.astype(BF16)
        wv_s[...] = jnp.concatenate([ws_ref[0, KV_W:2 * KV_W], ww_ref[0, KV_W:2 * KV_W]], axis=0).astype(BF16)

    x = x_ref[...]
    kk = _dot(x, wk_s[...])
    vt = _dot_nt(wv_s[...], x)
    pos = (pl.program_id(0) * tm + lax.broadcasted_iota(jnp.int32, (tm, MASK_ROWS), 0)) % seq
    blk = lax.broadcasted_iota(jnp.int32, (tm, MASK_ROWS), 1)
    onehot = jnp.where((pos >> int(math.log2(SLC_BLOCK))) == blk, 1.0, 0.0)
    zeros = jnp.zeros((tm, MASK_ROWS), F32)
    tail = jnp.where(lax.broadcasted_iota(jnp.int32, (LANES - DK, tm), 0) == 0, 1.0, 0.0)
    for h in range(HKV):
        ks_ref[:, h * LANES:(h + 1) * LANES] = jnp.concatenate(
            [kk[:, h * DK:(h + 1) * DK], onehot], axis=1).astype(ks_ref.dtype)
        kw_ref[:, h * LANES:(h + 1) * LANES] = jnp.concatenate(
            [kk[:, KV_W + h * DK:KV_W + (h + 1) * DK], zeros], axis=1).astype(kw_ref.dtype)
        vs_ref[h] = jnp.concatenate([vt[h * DK:(h + 1) * DK], tail], axis=0).astype(vs_ref.dtype)
        vw_ref[h] = jnp.concatenate([vt[KV_W + h * DK:KV_W + (h + 1) * DK], tail], axis=0).astype(vw_ref.dtype)


def kv_projection(x, wt3, layer_idx, row0, seq, tm):
    m, k = x.shape
    nrows = 2 * KV_W
    assert m % tm == 0 and row0 % nrows == 0 and seq % tm == 0
    wspec = lambda d: pl.BlockSpec((1, nrows, k), lambda i: (layer_idx, row0 // nrows + d, 0),
                                   pipeline_mode=pl.Buffered(1))
    keys = jax.ShapeDtypeStruct((m, HKV * LANES), BF16)
    vals = jax.ShapeDtypeStruct((HKV, LANES, m), BF16)
    return pl.pallas_call(
        functools.partial(_kv_proj_kernel, tm=tm, seq=seq),
        grid=(m // tm,),
        in_specs=[pl.BlockSpec((tm, k), lambda i: (i, 0)), wspec(0), wspec(1)],
        out_specs=[pl.BlockSpec((tm, HKV * LANES), lambda i: (i, 0))] * 2
                  + [pl.BlockSpec((HKV, LANES, tm), lambda i: (0, 0, i))] * 2,
        out_shape=[keys, keys, vals, vals],
        scratch_shapes=[pltpu.VMEM((k, 2 * KV_W), BF16), pltpu.VMEM((2 * KV_W, k), BF16)],
        compiler_params=_params(dimension_semantics=("arbitrary",)),
        name="kv_projection",
    )(x, wt3, wt3)


def _gmlp_kernel(u_ref, v_ref, ga_ref, lng_ref, lnb_ref, w_ref, bs_ref, o_ref, *, rows):
    gd = D_BR // GMLP_GROUPS
    for c in range(rows // GMLP_CHUNK):
        r = slice(c * GMLP_CHUNK, (c + 1) * GMLP_CHUNK)
        v = _gelu_tanh(v_ref[r, :])
        mu = jnp.mean(v, axis=-1, keepdims=True)
        vc = v - mu
        var = jnp.mean(vc * vc, axis=-1, keepdims=True)
        vb = (vc * lax.rsqrt(var + LN_EPS) * lng_ref[...] + lnb_ref[...]).astype(BF16)
        u = _gelu_tanh(u_ref[r, :]) * _silu(ga_ref[r, :])
        for g in range(GMLP_GROUPS):
            cs = slice(g * gd, (g + 1) * gd)
            mixed = _dot(w_ref[g], vb[:, cs]) + bs_ref[:, g:g + 1]
            o_ref[r, cs] = (u[:, cs] * mixed).astype(o_ref.dtype)


def gmlp_branch(h1, ln_g, ln_b, w_s, b_s, rows=512):
    m = h1.shape[0]
    rows = min(rows, m)
    causal = jnp.tril(jnp.ones((GMLP_CHUNK, GMLP_CHUNK), dtype=bool))
    w = jnp.where(causal, w_s, 0).astype(BF16)
    col = lambda c: pl.BlockSpec((rows, D_BR), lambda i, c=c: (i, c // D_BR))
    full = lambda a: pl.BlockSpec(a.shape, lambda i: (0,) * a.ndim)
    args = (ln_g.reshape(1, D_BR), ln_b.reshape(1, D_BR), w, b_s.T)
    return pl.pallas_call(
        functools.partial(_gmlp_kernel, rows=rows),
        grid=(m // rows,),
        in_specs=[col(C1_U), col(C1_V), col(C1_GA)] + [full(a) for a in args],
        out_specs=pl.BlockSpec((rows, D_BR), lambda i: (i, 0)),
        out_shape=jax.ShapeDtypeStruct((m, D_BR), BF16),
        compiler_params=_params(),
        name="gmlp",
    )(h1, h1, h1, *args)


def _lru_kernel(xb_ref, gb_ref, cw_ref, cb_ref, wa_ref, ba_ref, wx_ref, bx_ref, lam_ref, o_ref,
                xbuf, hcarry, a_s, g_s, *, ts):
    @pl.when(pl.program_id(1) == 0)
    def _():
        xbuf[0:8, :] = jnp.zeros((8, D_BR), F32)
        hcarry[...] = jnp.zeros((8, D_BR), F32)

    xbuf[8:8 + ts, :] = xb_ref[...]
    xc = cb_ref[...] + cw_ref[0:1, :] * xbuf[pl.ds(8 - (CONV_WIDTH - 1), ts), :]
    for k in range(1, CONV_WIDTH):
        xc = xc + cw_ref[k:k + 1, :] * xbuf[pl.ds(8 - (CONV_WIDTH - 1) + k, ts), :]
    xbuf[0:8, :] = xbuf[ts:ts + 8, :]

    xcb = xc.astype(BF16)
    bd = LRU_BLOCK_DIM
    r = jnp.concatenate([_dot(xcb[:, n * bd:(n + 1) * bd], wa_ref[n]) for n in range(LRU_BLOCKS)], axis=1)
    i = jnp.concatenate([_dot(xcb[:, n * bd:(n + 1) * bd], wx_ref[n]) for n in range(LRU_BLOCKS)], axis=1)
    r = _sigmoid(r + ba_ref[...])
    i = _sigmoid(i + bx_ref[...])
    nl = -lam_ref[...]
    softplus = jnp.maximum(nl, 0.0) + jnp.log1p(jnp.exp(-jnp.abs(nl)))
    log_a = (-LRU_C * softplus) * r
    a = jnp.exp(log_a)
    a_s[...] = a
    g_s[...] = jnp.sqrt(1.0 - a * a) * i * xc

    row = lax.broadcasted_iota(jnp.int32, (8, D_BR), 0)

    def body(j, carry):
        r0 = pl.multiple_of(j * 8, 8)
        av = a_s[pl.ds(r0, 8), :]
        bv = g_s[pl.ds(r0, 8), :]
        for d in (1, 2, 4):
            keep = row >= d
            a_sh = pltpu.roll(av, d, axis=0)
            b_sh = pltpu.roll(bv, d, axis=0)
            bv = jnp.where(keep, av * b_sh + bv, bv)
            av = jnp.where(keep, av * a_sh, av)
        hv = av * carry + bv
        g_s[pl.ds(r0, 8), :] = hv
        return jnp.broadcast_to(hv[7:8, :], (8, D_BR))

    hcarry[...] = lax.fori_loop(0, ts // 8, body, hcarry[...])
    o_ref[...] = (g_s[...] * _silu(gb_ref[...])).astype(o_ref.dtype)


def lru_branch(h1, bsz, seq, conv_w, conv_b, wa, ba, wx, bx, lam, ts=512):
    ts = min(ts, seq)
    ns = seq // ts
    col = lambda c: pl.BlockSpec((ts, D_BR), lambda b, s, c=c: (b * ns + s, c // D_BR))
    full = lambda a: pl.BlockSpec(a.shape, lambda b, s: (0,) * a.ndim)
    row = lambda a: a.reshape(1, D_BR)
    args = (conv_w, row(conv_b), wa.astype(BF16), row(ba), wx.astype(BF16), row(bx), row(lam))
    return pl.pallas_call(
        functools.partial(_lru_kernel, ts=ts),
        grid=(bsz, ns),
        in_specs=[col(C1_XB), col(C1_GB)] + [full(a) for a in args],
        out_specs=pl.BlockSpec((ts, D_BR), lambda b, s: (b * ns + s, 0)),
        out_shape=jax.ShapeDtypeStruct((bsz * seq, D_BR), BF16),
        scratch_shapes=[pltpu.VMEM((ts + 8, D_BR), F32), pltpu.VMEM((8, D_BR), F32),
                        pltpu.VMEM((ts, D_BR), F32), pltpu.VMEM((ts, D_BR), F32)],
        compiler_params=_params(dimension_semantics=("arbitrary", "arbitrary")),
        name="lru",
    )(h1, h1, *args)


def _mem_kernel(q_ref, g_ref, kv_ref, o_ref):
    hd = MEM_HEAD_DIM
    for hh in range(MEM_HEADS):
        cs = slice(hh * hd, (hh + 1) * hd)
        q = (q_ref[:, cs] * (hd ** -0.5)).astype(BF16)
        s = _dot_nt(q, kv_ref[:, cs])
        p = jnp.exp(s - jnp.max(s, axis=-1, keepdims=True))
        l = jnp.sum(p, axis=-1, keepdims=True)
        o = _dot(p.astype(BF16), kv_ref[:, D_BR + hh * hd:D_BR + (hh + 1) * hd]) / l
        o_ref[:, cs] = (o * _silu(g_ref[:, cs])).astype(o_ref.dtype)


def mem_branch(h2, kv, bsz, seq, tq=512):
    tq = min(tq, seq)
    nq = seq // tq
    mlen = kv.shape[0] // bsz
    col = lambda c: pl.BlockSpec((tq, D_BR), lambda b, i, c=c: (b * nq + i, c // D_BR))
    return pl.pallas_call(
        _mem_kernel,
        grid=(bsz, nq),
        in_specs=[col(C2_QM), col(C2_GM), pl.BlockSpec((mlen, 2 * D_BR), lambda b, i: (b, 0))],
        out_specs=pl.BlockSpec((tq, D_BR), lambda b, i: (b * nq + i, 0)),
        out_shape=jax.ShapeDtypeStruct((bsz * seq, D_BR), BF16),
        compiler_params=_params(),
        name="mem_attn",
    )(h2, h2, kv)


def _cmp_kernel(*refs):
    ngrp = 2 * KV_W // LANES
    t_refs = refs[:ngrp]
    pek_ref, pev_ref, w1k_ref, w1v_ref, w2kt_ref, w2v_ref, kct_ref, vc_ref = refs[ngrp:]
    n = t_refs[0].shape[0] // CMP_STRIDE
    for h in range(HKV):
        for is_v, (pe_ref, w1_ref) in enumerate(((pek_ref, w1k_ref), (pev_ref, w1v_ref))):
            grp, off = divmod(is_v * KV_W + h * DK, LANES)
            lo = hi = None
            for tok in range(CMP_STRIDE):
                x = t_refs[grp][pl.ds(tok, n, stride=CMP_STRIDE), :][:, off:off + DK]
                t2 = CMP_STRIDE + tok
                a = _dot((x + pe_ref[tok:tok + 1, :]).astype(BF16), w1_ref[tok * DK:(tok + 1) * DK, :])
                b = _dot((x + pe_ref[t2:t2 + 1, :]).astype(BF16), w1_ref[t2 * DK:(t2 + 1) * DK, :])
                lo = a if lo is None else lo + a
                hi = b if hi is None else hi + b
            hidden = _silu(lo + pltpu.roll(hi, n - 1, axis=0)).astype(BF16)
            if is_v:
                vc_ref[h] = _dot(hidden, w2v_ref[...]).astype(vc_ref.dtype)
            else:
                kct_ref[h] = _dot_nt(w2kt_ref[...], hidden).astype(kct_ref.dtype)


def nsa_compress(h1, bsz, seq, pe_k, pe_v, w1_k, w1_v, w2_k, w2_v):
    n = seq // CMP_STRIDE
    args = (pe_k, pe_v, w1_k.astype(BF16), w1_v.astype(BF16), w2_k.T.astype(BF16), w2_v.astype(BF16))
    full = lambda a: pl.BlockSpec(a.shape, lambda b: (0,) * a.ndim)
    return pl.pallas_call(
        _cmp_kernel,
        grid=(bsz,),
        in_specs=[pl.BlockSpec((seq, LANES), lambda b, g=g: (b, C1_KV // LANES + g))
                  for g in range(2 * KV_W // LANES)] + [full(a) for a in args],
        out_specs=[pl.BlockSpec((HKV, DK, n), lambda b: (b, 0, 0)), pl.BlockSpec((HKV, n, DK), lambda b: (b, 0, 0))],
        out_shape=[jax.ShapeDtypeStruct((bsz * HKV, DK, n), BF16), jax.ShapeDtypeStruct((bsz * HKV, n, DK), BF16)],
        compiler_params=_params(),
        name="nsa_compress",
    )(*([h1] * (2 * KV_W // LANES)), *args)


def _nsa_select_kernel(q_ref, gl_ref, gb_ref, kct_ref, vc_ref, bc_ref, ovt_ref, qsel_ref, ocg_ref, vt_s):
    qb = Q_BLOCK
    lane = lax.broadcasted_iota(jnp.int32, (qb, LANES), 1)
    jrow = lax.broadcasted_iota(jnp.int32, (MASK_ROWS, qb), 0).astype(F32)
    qlane = lax.broadcasted_iota(jnp.int32, (MASK_ROWS, qb), 1)
    sub = lax.broadcasted_iota(jnp.int32, (SUBLANES, qb), 0)
    nv = MASK_ROWS // SUBLANES
    ovt = ovt_ref[...]
    kct = kct_ref[0]
    vc = vc_ref[0]
    for sb in range(SEL_SUB):
        rows = slice(sb * qb, (sb + 1) * qb)
        blk = pl.program_id(2) * SEL_SUB + sb
        q = q_ref[rows, :] * (DK ** -0.5)
        gt = _sigmoid(gl_ref[rows, :] + gb_ref[...])
        qpad = []
        for g in range(GQA):
            t = q[:, (g // 2) * LANES:(g // 2 + 1) * LANES]
            if g % 2:
                t = pltpu.roll(t, DK, axis=1)
            qpad.append(jnp.where(lane < DK, t, 0.0))

        has_keys = blk * qb + lax.broadcasted_iota(jnp.int32, (qb, 1), 0) >= CMP_BLOCK - 1
        o_c = []
        psum = None
        for g in range(GQA):
            s = _dot(qpad[g][:, :DK].astype(BF16), kct) + bc_ref[0, sb, g * qb:(g + 1) * qb, :]
            m = jnp.max(s, axis=-1, keepdims=True)
            p = jnp.exp(s - m)
            p = p * jnp.where(has_keys, 1.0 / jnp.maximum(jnp.sum(p, axis=-1, keepdims=True), 1e-30), 0.0)
            o_c.append(gt[:, g:g + 1] * _dot(p.astype(BF16), vc))
            psum = p if psum is None else psum + p
        ocg_ref[rows, :] = jnp.concatenate(o_c, axis=1)

        p_hi = psum.astype(BF16)
        rem = psum - p_hi.astype(F32)
        p_mid = rem.astype(BF16)
        p_lo = (rem - p_mid.astype(F32)).astype(BF16)
        imp_t = _dot_nt(ovt, p_hi) + _dot_nt(ovt, p_mid) + _dot_nt(ovt, p_lo)

        qblk = ((blk * qb + qlane) >> int(math.log2(SLC_BLOCK))).astype(F32)
        val = jnp.where(jrow == 0.0, 3e38,
                        jnp.where(jrow == qblk, 3e38, jnp.where(jrow == qblk - 1.0, 3e38, imp_t)))
        val = jnp.where(jrow > qblk, -1.0, val)
        vt_s[sb] = val
        vals = [val[k * SUBLANES:(k + 1) * SUBLANES] for k in range(nv)]
        cnt = [jnp.zeros((SUBLANES, qb), F32) for _ in range(nv)]
        for jp in range(MASK_ROWS):
            rowv = jnp.broadcast_to(vt_s[sb, jp:jp + 1, :], (SUBLANES, qb))
            for k in range(nv):
                if k * SUBLANES > jp:
                    beat = jnp.where(rowv >= vals[k], 1.0, 0.0)
                elif (k + 1) * SUBLANES - 1 < jp:
                    beat = jnp.where(rowv > vals[k], 1.0, 0.0)
                else:
                    beat = jnp.where(sub > jp - k * SUBLANES, jnp.where(rowv >= vals[k], 1.0, 0.0),
                                     jnp.where(rowv > vals[k], 1.0, 0.0))
                cnt[k] = cnt[k] + beat
        selneg_t = jnp.concatenate([jnp.where(c < float(SLC_TOPK), 0.0, NEG) for c in cnt], axis=0)
        for g in range(GQA):
            qt = jnp.concatenate([qpad[g].T[:DK], selneg_t], axis=0)
            qsel_ref[0, sb, :, g * qb:(g + 1) * qb] = qt.astype(qsel_ref.dtype)


def _nsa_attn_kernel(qsel_ref, ocg_ref, gc_ref, gl_ref, gb_ref, ks_ref, vs_ref, gs_ref, kw_ref, vw_ref, bw_ref,
                     o_ref, s_s, p_s):
    qb = Q_BLOCK
    hq = GQA * qb
    nq = ATT_NQ * hq
    i = pl.program_id(2) * ATT_NQ
    step = SEL_TK // qb
    n_tiles = (i * qb) // SEL_TK + 1
    qt = jnp.concatenate([qsel_ref[0, b] for b in range(ATT_NQ)], axis=1)

    def stage_qk(j):
        k0 = pl.multiple_of(j * SEL_TK, SEL_TK)
        sc = _dot(ks_ref[pl.ds(k0, SEL_TK), :], qt)
        for b in range(ATT_NQ):
            dd = jnp.minimum(i + b - j * step, SEL_FAR)
            boff = pl.multiple_of((SEL_FAR - dd) * qb, qb)
            s_s[:, b * hq:(b + 1) * hq] = sc[:, b * hq:(b + 1) * hq] + gs_ref[0, pl.ds(boff, SEL_TK), :]

    def stage_softmax(m):
        m_new, alpha = [], []
        for g in range(ATT_NQ * GQA):
            cs = slice(g * qb, (g + 1) * qb)
            sc = s_s[:, cs]
            mg = jnp.maximum(m[:, cs], jnp.max(sc, axis=0, keepdims=True))
            alpha.append(jnp.exp(m[:, cs] - mg))
            p_s[:, cs] = jnp.exp(sc - mg).astype(BF16)
            m_new.append(mg)
        return jnp.concatenate(m_new, axis=1), jnp.concatenate(alpha, axis=1)

    def stage_pv(j, acc):
        k0 = pl.multiple_of(jnp.maximum(j, 0) * SEL_TK, SEL_TK)
        return acc + _dot(vs_ref[0, :V_ROWS, pl.ds(k0, SEL_TK)], p_s[...])

    def advance(j, m, acc):
        acc = stage_pv(j - 2, acc)
        m, alpha = stage_softmax(m)
        return m, alpha * acc

    def body(j, carry):
        m, acc = advance(j, *carry)
        stage_qk(j)
        return m, acc

    p_s[...] = jnp.zeros(p_s.shape, BF16)
    stage_qk(jnp.int32(0))
    init = (jnp.full((1, nq), NEG, F32), jnp.zeros((V_ROWS, nq), F32))
    m, acc = lax.fori_loop(1, n_tiles, body, init)
    _, acc = advance(n_tiles, m, acc)
    acc_s = stage_pv(n_tiles - 1, acc)

    wk = WINDOW + ATT_NQ * qb
    w0 = pl.multiple_of(jnp.maximum(i * qb - WINDOW, 0), qb)
    rowi = lax.broadcasted_iota(jnp.int32, (LANES, nq), 0)
    q_win = jnp.where(rowi < DK, qt, jnp.zeros_like(qt))
    sw = _dot(kw_ref[pl.ds(w0, wk), :], q_win) + bw_ref[0, 0]
    pw = jnp.exp(sw - jnp.max(sw, axis=0, keepdims=True)).astype(BF16)
    acc_w = _dot(vw_ref[0, :V_ROWS, pl.ds(w0, wk)], pw)

    pad_rows = jnp.zeros((LANES - V_ROWS, qb), F32)
    for b in range(ATT_NQ):
        rows = slice(b * qb, (b + 1) * qb)
        gt = _sigmoid(gl_ref[rows, :] + gb_ref[...])
        outs = []
        for g in range(GQA):
            cs = slice(b * hq + g * qb, b * hq + (g + 1) * qb)
            a_w = jnp.concatenate([acc_w[:, cs], pad_rows], axis=0).T
            a_s = jnp.concatenate([acc_s[:, cs], pad_rows], axis=0).T
            o_w = a_w[:, :DK] / a_w[:, DK:DK + 1]
            o_s = a_s[:, :DK] / a_s[:, DK:DK + 1]
            outs.append(gt[:, GQA + g:GQA + g + 1] * o_s + gt[:, 2 * GQA + g:2 * GQA + g + 1] * o_w)
        o = ocg_ref[rows, :] + jnp.concatenate(outs, axis=1)
        o_ref[rows, :] = (o * _silu(gc_ref[rows, :])).astype(o_ref.dtype)


def _rel_bucket(dist):
    n = jnp.maximum(dist, 0)
    exact = REL_BUCKETS // 2
    nf = jnp.maximum(n, 1).astype(jnp.float32)
    large = exact + (jnp.log(nf / exact) / math.log(REL_MAX_DIST / exact)
                     * (REL_BUCKETS - exact)).astype(jnp.int32)
    return jnp.where(n < exact, n, jnp.minimum(large, REL_BUCKETS - 1))


def nsa_bias_tables(rel_bias, seq):
    qb = Q_BLOCK
    n_pad = seq // CMP_STRIDE
    c0 = n_pad - NA
    wc = 2 * n_pad
    ws = SEL_FAR * qb + SEL_TK
    nmax = max(seq, SEL_FAR * qb + qb)
    padl = CMP_STRIDE * (wc + 2)
    bvec = rel_bias[_rel_bucket(jnp.arange(nmax))].T.astype(F32)
    vext = jnp.concatenate([jnp.full((NSA_HEADS, padl), NEG, F32), bvec], axis=1)
    pos = jnp.arange(padl + nmax) - padl
    vwin = jnp.where(pos < WINDOW, vext, NEG)

    mlo, mhi = c0 - wc + 1, NA - 1 + c0
    start = padl + CMP_STRIDE * mlo - (CMP_BLOCK - 1)
    assert start >= 0
    u = vext[:, start:start + CMP_STRIDE * (mhi - mlo + 1)].reshape(NSA_HEADS, mhi - mlo + 1, CMP_STRIDE)
    urev = u[:, ::-1, :]
    gcb = jnp.stack([urev[:, NA - 1 - a:NA - 1 - a + wc, :] for a in range(NA)], axis=1)
    gcb = gcb.transpose(0, 1, 3, 2).reshape(HKV, GQA * qb, wc)
    nqb = seq // qb
    bc = jnp.stack([gcb[:, :, c0 - NA * i:c0 - NA * i + n_pad] for i in range(nqb)], axis=1)

    def toeplitz(vec, d0, width):
        vrev = vec[:, ::-1]
        s0 = nmax - d0 - qb
        period = width + qb - 1
        assert s0 >= 0 and s0 + period <= padl + nmax
        sl = vrev[:, s0:s0 + period]
        y = jnp.concatenate([sl[:, qb - 1:], sl[:, :qb - 1]], axis=1)
        z = jnp.tile(y, (1, qb))[:, :qb * (period - 1)].reshape(NSA_HEADS, qb, period - 1)
        return z[:, :, :width].reshape(HKV, GQA * qb, width)

    gs = toeplitz(vext, SEL_FAR * qb, ws)
    bw = tuple(tuple(toeplitz(vwin, first + b * qb, WINDOW + ATT_NQ * qb) for b in range(ATT_NQ))
               for first in (0, WINDOW))
    return bc, gs, bw


def _overlap_matrix_t(n_pad):
    c_start = np.arange(n_pad)[None, :] * CMP_STRIDE
    s_start = np.arange(MASK_ROWS)[:, None] * SLC_BLOCK
    ov = np.clip(np.minimum(c_start + CMP_BLOCK, s_start + SLC_BLOCK) - np.maximum(c_start, s_start), 0, None)
    return ov.astype(np.float32) / CMP_BLOCK


def nsa_branch(h1, h2, kv, hgl, gate_b4, kct, vc, tables, bsz, seq):
    qb = Q_BLOCK
    nqb = seq // qb
    n_pad = seq // CMP_STRIDE
    n_slc = seq // SLC_BLOCK
    assert n_slc <= MASK_ROWS and seq % SEL_TK == 0 and nqb % SEL_SUB == 0
    assert (SEL_TK // qb) % ATT_NQ == 0 and nqb % ATT_NQ == 0
    bc, gs, bw = tables
    m = bsz * seq
    gw = GQA * DK

    sq = SEL_SUB * qb
    nsq = seq // sq
    ovt = jnp.asarray(_overlap_matrix_t(n_pad), BF16)
    qsel, ocg = pl.pallas_call(
        _nsa_select_kernel,
        grid=(HKV, bsz, nsq),
        in_specs=[
            pl.BlockSpec((sq, gw), lambda hh, b, i: (b * nsq + i, C1_QC // gw + hh)),
            pl.BlockSpec((sq, LANES), lambda hh, b, i: (b * nsq + i, hh)),
            pl.BlockSpec((1, LANES), lambda hh, b, i: (0, hh)),
            pl.BlockSpec((1, DK, n_pad), lambda hh, b, i: (b * HKV + hh, 0, 0)),
            pl.BlockSpec((1, n_pad, DK), lambda hh, b, i: (b * HKV + hh, 0, 0)),
            pl.BlockSpec((1, SEL_SUB) + bc.shape[2:], lambda hh, b, i: (hh, i, 0, 0)),
            pl.BlockSpec(ovt.shape, lambda hh, b, i: (0, 0)),
        ],
        out_specs=[pl.BlockSpec((1, SEL_SUB, LANES, GQA * qb), lambda hh, b, i: (hh, b * nsq + i, 0, 0)),
                   pl.BlockSpec((sq, gw), lambda hh, b, i: (b * nsq + i, hh))],
        out_shape=[jax.ShapeDtypeStruct((HKV, bsz * nqb, LANES, GQA * qb), BF16),
                   jax.ShapeDtypeStruct((m, D_BR), F32)],
        scratch_shapes=[pltpu.VMEM((SEL_SUB, MASK_ROWS, qb), F32)],
        compiler_params=_params(),
        name="nsa_select",
    )(h1, hgl, gate_b4, kct, vc, bc, ovt)

    gs_t = gs.transpose(0, 2, 1)
    bw_t = jnp.stack([jnp.concatenate([w.transpose(0, 2, 1) for w in variant], axis=2) for variant in bw],
                     axis=1)
    ks_ext, kw_ext, vs_ext, vw_ext = kv

    aq = ATT_NQ * qb
    npair = nqb // ATT_NQ
    rowblk = lambda width, c0: pl.BlockSpec((aq, width), lambda hh, b, i: (b * npair + i, c0 // width + hh))
    keys = pl.BlockSpec((seq, LANES), lambda hh, b, i: (b, hh))
    vals = pl.BlockSpec((1, LANES, seq), lambda hh, b, i: (hh, 0, b))
    return pl.pallas_call(
        _nsa_attn_kernel,
        grid=(HKV, bsz, npair),
        in_specs=[
            pl.BlockSpec((1, ATT_NQ, LANES, GQA * qb), lambda hh, b, i: (hh, b * npair + i, 0, 0)),
            rowblk(gw, 0), rowblk(gw, C2_GC),
            pl.BlockSpec((aq, LANES), lambda hh, b, i: (b * npair + i, hh)),
            pl.BlockSpec((1, LANES), lambda hh, b, i: (0, hh)),
            keys, vals, pl.BlockSpec((1,) + gs_t.shape[1:], lambda hh, b, i: (hh, 0, 0)),
            keys, vals, pl.BlockSpec((1, 1) + bw_t.shape[2:], lambda hh, b, i: (hh, jnp.minimum(i, 1), 0, 0)),
        ],
        out_specs=pl.BlockSpec((aq, gw), lambda hh, b, i: (b * npair + i, hh)),
        out_shape=jax.ShapeDtypeStruct((m, D_BR), BF16),
        scratch_shapes=[pltpu.VMEM((SEL_TK, ATT_NQ * GQA * qb), F32), pltpu.VMEM((SEL_TK, ATT_NQ * GQA * qb), BF16)],
        compiler_params=_params(),
        name="nsa_attn",
    )(qsel, ocg, h2, hgl, gate_b4, ks_ext, vs_ext, gs_t, kw_ext, vw_ext, bw_t)


def _final_kernel(oa_ref, ob_ref, oc_ref, om_ref, *rest):
    gate_refs = rest[:2 * N_BRANCHES]
    x_ref, wb_ref, wo_ref, lng_ref, lnb_ref, y_ref, yb_ref = rest[2 * N_BRANCHES:]
    o_refs = (oa_ref, ob_ref, oc_ref, om_ref)
    halves = []
    for c in range(2):
        cs = slice(c * D_BR, (c + 1) * D_BR)
        acc = None
        for k in range(N_BRANCHES):
            term = _sigmoid(gate_refs[2 * k + c][...]) * _dot(o_refs[k][...], wb_ref[k, :, cs])
            acc = term if acc is None else acc + term
        halves.append(acc.astype(BF16))
    merged = jnp.concatenate(halves, axis=1)
    z = DN_ALPHA * x_ref[...] + _dot(merged, wo_ref[...])
    mu = jnp.mean(z, axis=-1, keepdims=True)
    zc = z - mu
    var = jnp.mean(zc * zc, axis=-1, keepdims=True)
    y = zc * lax.rsqrt(var + LN_EPS) * lng_ref[...] + lnb_ref[...]
    y_ref[...] = y
    yb_ref[...] = y.astype(BF16)


def final_merge(o_a, o_b, o_c, o_m, h2, x, w_branch, w_out, ln_g, ln_b, tm=128):
    m = x.shape[0]
    tm = min(tm, m)
    br = pl.BlockSpec((tm, D_BR), lambda i: (i, 0))
    gate = lambda k: pl.BlockSpec((tm, D_BR), lambda i, k=k: (i, C2_MERGE // D_BR + k))
    xs = pl.BlockSpec((tm, D_MODEL), lambda i: (i, 0))
    resident = lambda a: pl.BlockSpec(a.shape, lambda i: (0,) * a.ndim, pipeline_mode=pl.Buffered(1))
    wb = w_branch.astype(BF16)
    wo = w_out.astype(BF16)
    lg, lb = ln_g.reshape(1, D_MODEL), ln_b.reshape(1, D_MODEL)
    ngate = 2 * N_BRANCHES
    return pl.pallas_call(
        _final_kernel,
        grid=(m // tm,),
        in_specs=[br, br, br, br] + [gate(k) for k in range(ngate)] + [xs]
                 + [resident(wb), resident(wo), resident(lg), resident(lb)],
        out_specs=[xs, xs],
        out_shape=[jax.ShapeDtypeStruct((m, D_MODEL), F32), jax.ShapeDtypeStruct((m, D_MODEL), BF16)],
        compiler_params=_params(),
        name="merge_out_ln",
    )(o_a, o_b, o_c, o_m, *([h2] * ngate), x, wb, wo, lg, lb)


def _gate_spread_matrix():
    p = np.zeros((3 * NSA_HEADS, HKV * LANES), np.float32)
    for hh in range(HKV):
        for brn in range(3):
            for g in range(GQA):
                p[brn * NSA_HEADS + hh * GQA + g, hh * LANES + brn * GQA + g] = 1.0
    return p


def _spread_gate_cols(a):
    return jnp.dot(a, jnp.asarray(_gate_spread_matrix()), precision=lax.Precision.HIGHEST)


def layer(l, x, xb, mem_b, tables, bsz, seq, w_in, sgu_ln_g, sgu_ln_b, sgu_w, sgu_b, conv_w, conv_b,
          lru_wa, lru_ba, lru_wx, lru_bx, lru_lambda, cmp_pe_k, cmp_pe_v, cmp_w1_k, cmp_w1_v,
          cmp_w2_k, cmp_w2_v, nsa_gate_b, w_mem_kv, w_branch, w_out, ln_g, ln_b):
    m = bsz * seq
    tm = min(1024, m)
    h1 = matmul_f32w(xb, w_in, l, 0, N1, F32, tm, 512, w_is_nk=True)
    kv = kv_projection(xb, w_in, l, N1, seq, tm)
    h2 = matmul_f32w(xb, w_in, l, GL_OFF + 3 * NSA_HEADS, N2, F32, tm, 512, w_is_nk=True)
    w_gl = _spread_gate_cols(w_in[l, GL_OFF:GL_OFF + 3 * NSA_HEADS, :].T).astype(BF16)
    hgl = matmul(xb, w_gl, F32, tm, HKV * LANES)
    gate_b4 = _spread_gate_cols(nsa_gate_b[l].reshape(1, 3 * NSA_HEADS))

    o_a = gmlp_branch(h1, sgu_ln_g[l], sgu_ln_b[l], sgu_w[l], sgu_b[l])
    o_b = lru_branch(h1, bsz, seq, conv_w[l], conv_b[l], lru_wa[l], lru_ba[l], lru_wx[l], lru_bx[l],
                     lru_lambda[l])

    kct, vc = nsa_compress(h1, bsz, seq, cmp_pe_k[l], cmp_pe_v[l],
                           cmp_w1_k[l], cmp_w1_v[l], cmp_w2_k[l], cmp_w2_v[l])
    o_c = nsa_branch(h1, h2, kv, hgl, gate_b4, kct, vc, tables, bsz, seq)

    mrows = mem_b.shape[0]
    kv = matmul_f32w(mem_b, w_mem_kv, l, 0, 2 * D_BR, BF16, min(512, mrows), 512)
    o_m = mem_branch(h2, kv, bsz, seq)
    return final_merge(o_a, o_b, o_c, o_m, h2, x, w_branch[l], w_out[l], ln_g[l], ln_b[l])


def kernel(x, mem, rel_bias, w_in, sgu_ln_g, sgu_ln_b, sgu_w, sgu_b, conv_w, conv_b, lru_wa, lru_ba, lru_wx,
           lru_bx, lru_lambda, cmp_pe_k, cmp_pe_v, cmp_w1_k, cmp_w1_v, cmp_w2_k, cmp_w2_v, nsa_gate_b,
           w_mem_kv, w_branch, w_out, ln_g, ln_b):
    bsz, seq, _ = x.shape
    tables = nsa_bias_tables(rel_bias, seq)
    xf = x.reshape(bsz * seq, D_MODEL)
    xb = xf.astype(BF16)
    mem_b = mem.reshape(-1, D_MODEL).astype(BF16)
    w_in_t = jnp.swapaxes(w_in, 1, 2)
    params = (w_in_t, sgu_ln_g, sgu_ln_b, sgu_w, sgu_b, conv_w, conv_b, lru_wa, lru_ba, lru_wx, lru_bx,
              lru_lambda, cmp_pe_k, cmp_pe_v, cmp_w1_k, cmp_w1_v, cmp_w2_k, cmp_w2_v, nsa_gate_b,
              w_mem_kv, w_branch, w_out, ln_g, ln_b)
    for l in range(w_in.shape[0]):
        xf, xb = layer(l, xf, xb, mem_b, tables, bsz, seq, *params)
    return xf.reshape(bsz, seq, D_MODEL)
```

```python
import functools
import math

import numpy as np
import jax
import jax.numpy as jnp
from jax import lax
from jax.experimental import pallas as pl
from jax.experimental.pallas import tpu as pltpu

F32 = jnp.float32
BF16 = jnp.bfloat16

D_MODEL = 2048
DEPTH = 2
D_BR = D_MODEL // 2
N_BRANCHES = 4
GMLP_CHUNK = 128
GMLP_GROUPS = 8
LRU_BLOCKS = 8
LRU_BLOCK_DIM = D_BR // LRU_BLOCKS
CONV_WIDTH = 4
LRU_C = 8.0
DK = 64
NSA_HEADS = D_BR // DK
HKV = NSA_HEADS // 4
GQA = NSA_HEADS // HKV
KV_W = HKV * DK
CMP_BLOCK = 32
CMP_STRIDE = 16
CMP_HIDDEN = 256
SLC_BLOCK = 64
SLC_TOPK = 8
WINDOW = 256
Q_BLOCK = 128
MEM_HEADS = 4
MEM_HEAD_DIM = D_BR // MEM_HEADS
REL_BUCKETS = 32
REL_MAX_DIST = 1024
DN_ALPHA = (2 * DEPTH) ** 0.25
LN_EPS = 1e-5

LANES = 128
SUBLANES = 8
NEG = -1e30
SEL_TK = 512
SEL_FAR = 11
MASK_ROWS = LANES - DK
NA = Q_BLOCK // CMP_STRIDE
SEL_SUB = 4
ATT_NQ = 2
V_ROWS = DK + 16
VMEM_LIMIT = 56 * 1024 * 1024

C1_U, C1_V, C1_GA, C1_XB, C1_GB, C1_QC = (D_BR * k for k in range(6))
C1_KV = 6 * D_BR
N1 = C1_KV + 2 * KV_W
NKV = 4 * KV_W
C2_GC, C2_QM, C2_GM, C2_MERGE = (D_BR * k for k in range(4))
N2 = C2_MERGE + N_BRANCHES * D_MODEL
GL_OFF = N1 + NKV


def _sigmoid(x):
    return 1.0 / (1.0 + jnp.exp(-x))


def _silu(x):
    return x * _sigmoid(x)


def _gelu_tanh(x):
    return 0.5 * x * (1.0 + jnp.tanh(math.sqrt(2.0 / math.pi) * (x + 0.044715 * (x * x * x))))


def _dot(a, b):
    return jnp.dot(a, b, preferred_element_type=F32)


def _dot_nt(a, b):
    return lax.dot_general(a, b, (((1,), (1,)), ((), ())), preferred_element_type=F32)


def _params(**kw):
    return pltpu.CompilerParams(vmem_limit_bytes=VMEM_LIMIT, **kw)


def _mm_kernel(x_ref, w_ref, o_ref):
    o_ref[...] = _dot(x_ref[...], w_ref[...]).astype(o_ref.dtype)


def matmul(x, w, out_dtype, tm, tn):
    m, k = x.shape
    n = w.shape[1]
    assert m % tm == 0 and n % tn == 0, (m, n, tm, tn)
    return pl.pallas_call(
        _mm_kernel,
        grid=(n // tn, m // tm),
        in_specs=[pl.BlockSpec((tm, k), lambda j, i: (i, 0)),
                  pl.BlockSpec((k, tn), lambda j, i: (0, j))],
        out_specs=pl.BlockSpec((tm, tn), lambda j, i: (i, j)),
        out_shape=jax.ShapeDtypeStruct((m, n), out_dtype),
        compiler_params=_params(),
        name="matmul",
    )(x, w)


def _mm_castw_kernel(x_ref, *rest, w_is_nk, shift):
    w_refs, (o_ref, wb_ref) = rest[:-2], rest[-2:]

    @pl.when(pl.program_id(1) == 0)
    def _():
        if shift:
            w = jnp.concatenate([w_refs[0][0, shift:, :], w_refs[1][0, :shift, :]], axis=0)
        else:
            w = w_refs[0][0]
        wb_ref[...] = (w.T if w_is_nk else w).astype(BF16)

    o_ref[...] = _dot(x_ref[...], wb_ref[...]).astype(o_ref.dtype)


def matmul_f32w(x, w3, layer_idx, col0, n, out_dtype, tm, tn, w_is_nk=False):
    m, k = x.shape
    ncols = w3.shape[1] if w_is_nk else w3.shape[2]
    shift = col0 % tn
    assert m % tm == 0 and n % tn == 0 and col0 + n <= ncols, (m, n, col0, tm, tn)
    assert shift == 0 or (w_is_nk and shift % SUBLANES == 0), (col0, tn)
    j0 = col0 // tn
    if w_is_nk:
        w_specs = [pl.BlockSpec((1, tn, k), lambda j, i, d=d: (layer_idx, j0 + j + d, 0))
                   for d in range(2 if shift else 1)]
    else:
        w_specs = [pl.BlockSpec((1, k, tn), lambda j, i: (layer_idx, 0, j0 + j))]
    return pl.pallas_call(
        functools.partial(_mm_castw_kernel, w_is_nk=w_is_nk, shift=shift),
        grid=(n // tn, m // tm),
        in_specs=[pl.BlockSpec((tm, k), lambda j, i: (i, 0))] + w_specs,
        out_specs=pl.BlockSpec((tm, tn), lambda j, i: (i, j)),
        out_shape=jax.ShapeDtypeStruct((m, n), out_dtype),
        scratch_shapes=[pltpu.VMEM((k, tn), BF16)],
        compiler_params=_params(dimension_semantics=("arbitrary", "arbitrary")),
        name="matmul_f32w",
    )(x, *([w3] * len(w_specs)))


def _kv_proj_kernel(x_ref, ws_ref, ww_ref, ks_ref, kw_ref, vs_ref, vw_ref, wk_s, wv_s, *, tm, seq):
    @pl.when(pl.program_id(0) == 0)
    def _():
        wk = jnp.concatenate([ws_ref[0, 0:KV_W], ww_ref[0, 0:KV_W]], axis=0)
        wk_s[...] = wk.T.astype(BF16)
        wv_s[...] = jnp.concatenate([ws_ref[0, KV_W:2 * KV_W], ww_ref[0, KV_W:2 * KV_W]], axis=0).astype(BF16)

    x = x_ref[...]
    kk = _dot(x, wk_s[...])
    vt = _dot_nt(wv_s[...], x)
    pos = (pl.program_id(0) * tm + lax.broadcasted_iota(jnp.int32, (tm, MASK_ROWS), 0)) % seq
    blk = lax.broadcasted_iota(jnp.int32, (tm, MASK_ROWS), 1)
    onehot = jnp.where((pos >> int(math.log2(SLC_BLOCK))) == blk, 1.0, 0.0)
    zeros = jnp.zeros((tm, MASK_ROWS), F32)
    tail = jnp.where(lax.broadcasted_iota(jnp.int32, (LANES - DK, tm), 0) == 0, 1.0, 0.0)
    for h in range(HKV):
        ks_ref[:, h * LANES:(h + 1) * LANES] = jnp.concatenate(
            [kk[:, h * DK:(h + 1) * DK], onehot], axis=1).astype(ks_ref.dtype)
        kw_ref[:, h * LANES:(h + 1) * LANES] = jnp.concatenate(
            [kk[:, KV_W + h * DK:KV_W + (h + 1) * DK], zeros], axis=1).astype(kw_ref.dtype)
        vs_ref[h] = jnp.concatenate([vt[h * DK:(h + 1) * DK], tail], axis=0).astype(vs_ref.dtype)
        vw_ref[h] = jnp.concatenate([vt[KV_W + h * DK:KV_W + (h + 1) * DK], tail], axis=0).astype(vw_ref.dtype)


def kv_projection(x, wt3, layer_idx, row0, seq, tm):
    m, k = x.shape
    nrows = 2 * KV_W
    assert m % tm == 0 and row0 % nrows == 0 and seq % tm == 0
    wspec = lambda d: pl.BlockSpec((1, nrows, k), lambda i: (layer_idx, row0 // nrows + d, 0),
                                   pipeline_mode=pl.Buffered(1))
    keys = jax.ShapeDtypeStruct((m, HKV * LANES), BF16)
    vals = jax.ShapeDtypeStruct((HKV, LANES, m), BF16)
    return pl.pallas_call(
        functools.partial(_kv_proj_kernel, tm=tm, seq=seq),
        grid=(m // tm,),
        in_specs=[pl.BlockSpec((tm, k), lambda i: (i, 0)), wspec(0), wspec(1)],
        out_specs=[pl.BlockSpec((tm, HKV * LANES), lambda i: (i, 0))] * 2
                  + [pl.BlockSpec((HKV, LANES, tm), lambda i: (0, 0, i))] * 2,
        out_shape=[keys, keys, vals, vals],
        scratch_shapes=[pltpu.VMEM((k, 2 * KV_W), BF16), pltpu.VMEM((2 * KV_W, k), BF16)],
        compiler_params=_params(dimension_semantics=("arbitrary",)),
        name="kv_projection",
    )(x, wt3, wt3)


def _gmlp_kernel(u_ref, v_ref, ga_ref, lng_ref, lnb_ref, w_ref, bs_ref, o_ref, *, rows):
    gd = D_BR // GMLP_GROUPS
    for c in range(rows // GMLP_CHUNK):
        r = slice(c * GMLP_CHUNK, (c + 1) * GMLP_CHUNK)
        v = _gelu_tanh(v_ref[r, :])
        mu = jnp.mean(v, axis=-1, keepdims=True)
        vc = v - mu
        var = jnp.mean(vc * vc, axis=-1, keepdims=True)
        vb = (vc * lax.rsqrt(var + LN_EPS) * lng_ref[...] + lnb_ref[...]).astype(BF16)
        u = _gelu_tanh(u_ref[r, :]) * _silu(ga_ref[r, :])
        for g in range(GMLP_GROUPS):
            cs = slice(g * gd, (g + 1) * gd)
            mixed = _dot(w_ref[g], vb[:, cs]) + bs_ref[:, g:g + 1]
            o_ref[r, cs] = (u[:, cs] * mixed).astype(o_ref.dtype)


def gmlp_branch(h1, ln_g, ln_b, w_s, b_s, rows=512):
    m = h1.shape[0]
    rows = min(rows, m)
    causal = jnp.tril(jnp.ones((GMLP_CHUNK, GMLP_CHUNK), dtype=bool))
    w = jnp.where(causal, w_s, 0).astype(BF16)
    col = lambda c: pl.BlockSpec((rows, D_BR), lambda i, c=c: (i, c // D_BR))
    full = lambda a: pl.BlockSpec(a.shape, lambda i: (0,) * a.ndim)
    args = (ln_g.reshape(1, D_BR), ln_b.reshape(1, D_BR), w, b_s.T)
    return pl.pallas_call(
        functools.partial(_gmlp_kernel, rows=rows),
        grid=(m // rows,),
        in_specs=[col(C1_U), col(C1_V), col(C1_GA)] + [full(a) for a in args],
        out_specs=pl.BlockSpec((rows, D_BR), lambda i: (i, 0)),
        out_shape=jax.ShapeDtypeStruct((m, D_BR), BF16),
        compiler_params=_params(),
        name="gmlp",
    )(h1, h1, h1, *args)


def _lru_kernel(xb_ref, gb_ref, cw_ref, cb_ref, wa_ref, ba_ref, wx_ref, bx_ref, lam_ref, o_ref,
                xbuf, hcarry, a_s, g_s, *, ts):
    @pl.when(pl.program_id(1) == 0)
    def _():
        xbuf[0:8, :] = jnp.zeros((8, D_BR), F32)
        hcarry[...] = jnp.zeros((8, D_BR), F32)

    xbuf[8:8 + ts, :] = xb_ref[...]
    xc = cb_ref[...] + cw_ref[0:1, :] * xbuf[pl.ds(8 - (CONV_WIDTH - 1), ts), :]
    for k in range(1, CONV_WIDTH):
        xc = xc + cw_ref[k:k + 1, :] * xbuf[pl.ds(8 - (CONV_WIDTH - 1) + k, ts), :]
    xbuf[0:8, :] = xbuf[ts:ts + 8, :]

    xcb = xc.astype(BF16)
    bd = LRU_BLOCK_DIM
    r = jnp.concatenate([_dot(xcb[:, n * bd:(n + 1) * bd], wa_ref[n]) for n in range(LRU_BLOCKS)], axis=1)
    i = jnp.concatenate([_dot(xcb[:, n * bd:(n + 1) * bd], wx_ref[n]) for n in range(LRU_BLOCKS)], axis=1)
    r = _sigmoid(r + ba_ref[...])
    i = _sigmoid(i + bx_ref[...])
    nl = -lam_ref[...]
    softplus = jnp.maximum(nl, 0.0) + jnp.log1p(jnp.exp(-jnp.abs(nl)))
    log_a = (-LRU_C * softplus) * r
    a = jnp.exp(log_a)
    a_s[...] = a
    g_s[...] = jnp.sqrt(1.0 - a * a) * i * xc

    row = lax.broadcasted_iota(jnp.int32, (8, D_BR), 0)

    def body(j, carry):
        r0 = pl.multiple_of(j * 8, 8)
        av = a_s[pl.ds(r0, 8), :]
        bv = g_s[pl.ds(r0, 8), :]
        for d in (1, 2, 4):
            keep = row >= d
            a_sh = pltpu.roll(av, d, axis=0)
            b_sh = pltpu.roll(bv, d, axis=0)
            bv = jnp.where(keep, av * b_sh + bv, bv)
            av = jnp.where(keep, av * a_sh, av)
        hv = av * carry + bv
        g_s[pl.ds(r0, 8), :] = hv
        return jnp.broadcast_to(hv[7:8, :], (8, D_BR))

    hcarry[...] = lax.fori_loop(0, ts // 8, body, hcarry[...])
    o_ref[...] = (g_s[...] * _silu(gb_ref[...])).astype(o_ref.dtype)


def lru_branch(h1, bsz, seq, conv_w, conv_b, wa, ba, wx, bx, lam, ts=512):
    ts = min(ts, seq)
    ns = seq // ts
    col = lambda c: pl.BlockSpec((ts, D_BR), lambda b, s, c=c: (b * ns + s, c // D_BR))
    full = lambda a: pl.BlockSpec(a.shape, lambda b, s: (0,) * a.ndim)
    row = lambda a: a.reshape(1, D_BR)
    args = (conv_w, row(conv_b), wa.astype(BF16), row(ba), wx.astype(BF16), row(bx), row(lam))
    return pl.pallas_call(
        functools.partial(_lru_kernel, ts=ts),
        grid=(bsz, ns),
        in_specs=[col(C1_XB), col(C1_GB)] + [full(a) for a in args],
        out_specs=pl.BlockSpec((ts, D_BR), lambda b, s: (b * ns + s, 0)),
        out_shape=jax.ShapeDtypeStruct((bsz * seq, D_BR), BF16),
        scratch_shapes=[pltpu.VMEM((ts + 8, D_BR), F32), pltpu.VMEM((8, D_BR), F32),
                        pltpu.VMEM((ts, D_BR), F32), pltpu.VMEM((ts, D_BR), F32)],
        compiler_params=_params(dimension_semantics=("arbitrary", "arbitrary")),
        name="lru",
    )(h1, h1, *args)


def _mem_kernel(q_ref, g_ref, kv_ref, o_ref):
    hd = MEM_HEAD_DIM
    for hh in range(MEM_HEADS):
        cs = slice(hh * hd, (hh + 1) * hd)
        q = (q_ref[:, cs] * (hd ** -0.5)).astype(BF16)
        s = _dot_nt(q, kv_ref[:, cs])
        p = jnp.exp(s - jnp.max(s, axis=-1, keepdims=True))
        l = jnp.sum(p, axis=-1, keepdims=True)
        o = _dot(p.astype(BF16), kv_ref[:, D_BR + hh * hd:D_BR + (hh + 1) * hd]) / l
        o_ref[:, cs] = (o * _silu(g_ref[:, cs])).astype(o_ref.dtype)


def mem_branch(h2, kv, bsz, seq, tq=512):
    tq = min(tq, seq)
    nq = seq // tq
    mlen = kv.shape[0] // bsz
    col = lambda c: pl.BlockSpec((tq, D_BR), lambda b, i, c=c: (b * nq + i, c // D_BR))
    return pl.pallas_call(
        _mem_kernel,
        grid=(bsz, nq),
        in_specs=[col(C2_QM), col(C2_GM), pl.BlockSpec((mlen, 2 * D_BR), lambda b, i: (b, 0))],
        out_specs=pl.BlockSpec((tq, D_BR), lambda b, i: (b * nq + i, 0)),
        out_shape=jax.ShapeDtypeStruct((bsz * seq, D_BR), BF16),
        compiler_params=_params(),
        name="mem_attn",
    )(h2, h2, kv)


def _cmp_kernel(*refs):
    ngrp = 2 * KV_W // LANES
    t_refs = refs[:ngrp]
    pek_ref, pev_ref, w1k_ref, w1v_ref, w2kt_ref, w2v_ref, kct_ref, vc_ref = refs[ngrp:]
    n = t_refs[0].shape[0] // CMP_STRIDE
    for h in range(HKV):
        for is_v, (pe_ref, w1_ref) in enumerate(((pek_ref, w1k_ref), (pev_ref, w1v_ref))):
            grp, off = divmod(is_v * KV_W + h * DK, LANES)
            lo = hi = None
            for tok in range(CMP_STRIDE):
                x = t_refs[grp][pl.ds(tok, n, stride=CMP_STRIDE), :][:, off:off + DK]
                t2 = CMP_STRIDE + tok
                a = _dot((x + pe_ref[tok:tok + 1, :]).astype(BF16), w1_ref[tok * DK:(tok + 1) * DK, :])
                b = _dot((x + pe_ref[t2:t2 + 1, :]).astype(BF16), w1_ref[t2 * DK:(t2 + 1) * DK, :])
                lo = a if lo is None else lo + a
                hi = b if hi is None else hi + b
            hidden = _silu(lo + pltpu.roll(hi, n - 1, axis=0)).astype(BF16)
            if is_v:
                vc_ref[h] = _dot(hidden, w2v_ref[...]).astype(vc_ref.dtype)
            else:
                kct_ref[h] = _dot_nt(w2kt_ref[...], hidden).astype(kct_ref.dtype)


def nsa_compress(h1, bsz, seq, pe_k, pe_v, w1_k, w1_v, w2_k, w2_v):
    n = seq // CMP_STRIDE
    args = (pe_k, pe_v, w1_k.astype(BF16), w1_v.astype(BF16), w2_k.T.astype(BF16), w2_v.astype(BF16))
    full = lambda a: pl.BlockSpec(a.shape, lambda b: (0,) * a.ndim)
    return pl.pallas_call(
        _cmp_kernel,
        grid=(bsz,),
        in_specs=[pl.BlockSpec((seq, LANES), lambda b, g=g: (b, C1_KV // LANES + g))
                  for g in range(2 * KV_W // LANES)] + [full(a) for a in args],
        out_specs=[pl.BlockSpec((HKV, DK, n), lambda b: (b, 0, 0)), pl.BlockSpec((HKV, n, DK), lambda b: (b, 0, 0))],
        out_shape=[jax.ShapeDtypeStruct((bsz * HKV, DK, n), BF16), jax.ShapeDtypeStruct((bsz * HKV, n, DK), BF16)],
        compiler_params=_params(),
        name="nsa_compress",
    )(*([h1] * (2 * KV_W // LANES)), *args)


def _nsa_select_kernel(q_ref, gl_ref, gb_ref, kct_ref, vc_ref, bc_ref, ovt_ref, qsel_ref, ocg_ref, vt_s):
    qb = Q_BLOCK
    lane = lax.broadcasted_iota(jnp.int32, (qb, LANES), 1)
    jrow = lax.broadcasted_iota(jnp.int32, (MASK_ROWS, qb), 0).astype(F32)
    qlane = lax.broadcasted_iota(jnp.int32, (MASK_ROWS, qb), 1)
    sub = lax.broadcasted_iota(jnp.int32, (SUBLANES, qb), 0)
    nv = MASK_ROWS // SUBLANES
    ovt = ovt_ref[...]
    kct = kct_ref[0]
    vc = vc_ref[0]
    for sb in range(SEL_SUB):
        rows = slice(sb * qb, (sb + 1) * qb)
        blk = pl.program_id(2) * SEL_SUB + sb
        q = q_ref[rows, :] * (DK ** -0.5)
        gt = _sigmoid(gl_ref[rows, :] + gb_ref[...])
        qpad = []
        for g in range(GQA):
            t = q[:, (g // 2) * LANES:(g // 2 + 1) * LANES]
            if g % 2:
                t = pltpu.roll(t, DK, axis=1)
            qpad.append(jnp.where(lane < DK, t, 0.0))

        has_keys = blk * qb + lax.broadcasted_iota(jnp.int32, (qb, 1), 0) >= CMP_BLOCK - 1
        o_c = []
        psum = None
        for g in range(GQA):
            s = _dot(qpad[g][:, :DK].astype(BF16), kct) + bc_ref[0, sb, g * qb:(g + 1) * qb, :]
            m = jnp.max(s, axis=-1, keepdims=True)
            p = jnp.exp(s - m)
            p = p * jnp.where(has_keys, 1.0 / jnp.maximum(jnp.sum(p, axis=-1, keepdims=True), 1e-30), 0.0)
            o_c.append(gt[:, g:g + 1] * _dot(p.astype(BF16), vc))
            psum = p if psum is None else psum + p
        ocg_ref[rows, :] = jnp.concatenate(o_c, axis=1)

        p_hi = psum.astype(BF16)
        rem = psum - p_hi.astype(F32)
        p_mid = rem.astype(BF16)
        p_lo = (rem - p_mid.astype(F32)).astype(BF16)
        imp_t = _dot_nt(ovt, p_hi) + _dot_nt(ovt, p_mid) + _dot_nt(ovt, p_lo)

        qblk = ((blk * qb + qlane) >> int(math.log2(SLC_BLOCK))).astype(F32)
        val = jnp.where(jrow == 0.0, 3e38,
                        jnp.where(jrow == qblk, 3e38, jnp.where(jrow == qblk - 1.0, 3e38, imp_t)))
        val = jnp.where(jrow > qblk, -1.0, val)
        vt_s[sb] = val
        vals = [val[k * SUBLANES:(k + 1) * SUBLANES] for k in range(nv)]
        cnt = [jnp.zeros((SUBLANES, qb), F32) for _ in range(nv)]
        for jp in range(MASK_ROWS):
            rowv = jnp.broadcast_to(vt_s[sb, jp:jp + 1, :], (SUBLANES, qb))
            for k in range(nv):
                if k * SUBLANES > jp:
                    beat = jnp.where(rowv >= vals[k], 1.0, 0.0)
                elif (k + 1) * SUBLANES - 1 < jp:
                    beat = jnp.where(rowv > vals[k], 1.0, 0.0)
                else:
                    beat = jnp.where(sub > jp - k * SUBLANES, jnp.where(rowv >= vals[k], 1.0, 0.0),
                                     jnp.where(rowv > vals[k], 1.0, 0.0))
                cnt[k] = cnt[k] + beat
        selneg_t = jnp.concatenate([jnp.where(c < float(SLC_TOPK), 0.0, NEG) for c in cnt], axis=0)
        for g in range(GQA):
            qt = jnp.concatenate([qpad[g].T[:DK], selneg_t], axis=0)
            qsel_ref[0, sb, :, g * qb:(g + 1) * qb] = qt.astype(qsel_ref.dtype)


def _nsa_attn_kernel(qsel_ref, ocg_ref, gc_ref, gl_ref, gb_ref, ks_ref, vs_ref, gs_ref, kw_ref, vw_ref, bw_ref,
                     o_ref, s_s, p_s):
    qb = Q_BLOCK
    hq = GQA * qb
    nq = ATT_NQ * hq
    i = pl.program_id(2) * ATT_NQ
    step = SEL_TK // qb
    n_tiles = (i * qb) // SEL_TK + 1
    qt = jnp.concatenate([qsel_ref[0, b] for b in range(ATT_NQ)], axis=1)

    def stage_qk(j):
        k0 = pl.multiple_of(j * SEL_TK, SEL_TK)
        sc = _dot(ks_ref[pl.ds(k0, SEL_TK), :], qt)
        for b in range(ATT_NQ):
            dd = jnp.minimum(i + b - j * step, SEL_FAR)
            boff = pl.multiple_of((SEL_FAR - dd) * qb, qb)
            s_s[:, b * hq:(b + 1) * hq] = sc[:, b * hq:(b + 1) * hq] + gs_ref[0, pl.ds(boff, SEL_TK), :]

    def stage_softmax(m):
        m_new, alpha = [], []
        for g in range(ATT_NQ * GQA):
            cs = slice(g * qb, (g + 1) * qb)
            sc = s_s[:, cs]
            mg = jnp.maximum(m[:, cs], jnp.max(sc, axis=0, keepdims=True))
            alpha.append(jnp.exp(m[:, cs] - mg))
            p_s[:, cs] = jnp.exp(sc - mg).astype(BF16)
            m_new.append(mg)
        return jnp.concatenate(m_new, axis=1), jnp.concatenate(alpha, axis=1)

    def stage_pv(j, acc):
        k0 = pl.multiple_of(jnp.maximum(j, 0) * SEL_TK, SEL_TK)
        return acc + _dot(vs_ref[0, :V_ROWS, pl.ds(k0, SEL_TK)], p_s[...])

    def advance(j, m, acc):
        acc = stage_pv(j - 2, acc)
        m, alpha = stage_softmax(m)
        return m, alpha * acc

    def body(j, carry):
        m, acc = advance(j, *carry)
        stage_qk(j)
        return m, acc

    p_s[...] = jnp.zeros(p_s.shape, BF16)
    stage_qk(jnp.int32(0))
    init = (jnp.full((1, nq), NEG, F32), jnp.zeros((V_ROWS, nq), F32))
    m, acc = lax.fori_loop(1, n_tiles, body, init)
    _, acc = advance(n_tiles, m, acc)
    acc_s = stage_pv(n_tiles - 1, acc)

    wk = WINDOW + ATT_NQ * qb
    w0 = pl.multiple_of(jnp.maximum(i * qb - WINDOW, 0), qb)
    rowi = lax.broadcasted_iota(jnp.int32, (LANES, nq), 0)
    q_win = jnp.where(rowi < DK, qt, jnp.zeros_like(qt))
    sw = _dot(kw_ref[pl.ds(w0, wk), :], q_win) + bw_ref[0, 0]
    pw = jnp.exp(sw - jnp.max(sw, axis=0, keepdims=True)).astype(BF16)
    acc_w = _dot(vw_ref[0, :V_ROWS, pl.ds(w0, wk)], pw)

    pad_rows = jnp.zeros((LANES - V_ROWS, qb), F32)
    for b in range(ATT_NQ):
        rows = slice(b * qb, (b + 1) * qb)
        gt = _sigmoid(gl_ref[rows, :] + gb_ref[...])
        outs = []
        for g in range(GQA):
            cs = slice(b * hq + g * qb, b * hq + (g + 1) * qb)
            a_w = jnp.concatenate([acc_w[:, cs], pad_rows], axis=0).T
            a_s = jnp.concatenate([acc_s[:, cs], pad_rows], axis=0).T
            o_w = a_w[:, :DK] / a_w[:, DK:DK + 1]
            o_s = a_s[:, :DK] / a_s[:, DK:DK + 1]
            outs.append(gt[:, GQA + g:GQA + g + 1] * o_s + gt[:, 2 * GQA + g:2 * GQA + g + 1] * o_w)
        o = ocg_ref[rows, :] + jnp.concatenate(outs, axis=1)
        o_ref[rows, :] = (o * _silu(gc_ref[rows, :])).astype(o_ref.dtype)


def _rel_bucket(dist):
    n = jnp.maximum(dist, 0)
    exact = REL_BUCKETS // 2
    nf = jnp.maximum(n, 1).astype(jnp.float32)
    large = exact + (jnp.log(nf / exact) / math.log(REL_MAX_DIST / exact)
                     * (REL_BUCKETS - exact)).astype(jnp.int32)
    return jnp.where(n < exact, n, jnp.minimum(large, REL_BUCKETS - 1))


def nsa_bias_tables(rel_bias, seq):
    qb = Q_BLOCK
    n_pad = seq // CMP_STRIDE
    c0 = n_pad - NA
    wc = 2 * n_pad
    ws = SEL_FAR * qb + SEL_TK
    nmax = max(seq, SEL_FAR * qb + qb)
    padl = CMP_STRIDE * (wc + 2)
    bvec = rel_bias[_rel_bucket(jnp.arange(nmax))].T.astype(F32)
    vext = jnp.concatenate([jnp.full((NSA_HEADS, padl), NEG, F32), bvec], axis=1)
    pos = jnp.arange(padl + nmax) - padl
    vwin = jnp.where(pos < WINDOW, vext, NEG)

    mlo, mhi = c0 - wc + 1, NA - 1 + c0
    start = padl + CMP_STRIDE * mlo - (CMP_BLOCK - 1)
    assert start >= 0
    u = vext[:, start:start + CMP_STRIDE * (mhi - mlo + 1)].reshape(NSA_HEADS, mhi - mlo + 1, CMP_STRIDE)
    urev = u[:, ::-1, :]
    gcb = jnp.stack([urev[:, NA - 1 - a:NA - 1 - a + wc, :] for a in range(NA)], axis=1)
    gcb = gcb.transpose(0, 1, 3, 2).reshape(HKV, GQA * qb, wc)
    nqb = seq // qb
    bc = jnp.stack([gcb[:, :, c0 - NA * i:c0 - NA * i + n_pad] for i in range(nqb)], axis=1)

    def toeplitz(vec, d0, width):
        vrev = vec[:, ::-1]
        s0 = nmax - d0 - qb
        period = width + qb - 1
        assert s0 >= 0 and s0 + period <= padl + nmax
        sl = vrev[:, s0:s0 + period]
        y = jnp.concatenate([sl[:, qb - 1:], sl[:, :qb - 1]], axis=1)
        z = jnp.tile(y, (1, qb))[:, :qb * (period - 1)].reshape(NSA_HEADS, qb, period - 1)
        return z[:, :, :width].reshape(HKV, GQA * qb, width)

    gs = toeplitz(vext, SEL_FAR * qb, ws)
    bw = tuple(tuple(toeplitz(vwin, first + b * qb, WINDOW + ATT_NQ * qb) for b in range(ATT_NQ))
               for first in (0, WINDOW))
    return bc, gs, bw


def _overlap_matrix_t(n_pad):
    c_start = np.arange(n_pad)[None, :] * CMP_STRIDE
    s_start = np.arange(MASK_ROWS)[:, None] * SLC_BLOCK
    ov = np.clip(np.minimum(c_start + CMP_BLOCK, s_start + SLC_BLOCK) - np.maximum(c_start, s_start), 0, None)
    return ov.astype(np.float32) / CMP_BLOCK


def nsa_branch(h1, h2, kv, hgl, gate_b4, kct, vc, tables, bsz, seq):
    qb = Q_BLOCK
    nqb = seq // qb
    n_pad = seq // CMP_STRIDE
    n_slc = seq // SLC_BLOCK
    assert n_slc <= MASK_ROWS and seq % SEL_TK == 0 and nqb % SEL_SUB == 0
    assert (SEL_TK // qb) % ATT_NQ == 0 and nqb % ATT_NQ == 0
    bc, gs, bw = tables
    m = bsz * seq
    gw = GQA * DK

    sq = SEL_SUB * qb
    nsq = seq // sq
    ovt = jnp.asarray(_overlap_matrix_t(n_pad), BF16)
    qsel, ocg = pl.pallas_call(
        _nsa_select_kernel,
        grid=(HKV, bsz, nsq),
        in_specs=[
            pl.BlockSpec((sq, gw), lambda hh, b, i: (b * nsq + i, C1_QC // gw + hh)),
            pl.BlockSpec((sq, LANES), lambda hh, b, i: (b * nsq + i, hh)),
            pl.BlockSpec((1, LANES), lambda hh, b, i: (0, hh)),
            pl.BlockSpec((1, DK, n_pad), lambda hh, b, i: (b * HKV + hh, 0, 0)),
            pl.BlockSpec((1, n_pad, DK), lambda hh, b, i: (b * HKV + hh, 0, 0)),
            pl.BlockSpec((1, SEL_SUB) + bc.shape[2:], lambda hh, b, i: (hh, i, 0, 0)),
            pl.BlockSpec(ovt.shape, lambda hh, b, i: (0, 0)),
        ],
        out_specs=[pl.BlockSpec((1, SEL_SUB, LANES, GQA * qb), lambda hh, b, i: (hh, b * nsq + i, 0, 0)),
                   pl.BlockSpec((sq, gw), lambda hh, b, i: (b * nsq + i, hh))],
        out_shape=[jax.ShapeDtypeStruct((HKV, bsz * nqb, LANES, GQA * qb), BF16),
                   jax.ShapeDtypeStruct((m, D_BR), F32)],
        scratch_shapes=[pltpu.VMEM((SEL_SUB, MASK_ROWS, qb), F32)],
        compiler_params=_params(),
        name="nsa_select",
    )(h1, hgl, gate_b4, kct, vc, bc, ovt)

    gs_t = gs.transpose(0, 2, 1)
    bw_t = jnp.stack([jnp.concatenate([w.transpose(0, 2, 1) for w in variant], axis=2) for variant in bw],
                     axis=1)
    ks_ext, kw_ext, vs_ext, vw_ext = kv

    aq = ATT_NQ * qb
    npair = nqb // ATT_NQ
    rowblk = lambda width, c0: pl.BlockSpec((aq, width), lambda hh, b, i: (b * npair + i, c0 // width + hh))
    keys = pl.BlockSpec((seq, LANES), lambda hh, b, i: (b, hh))
    vals = pl.BlockSpec((1, LANES, seq), lambda hh, b, i: (hh, 0, b))
    return pl.pallas_call(
        _nsa_attn_kernel,
        grid=(HKV, bsz, npair),
        in_specs=[
            pl.BlockSpec((1, ATT_NQ, LANES, GQA * qb), lambda hh, b, i: (hh, b * npair + i, 0, 0)),
            rowblk(gw, 0), rowblk(gw, C2_GC),
            pl.BlockSpec((aq, LANES), lambda hh, b, i: (b * npair + i, hh)),
            pl.BlockSpec((1, LANES), lambda hh, b, i: (0, hh)),
            keys, vals, pl.BlockSpec((1,) + gs_t.shape[1:], lambda hh, b, i: (hh, 0, 0)),
            keys, vals, pl.BlockSpec((1, 1) + bw_t.shape[2:], lambda hh, b, i: (hh, jnp.minimum(i, 1), 0, 0)),
        ],
        out_specs=pl.BlockSpec((aq, gw), lambda hh, b, i: (b * npair + i, hh)),
        out_shape=jax.ShapeDtypeStruct((m, D_BR), BF16),
        scratch_shapes=[pltpu.VMEM((SEL_TK, ATT_NQ * GQA * qb), F32), pltpu.VMEM((SEL_TK, ATT_NQ * GQA * qb), BF16)],
        compiler_params=_params(),
        name="nsa_attn",
    )(qsel, ocg, h2, hgl, gate_b4, ks_ext, vs_ext, gs_t, kw_ext, vw_ext, bw_t)


def _final_kernel(oa_ref, ob_ref, oc_ref, om_ref, *rest):
    gate_refs = rest[:2 * N_BRANCHES]
    x_ref, wb_ref, wo_ref, lng_ref, lnb_ref, y_ref, yb_ref = rest[2 * N_BRANCHES:]
    o_refs = (oa_ref, ob_ref, oc_ref, om_ref)
    halves = []
    for c in range(2):
        cs = slice(c * D_BR, (c + 1) * D_BR)
        acc = None
        for k in range(N_BRANCHES):
            term = _sigmoid(gate_refs[2 * k + c][...]) * _dot(o_refs[k][...], wb_ref[k, :, cs])
            acc = term if acc is None else acc + term
        halves.append(acc.astype(BF16))
    merged = jnp.concatenate(halves, axis=1)
    z = DN_ALPHA * x_ref[...] + _dot(merged, wo_ref[...])
    mu = jnp.mean(z, axis=-1, keepdims=True)
    zc = z - mu
    var = jnp.mean(zc * zc, axis=-1, keepdims=True)
    y = zc * lax.rsqrt(var + LN_EPS) * lng_ref[...] + lnb_ref[...]
    y_ref[...] = y
    yb_ref[...] = y.astype(BF16)


def final_merge(o_a, o_b, o_c, o_m, h2, x, w_branch, w_out, ln_g, ln_b, tm=128):
    m = x.shape[0]
    tm = min(tm, m)
    br = pl.BlockSpec((tm, D_BR), lambda i: (i, 0))
    gate = lambda k: pl.BlockSpec((tm, D_BR), lambda i, k=k: (i, C2_MERGE // D_BR + k))
    xs = pl.BlockSpec((tm, D_MODEL), lambda i: (i, 0))
    resident = lambda a: pl.BlockSpec(a.shape, lambda i: (0,) * a.ndim, pipeline_mode=pl.Buffered(1))
    wb = w_branch.astype(BF16)
    wo = w_out.astype(BF16)
    lg, lb = ln_g.reshape(1, D_MODEL), ln_b.reshape(1, D_MODEL)
    ngate = 2 * N_BRANCHES
    return pl.pallas_call(
        _final_kernel,
        grid=(m // tm,),
        in_specs=[br, br, br, br] + [gate(k) for k in range(ngate)] + [xs]
                 + [resident(wb), resident(wo), resident(lg), resident(lb)],
        out_specs=[xs, xs],
        out_shape=[jax.ShapeDtypeStruct((m, D_MODEL), F32), jax.ShapeDtypeStruct((m, D_MODEL), BF16)],
        compiler_params=_params(),
        name="merge_out_ln",
    )(o_a, o_b, o_c, o_m, *([h2] * ngate), x, wb, wo, lg, lb)


def _gate_spread_matrix():
    p = np.zeros((3 * NSA_HEADS, HKV * LANES), np.float32)
    for hh in range(HKV):
        for brn in range(3):
            for g in range(GQA):
                p[brn * NSA_HEADS + hh * GQA + g, hh * LANES + brn * GQA + g] = 1.0
    return p


def _spread_gate_cols(a):
    return jnp.dot(a, jnp.asarray(_gate_spread_matrix()), precision=lax.Precision.HIGHEST)


def layer(l, x, xb, mem_b, tables, bsz, seq, w_in, sgu_ln_g, sgu_ln_b, sgu_w, sgu_b, conv_w, conv_b,
          lru_wa, lru_ba, lru_wx, lru_bx, lru_lambda, cmp_pe_k, cmp_pe_v, cmp_w1_k, cmp_w1_v,
          cmp_w2_k, cmp_w2_v, nsa_gate_b, w_mem_kv, w_branch, w_out, ln_g, ln_b):
    m = bsz * seq
    tm = min(1024, m)
    tm_big = min(2048, m)
    h1 = matmul_f32w(xb, w_in, l, 0, N1, F32, tm_big, 512, w_is_nk=True)
    kv = kv_projection(xb, w_in, l, N1, seq, tm)
    h2 = matmul_f32w(xb, w_in, l, GL_OFF + 3 * NSA_HEADS, N2, F32, tm_big, 512, w_is_nk=True)
    w_gl = _spread_gate_cols(w_in[l, GL_OFF:GL_OFF + 3 * NSA_HEADS, :].T).astype(BF16)
    hgl = matmul(xb, w_gl, F32, tm, HKV * LANES)
    gate_b4 = _spread_gate_cols(nsa_gate_b[l].reshape(1, 3 * NSA_HEADS))

    o_a = gmlp_branch(h1, sgu_ln_g[l], sgu_ln_b[l], sgu_w[l], sgu_b[l])
    o_b = lru_branch(h1, bsz, seq, conv_w[l], conv_b[l], lru_wa[l], lru_ba[l], lru_wx[l], lru_bx[l],
                     lru_lambda[l])

    kct, vc = nsa_compress(h1, bsz, seq, cmp_pe_k[l], cmp_pe_v[l],
                           cmp_w1_k[l], cmp_w1_v[l], cmp_w2_k[l], cmp_w2_v[l])
    o_c = nsa_branch(h1, h2, kv, hgl, gate_b4, kct, vc, tables, bsz, seq)

    mrows = mem_b.shape[0]
    kv = matmul_f32w(mem_b, w_mem_kv, l, 0, 2 * D_BR, BF16, min(512, mrows), 512)
    o_m = mem_branch(h2, kv, bsz, seq)
    return final_merge(o_a, o_b, o_c, o_m, h2, x, w_branch[l], w_out[l], ln_g[l], ln_b[l])


def kernel(x, mem, rel_bias, w_in, sgu_ln_g, sgu_ln_b, sgu_w, sgu_b, conv_w, conv_b, lru_wa, lru_ba, lru_wx,
           lru_bx, lru_lambda, cmp_pe_k, cmp_pe_v, cmp_w1_k, cmp_w1_v, cmp_w2_k, cmp_w2_v, nsa_gate_b,
           w_mem_kv, w_branch, w_out, ln_g, ln_b):
    bsz, seq, _ = x.shape
    tables = nsa_bias_tables(rel_bias, seq)
    xf = x.reshape(bsz * seq, D_MODEL)
    xb = xf.astype(BF16)
    mem_b = mem.reshape(-1, D_MODEL).astype(BF16)
    w_in_t = jnp.swapaxes(w_in, 1, 2)
    params = (w_in_t, sgu_ln_g, sgu_ln_b, sgu_w, sgu_b, conv_w, conv_b, lru_wa, lru_ba, lru_wx, lru_bx,
              lru_lambda, cmp_pe_k, cmp_pe_v, cmp_w1_k, cmp_w1_v, cmp_w2_k, cmp_w2_v, nsa_gate_b,
              w_mem_kv, w_branch, w_out, ln_g, ln_b)
    for l in range(w_in.shape[0]):
        xf, xb = layer(l, xf, xb, mem_b, tables, bsz, seq, *params)
    return xf.reshape(bsz, seq, D_MODEL)
```

```python
import functools
import math

import numpy as np
import jax
import jax.numpy as jnp
from jax import lax
from jax.experimental import pallas as pl
from jax.experimental.pallas import tpu as pltpu

F32 = jnp.float32
BF16 = jnp.bfloat16

D_MODEL = 2048
DEPTH = 2
D_BR = D_MODEL // 2
N_BRANCHES = 4
GMLP_CHUNK = 128
GMLP_GROUPS = 8
LRU_BLOCKS = 8
LRU_BLOCK_DIM = D_BR // LRU_BLOCKS
CONV_WIDTH = 4
LRU_C = 8.0
DK = 64
NSA_HEADS = D_BR // DK
HKV = NSA_HEADS // 4
GQA = NSA_HEADS // HKV
KV_W = HKV * DK
CMP_BLOCK = 32
CMP_STRIDE = 16
CMP_HIDDEN = 256
SLC_BLOCK = 64
SLC_TOPK = 8
WINDOW = 256
Q_BLOCK = 128
MEM_HEADS = 4
MEM_HEAD_DIM = D_BR // MEM_HEADS
REL_BUCKETS = 32
REL_MAX_DIST = 1024
DN_ALPHA = (2 * DEPTH) ** 0.25
LN_EPS = 1e-5

LANES = 128
SUBLANES = 8
NEG = -1e30
SEL_TK = 512
SEL_FAR = 11
MASK_ROWS = LANES - DK
NA = Q_BLOCK // CMP_STRIDE
SEL_SUB = 4
ATT_NQ = 2
V_ROWS = DK + 16
VMEM_LIMIT = 56 * 1024 * 1024
VMEM_LIMIT_MERGE = 60 * 1024 * 1024

C1_U, C1_V, C1_GA, C1_XB, C1_GB, C1_QC = (D_BR * k for k in range(6))
C1_KV = 6 * D_BR
N1 = C1_KV + 2 * KV_W
NKV = 4 * KV_W
C2_GC, C2_QM, C2_GM, C2_MERGE = (D_BR * k for k in range(4))
N2 = C2_MERGE + N_BRANCHES * D_MODEL
GL_OFF = N1 + NKV


def _sigmoid(x):
    return 1.0 / (1.0 + jnp.exp(-x))


def _silu(x):
    return x * _sigmoid(x)


def _gelu_tanh(x):
    return 0.5 * x * (1.0 + jnp.tanh(math.sqrt(2.0 / math.pi) * (x + 0.044715 * (x * x * x))))


def _dot(a, b):
    return jnp.dot(a, b, preferred_element_type=F32)


def _dot_nt(a, b):
    return lax.dot_general(a, b, (((1,), (1,)), ((), ())), preferred_element_type=F32)


def _params(**kw):
    return pltpu.CompilerParams(vmem_limit_bytes=VMEM_LIMIT, **kw)


def _mm_kernel(x_ref, w_ref, o_ref):
    o_ref[...] = _dot(x_ref[...], w_ref[...]).astype(o_ref.dtype)


def matmul(x, w, out_dtype, tm, tn):
    m, k = x.shape
    n = w.shape[1]
    assert m % tm == 0 and n % tn == 0, (m, n, tm, tn)
    return pl.pallas_call(
        _mm_kernel,
        grid=(n // tn, m // tm),
        in_specs=[pl.BlockSpec((tm, k), lambda j, i: (i, 0)),
                  pl.BlockSpec((k, tn), lambda j, i: (0, j))],
        out_specs=pl.BlockSpec((tm, tn), lambda j, i: (i, j)),
        out_shape=jax.ShapeDtypeStruct((m, n), out_dtype),
        compiler_params=_params(),
        name="matmul",
    )(x, w)


def _mm_castw_kernel(x_ref, *rest, w_is_nk, shift):
    w_refs, (o_ref, wb_ref) = rest[:-2], rest[-2:]

    @pl.when(pl.program_id(1) == 0)
    def _():
        if shift:
            w = jnp.concatenate([w_refs[0][0, shift:, :], w_refs[1][0, :shift, :]], axis=0)
        else:
            w = w_refs[0][0]
        wb_ref[...] = (w.T if w_is_nk else w).astype(BF16)

    o_ref[...] = _dot(x_ref[...], wb_ref[...]).astype(o_ref.dtype)


def matmul_f32w(x, w3, layer_idx, col0, n, out_dtype, tm, tn, w_is_nk=False):
    m, k = x.shape
    ncols = w3.shape[1] if w_is_nk else w3.shape[2]
    shift = col0 % tn
    assert m % tm == 0 and n % tn == 0 and col0 + n <= ncols, (m, n, col0, tm, tn)
    assert shift == 0 or (w_is_nk and shift % SUBLANES == 0), (col0, tn)
    j0 = col0 // tn
    if w_is_nk:
        w_specs = [pl.BlockSpec((1, tn, k), lambda j, i, d=d: (layer_idx, j0 + j + d, 0))
                   for d in range(2 if shift else 1)]
    else:
        w_specs = [pl.BlockSpec((1, k, tn), lambda j, i: (layer_idx, 0, j0 + j))]
    return pl.pallas_call(
        functools.partial(_mm_castw_kernel, w_is_nk=w_is_nk, shift=shift),
        grid=(n // tn, m // tm),
        in_specs=[pl.BlockSpec((tm, k), lambda j, i: (i, 0))] + w_specs,
        out_specs=pl.BlockSpec((tm, tn), lambda j, i: (i, j)),
        out_shape=jax.ShapeDtypeStruct((m, n), out_dtype),
        scratch_shapes=[pltpu.VMEM((k, tn), BF16)],
        compiler_params=_params(dimension_semantics=("arbitrary", "arbitrary")),
        name="matmul_f32w",
    )(x, *([w3] * len(w_specs)))


def _kv_proj_kernel(x_ref, ws_ref, ww_ref, ks_ref, kw_ref, vs_ref, vw_ref, wk_s, wv_s, *, tm, seq):
    @pl.when(pl.program_id(0) == 0)
    def _():
        wk = jnp.concatenate([ws_ref[0, 0:KV_W], ww_ref[0, 0:KV_W]], axis=0)
        wk_s[...] = wk.T.astype(BF16)
        wv_s[...] = jnp.concatenate([ws_ref[0, KV_W:2 * KV_W], ww_ref[0, KV_W:2 * KV_W]], axis=0).astype(BF16)

    x = x_ref[...]
    kk = _dot(x, wk_s[...])
    vt = _dot_nt(wv_s[...], x)
    pos = (pl.program_id(0) * tm + lax.broadcasted_iota(jnp.int32, (tm, MASK_ROWS), 0)) % seq
    blk = lax.broadcasted_iota(jnp.int32, (tm, MASK_ROWS), 1)
    onehot = jnp.where((pos >> int(math.log2(SLC_BLOCK))) == blk, 1.0, 0.0)
    zeros = jnp.zeros((tm, MASK_ROWS), F32)
    tail = jnp.where(lax.broadcasted_iota(jnp.int32, (LANES - DK, tm), 0) == 0, 1.0, 0.0)
    for h in range(HKV):
        ks_ref[:, h * LANES:(h + 1) * LANES] = jnp.concatenate(
            [kk[:, h * DK:(h + 1) * DK], onehot], axis=1).astype(ks_ref.dtype)
        kw_ref[:, h * LANES:(h + 1) * LANES] = jnp.concatenate(
            [kk[:, KV_W + h * DK:KV_W + (h + 1) * DK], zeros], axis=1).astype(kw_ref.dtype)
        vs_ref[h] = jnp.concatenate([vt[h * DK:(h + 1) * DK], tail], axis=0).astype(vs_ref.dtype)
        vw_ref[h] = jnp.concatenate([vt[KV_W + h * DK:KV_W + (h + 1) * DK], tail], axis=0).astype(vw_ref.dtype)


def kv_projection(x, wt3, layer_idx, row0, seq, tm):
    m, k = x.shape
    nrows = 2 * KV_W
    assert m % tm == 0 and row0 % nrows == 0 and seq % tm == 0
    wspec = lambda d: pl.BlockSpec((1, nrows, k), lambda i: (layer_idx, row0 // nrows + d, 0),
                                   pipeline_mode=pl.Buffered(1))
    keys = jax.ShapeDtypeStruct((m, HKV * LANES), BF16)
    vals = jax.ShapeDtypeStruct((HKV, LANES, m), BF16)
    return pl.pallas_call(
        functools.partial(_kv_proj_kernel, tm=tm, seq=seq),
        grid=(m // tm,),
        in_specs=[pl.BlockSpec((tm, k), lambda i: (i, 0)), wspec(0), wspec(1)],
        out_specs=[pl.BlockSpec((tm, HKV * LANES), lambda i: (i, 0))] * 2
                  + [pl.BlockSpec((HKV, LANES, tm), lambda i: (0, 0, i))] * 2,
        out_shape=[keys, keys, vals, vals],
        scratch_shapes=[pltpu.VMEM((k, 2 * KV_W), BF16), pltpu.VMEM((2 * KV_W, k), BF16)],
        compiler_params=_params(dimension_semantics=("arbitrary",)),
        name="kv_projection",
    )(x, wt3, wt3)


def _gmlp_kernel(u_ref, v_ref, ga_ref, lng_ref, lnb_ref, w_ref, bs_ref, o_ref, *, rows):
    gd = D_BR // GMLP_GROUPS
    for c in range(rows // GMLP_CHUNK):
        r = slice(c * GMLP_CHUNK, (c + 1) * GMLP_CHUNK)
        v = _gelu_tanh(v_ref[r, :])
        mu = jnp.mean(v, axis=-1, keepdims=True)
        vc = v - mu
        var = jnp.mean(vc * vc, axis=-1, keepdims=True)
        vb = (vc * lax.rsqrt(var + LN_EPS) * lng_ref[...] + lnb_ref[...]).astype(BF16)
        u = _gelu_tanh(u_ref[r, :]) * _silu(ga_ref[r, :])
        for g in range(GMLP_GROUPS):
            cs = slice(g * gd, (g + 1) * gd)
            mixed = _dot(w_ref[g], vb[:, cs]) + bs_ref[:, g:g + 1]
            o_ref[r, cs] = (u[:, cs] * mixed).astype(o_ref.dtype)


def gmlp_branch(h1, ln_g, ln_b, w_s, b_s, rows=512):
    m = h1.shape[0]
    rows = min(rows, m)
    causal = jnp.tril(jnp.ones((GMLP_CHUNK, GMLP_CHUNK), dtype=bool))
    w = jnp.where(causal, w_s, 0).astype(BF16)
    col = lambda c: pl.BlockSpec((rows, D_BR), lambda i, c=c: (i, c // D_BR))
    full = lambda a: pl.BlockSpec(a.shape, lambda i: (0,) * a.ndim)
    args = (ln_g.reshape(1, D_BR), ln_b.reshape(1, D_BR), w, b_s.T)
    return pl.pallas_call(
        functools.partial(_gmlp_kernel, rows=rows),
        grid=(m // rows,),
        in_specs=[col(C1_U), col(C1_V), col(C1_GA)] + [full(a) for a in args],
        out_specs=pl.BlockSpec((rows, D_BR), lambda i: (i, 0)),
        out_shape=jax.ShapeDtypeStruct((m, D_BR), BF16),
        compiler_params=_params(),
        name="gmlp",
    )(h1, h1, h1, *args)


def _lru_kernel(xb_ref, gb_ref, cw_ref, cb_ref, wa_ref, ba_ref, wx_ref, bx_ref, lam_ref, o_ref,
                xbuf, hcarry, a_s, g_s, *, ts):
    @pl.when(pl.program_id(1) == 0)
    def _():
        xbuf[0:8, :] = jnp.zeros((8, D_BR), F32)
        hcarry[...] = jnp.zeros((8, D_BR), F32)

    xbuf[8:8 + ts, :] = xb_ref[...]
    xc = cb_ref[...] + cw_ref[0:1, :] * xbuf[pl.ds(8 - (CONV_WIDTH - 1), ts), :]
    for k in range(1, CONV_WIDTH):
        xc = xc + cw_ref[k:k + 1, :] * xbuf[pl.ds(8 - (CONV_WIDTH - 1) + k, ts), :]
    xbuf[0:8, :] = xbuf[ts:ts + 8, :]

    xcb = xc.astype(BF16)
    bd = LRU_BLOCK_DIM
    r = jnp.concatenate([_dot(xcb[:, n * bd:(n + 1) * bd], wa_ref[n]) for n in range(LRU_BLOCKS)], axis=1)
    i = jnp.concatenate([_dot(xcb[:, n * bd:(n + 1) * bd], wx_ref[n]) for n in range(LRU_BLOCKS)], axis=1)
    r = _sigmoid(r + ba_ref[...])
    i = _sigmoid(i + bx_ref[...])
    nl = -lam_ref[...]
    softplus = jnp.maximum(nl, 0.0) + jnp.log1p(jnp.exp(-jnp.abs(nl)))
    log_a = (-LRU_C * softplus) * r
    a = jnp.exp(log_a)
    a_s[...] = a
    g_s[...] = jnp.sqrt(1.0 - a * a) * i * xc

    row = lax.broadcasted_iota(jnp.int32, (8, D_BR), 0)

    def body(j, carry):
        r0 = pl.multiple_of(j * 8, 8)
        av = a_s[pl.ds(r0, 8), :]
        bv = g_s[pl.ds(r0, 8), :]
        for d in (1, 2, 4):
            keep = row >= d
            a_sh = pltpu.roll(av, d, axis=0)
            b_sh = pltpu.roll(bv, d, axis=0)
            bv = jnp.where(keep, av * b_sh + bv, bv)
            av = jnp.where(keep, av * a_sh, av)
        hv = av * carry + bv
        g_s[pl.ds(r0, 8), :] = hv
        return jnp.broadcast_to(hv[7:8, :], (8, D_BR))

    hcarry[...] = lax.fori_loop(0, ts // 8, body, hcarry[...])
    o_ref[...] = (g_s[...] * _silu(gb_ref[...])).astype(o_ref.dtype)


def lru_branch(h1, bsz, seq, conv_w, conv_b, wa, ba, wx, bx, lam, ts=512):
    ts = min(ts, seq)
    ns = seq // ts
    col = lambda c: pl.BlockSpec((ts, D_BR), lambda b, s, c=c: (b * ns + s, c // D_BR))
    full = lambda a: pl.BlockSpec(a.shape, lambda b, s: (0,) * a.ndim)
    row = lambda a: a.reshape(1, D_BR)
    args = (conv_w, row(conv_b), wa.astype(BF16), row(ba), wx.astype(BF16), row(bx), row(lam))
    return pl.pallas_call(
        functools.partial(_lru_kernel, ts=ts),
        grid=(bsz, ns),
        in_specs=[col(C1_XB), col(C1_GB)] + [full(a) for a in args],
        out_specs=pl.BlockSpec((ts, D_BR), lambda b, s: (b * ns + s, 0)),
        out_shape=jax.ShapeDtypeStruct((bsz * seq, D_BR), BF16),
        scratch_shapes=[pltpu.VMEM((ts + 8, D_BR), F32), pltpu.VMEM((8, D_BR), F32),
                        pltpu.VMEM((ts, D_BR), F32), pltpu.VMEM((ts, D_BR), F32)],
        compiler_params=_params(dimension_semantics=("arbitrary", "arbitrary")),
        name="lru",
    )(h1, h1, *args)


def _mem_kernel(q_ref, g_ref, kv_ref, o_ref):
    hd = MEM_HEAD_DIM
    for hh in range(MEM_HEADS):
        cs = slice(hh * hd, (hh + 1) * hd)
        q = (q_ref[:, cs] * (hd ** -0.5)).astype(BF16)
        s = _dot_nt(q, kv_ref[:, cs])
        p = jnp.exp(s - jnp.max(s, axis=-1, keepdims=True))
        l = jnp.sum(p, axis=-1, keepdims=True)
        o = _dot(p.astype(BF16), kv_ref[:, D_BR + hh * hd:D_BR + (hh + 1) * hd]) / l
        o_ref[:, cs] = (o * _silu(g_ref[:, cs])).astype(o_ref.dtype)


def mem_branch(h2, kv, bsz, seq, tq=512):
    tq = min(tq, seq)
    nq = seq // tq
    mlen = kv.shape[0] // bsz
    col = lambda c: pl.BlockSpec((tq, D_BR), lambda b, i, c=c: (b * nq + i, c // D_BR))
    return pl.pallas_call(
        _mem_kernel,
        grid=(bsz, nq),
        in_specs=[col(C2_QM), col(C2_GM), pl.BlockSpec((mlen, 2 * D_BR), lambda b, i: (b, 0))],
        out_specs=pl.BlockSpec((tq, D_BR), lambda b, i: (b * nq + i, 0)),
        out_shape=jax.ShapeDtypeStruct((bsz * seq, D_BR), BF16),
        compiler_params=_params(),
        name="mem_attn",
    )(h2, h2, kv)


def _cmp_kernel(*refs):
    ngrp = 2 * KV_W // LANES
    t_refs = refs[:ngrp]
    pek_ref, pev_ref, w1k_ref, w1v_ref, w2kt_ref, w2v_ref, kct_ref, vc_ref = refs[ngrp:]
    n = t_refs[0].shape[0] // CMP_STRIDE
    for h in range(HKV):
        for is_v, (pe_ref, w1_ref) in enumerate(((pek_ref, w1k_ref), (pev_ref, w1v_ref))):
            grp, off = divmod(is_v * KV_W + h * DK, LANES)
            lo = hi = None
            for tok in range(CMP_STRIDE):
                x = t_refs[grp][pl.ds(tok, n, stride=CMP_STRIDE), :][:, off:off + DK]
                t2 = CMP_STRIDE + tok
                a = _dot((x + pe_ref[tok:tok + 1, :]).astype(BF16), w1_ref[tok * DK:(tok + 1) * DK, :])
                b = _dot((x + pe_ref[t2:t2 + 1, :]).astype(BF16), w1_ref[t2 * DK:(t2 + 1) * DK, :])
                lo = a if lo is None else lo + a
                hi = b if hi is None else hi + b
            hidden = _silu(lo + pltpu.roll(hi, n - 1, axis=0)).astype(BF16)
            if is_v:
                vc_ref[h] = _dot(hidden, w2v_ref[...]).astype(vc_ref.dtype)
            else:
                kct_ref[h] = _dot_nt(w2kt_ref[...], hidden).astype(kct_ref.dtype)


def nsa_compress(h1, bsz, seq, pe_k, pe_v, w1_k, w1_v, w2_k, w2_v):
    n = seq // CMP_STRIDE
    args = (pe_k, pe_v, w1_k.astype(BF16), w1_v.astype(BF16), w2_k.T.astype(BF16), w2_v.astype(BF16))
    full = lambda a: pl.BlockSpec(a.shape, lambda b: (0,) * a.ndim)
    return pl.pallas_call(
        _cmp_kernel,
        grid=(bsz,),
        in_specs=[pl.BlockSpec((seq, LANES), lambda b, g=g: (b, C1_KV // LANES + g))
                  for g in range(2 * KV_W // LANES)] + [full(a) for a in args],
        out_specs=[pl.BlockSpec((HKV, DK, n), lambda b: (b, 0, 0)), pl.BlockSpec((HKV, n, DK), lambda b: (b, 0, 0))],
        out_shape=[jax.ShapeDtypeStruct((bsz * HKV, DK, n), BF16), jax.ShapeDtypeStruct((bsz * HKV, n, DK), BF16)],
        compiler_params=_params(),
        name="nsa_compress",
    )(*([h1] * (2 * KV_W // LANES)), *args)


def _nsa_select_kernel(q_ref, gl_ref, gb_ref, kct_ref, vc_ref, bc_ref, ovt_ref, qsel_ref, ocg_ref, vt_s):
    qb = Q_BLOCK
    lane = lax.broadcasted_iota(jnp.int32, (qb, LANES), 1)
    jrow = lax.broadcasted_iota(jnp.int32, (MASK_ROWS, qb), 0).astype(F32)
    qlane = lax.broadcasted_iota(jnp.int32, (MASK_ROWS, qb), 1)
    sub = lax.broadcasted_iota(jnp.int32, (SUBLANES, qb), 0)
    nv = MASK_ROWS // SUBLANES
    ovt = ovt_ref[...]
    kct = kct_ref[0]
    vc = vc_ref[0]
    for sb in range(SEL_SUB):
        rows = slice(sb * qb, (sb + 1) * qb)
        blk = pl.program_id(2) * SEL_SUB + sb
        q = q_ref[rows, :] * (DK ** -0.5)
        gt = _sigmoid(gl_ref[rows, :] + gb_ref[...])
        qpad = []
        for g in range(GQA):
            t = q[:, (g // 2) * LANES:(g // 2 + 1) * LANES]
            if g % 2:
                t = pltpu.roll(t, DK, axis=1)
            qpad.append(jnp.where(lane < DK, t, 0.0))

        has_keys = blk * qb + lax.broadcasted_iota(jnp.int32, (qb, 1), 0) >= CMP_BLOCK - 1
        o_c = []
        psum = None
        for g in range(GQA):
            s = _dot(qpad[g][:, :DK].astype(BF16), kct) + bc_ref[0, sb, g * qb:(g + 1) * qb, :]
            m = jnp.max(s, axis=-1, keepdims=True)
            p = jnp.exp(s - m)
            p = p * jnp.where(has_keys, 1.0 / jnp.maximum(jnp.sum(p, axis=-1, keepdims=True), 1e-30), 0.0)
            o_c.append(gt[:, g:g + 1] * _dot(p.astype(BF16), vc))
            psum = p if psum is None else psum + p
        ocg_ref[rows, :] = jnp.concatenate(o_c, axis=1)

        p_hi = psum.astype(BF16)
        rem = psum - p_hi.astype(F32)
        p_mid = rem.astype(BF16)
        p_lo = (rem - p_mid.astype(F32)).astype(BF16)
        imp_t = _dot_nt(ovt, p_hi) + _dot_nt(ovt, p_mid) + _dot_nt(ovt, p_lo)

        qblk = ((blk * qb + qlane) >> int(math.log2(SLC_BLOCK))).astype(F32)
        val = jnp.where(jrow == 0.0, 3e38,
                        jnp.where(jrow == qblk, 3e38, jnp.where(jrow == qblk - 1.0, 3e38, imp_t)))
        val = jnp.where(jrow > qblk, -1.0, val)
        vt_s[sb] = val
        vals = [val[k * SUBLANES:(k + 1) * SUBLANES] for k in range(nv)]
        cnt = [jnp.zeros((SUBLANES, qb), F32) for _ in range(nv)]
        for jp in range(MASK_ROWS):
            rowv = jnp.broadcast_to(vt_s[sb, jp:jp + 1, :], (SUBLANES, qb))
            for k in range(nv):
                if k * SUBLANES > jp:
                    beat = jnp.where(rowv >= vals[k], 1.0, 0.0)
                elif (k + 1) * SUBLANES - 1 < jp:
                    beat = jnp.where(rowv > vals[k], 1.0, 0.0)
                else:
                    beat = jnp.where(sub > jp - k * SUBLANES, jnp.where(rowv >= vals[k], 1.0, 0.0),
                                     jnp.where(rowv > vals[k], 1.0, 0.0))
                cnt[k] = cnt[k] + beat
        selneg_t = jnp.concatenate([jnp.where(c < float(SLC_TOPK), 0.0, NEG) for c in cnt], axis=0)
        for g in range(GQA):
            qt = jnp.concatenate([qpad[g].T[:DK], selneg_t], axis=0)
            qsel_ref[0, sb, :, g * qb:(g + 1) * qb] = qt.astype(qsel_ref.dtype)


def _nsa_attn_kernel(qsel_ref, ocg_ref, gc_ref, gl_ref, gb_ref, ks_ref, vs_ref, gs_ref, kw_ref, vw_ref, bw_ref,
                     o_ref, s_s, p_s):
    qb = Q_BLOCK
    hq = GQA * qb
    nq = ATT_NQ * hq
    i = pl.program_id(2) * ATT_NQ
    step = SEL_TK // qb
    n_tiles = (i * qb) // SEL_TK + 1
    qt = jnp.concatenate([qsel_ref[0, b] for b in range(ATT_NQ)], axis=1)

    def stage_qk(j):
        k0 = pl.multiple_of(j * SEL_TK, SEL_TK)
        sc = _dot(ks_ref[pl.ds(k0, SEL_TK), :], qt)
        for b in range(ATT_NQ):
            dd = jnp.minimum(i + b - j * step, SEL_FAR)
            boff = pl.multiple_of((SEL_FAR - dd) * qb, qb)
            s_s[:, b * hq:(b + 1) * hq] = sc[:, b * hq:(b + 1) * hq] + gs_ref[0, pl.ds(boff, SEL_TK), :]

    def stage_softmax(m):
        m_new, alpha = [], []
        for g in range(ATT_NQ * GQA):
            cs = slice(g * qb, (g + 1) * qb)
            sc = s_s[:, cs]
            mg = jnp.maximum(m[:, cs], jnp.max(sc, axis=0, keepdims=True))
            alpha.append(jnp.exp(m[:, cs] - mg))
            p_s[:, cs] = jnp.exp(sc - mg).astype(BF16)
            m_new.append(mg)
        return jnp.concatenate(m_new, axis=1), jnp.concatenate(alpha, axis=1)

    def stage_pv(j, acc):
        k0 = pl.multiple_of(jnp.maximum(j, 0) * SEL_TK, SEL_TK)
        return acc + _dot(vs_ref[0, :V_ROWS, pl.ds(k0, SEL_TK)], p_s[...])

    def advance(j, m, acc):
        acc = stage_pv(j - 2, acc)
        m, alpha = stage_softmax(m)
        return m, alpha * acc

    def body(j, carry):
        m, acc = advance(j, *carry)
        stage_qk(j)
        return m, acc

    p_s[...] = jnp.zeros(p_s.shape, BF16)
    stage_qk(jnp.int32(0))
    init = (jnp.full((1, nq), NEG, F32), jnp.zeros((V_ROWS, nq), F32))
    m, acc = lax.fori_loop(1, n_tiles, body, init)
    _, acc = advance(n_tiles, m, acc)
    acc_s = stage_pv(n_tiles - 1, acc)

    wk = WINDOW + ATT_NQ * qb
    w0 = pl.multiple_of(jnp.maximum(i * qb - WINDOW, 0), qb)
    rowi = lax.broadcasted_iota(jnp.int32, (LANES, nq), 0)
    q_win = jnp.where(rowi < DK, qt, jnp.zeros_like(qt))
    sw = _dot(kw_ref[pl.ds(w0, wk), :], q_win) + bw_ref[0, 0]
    pw = jnp.exp(sw - jnp.max(sw, axis=0, keepdims=True)).astype(BF16)
    acc_w = _dot(vw_ref[0, :V_ROWS, pl.ds(w0, wk)], pw)

    pad_rows = jnp.zeros((LANES - V_ROWS, qb), F32)
    for b in range(ATT_NQ):
        rows = slice(b * qb, (b + 1) * qb)
        gt = _sigmoid(gl_ref[rows, :] + gb_ref[...])
        outs = []
        for g in range(GQA):
            cs = slice(b * hq + g * qb, b * hq + (g + 1) * qb)
            a_w = jnp.concatenate([acc_w[:, cs], pad_rows], axis=0).T
            a_s = jnp.concatenate([acc_s[:, cs], pad_rows], axis=0).T
            o_w = a_w[:, :DK] / a_w[:, DK:DK + 1]
            o_s = a_s[:, :DK] / a_s[:, DK:DK + 1]
            outs.append(gt[:, GQA + g:GQA + g + 1] * o_s + gt[:, 2 * GQA + g:2 * GQA + g + 1] * o_w)
        o = ocg_ref[rows, :] + jnp.concatenate(outs, axis=1)
        o_ref[rows, :] = (o * _silu(gc_ref[rows, :])).astype(o_ref.dtype)


def _rel_bucket(dist):
    n = jnp.maximum(dist, 0)
    exact = REL_BUCKETS // 2
    nf = jnp.maximum(n, 1).astype(jnp.float32)
    large = exact + (jnp.log(nf / exact) / math.log(REL_MAX_DIST / exact)
                     * (REL_BUCKETS - exact)).astype(jnp.int32)
    return jnp.where(n < exact, n, jnp.minimum(large, REL_BUCKETS - 1))


def nsa_bias_tables(rel_bias, seq):
    qb = Q_BLOCK
    n_pad = seq // CMP_STRIDE
    c0 = n_pad - NA
    wc = 2 * n_pad
    ws = SEL_FAR * qb + SEL_TK
    nmax = max(seq, SEL_FAR * qb + qb)
    padl = CMP_STRIDE * (wc + 2)
    bvec = rel_bias[_rel_bucket(jnp.arange(nmax))].T.astype(F32)
    vext = jnp.concatenate([jnp.full((NSA_HEADS, padl), NEG, F32), bvec], axis=1)
    pos = jnp.arange(padl + nmax) - padl
    vwin = jnp.where(pos < WINDOW, vext, NEG)

    mlo, mhi = c0 - wc + 1, NA - 1 + c0
    start = padl + CMP_STRIDE * mlo - (CMP_BLOCK - 1)
    assert start >= 0
    u = vext[:, start:start + CMP_STRIDE * (mhi - mlo + 1)].reshape(NSA_HEADS, mhi - mlo + 1, CMP_STRIDE)
    urev = u[:, ::-1, :]
    gcb = jnp.stack([urev[:, NA - 1 - a:NA - 1 - a + wc, :] for a in range(NA)], axis=1)
    gcb = gcb.transpose(0, 1, 3, 2).reshape(HKV, GQA * qb, wc)
    nqb = seq // qb
    bc = jnp.stack([gcb[:, :, c0 - NA * i:c0 - NA * i + n_pad] for i in range(nqb)], axis=1)

    def toeplitz(vec, d0, width):
        vrev = vec[:, ::-1]
        s0 = nmax - d0 - qb
        period = width + qb - 1
        assert s0 >= 0 and s0 + period <= padl + nmax
        sl = vrev[:, s0:s0 + period]
        y = jnp.concatenate([sl[:, qb - 1:], sl[:, :qb - 1]], axis=1)
        z = jnp.tile(y, (1, qb))[:, :qb * (period - 1)].reshape(NSA_HEADS, qb, period - 1)
        return z[:, :, :width].reshape(HKV, GQA * qb, width)

    gs = toeplitz(vext, SEL_FAR * qb, ws)
    bw = tuple(tuple(toeplitz(vwin, first + b * qb, WINDOW + ATT_NQ * qb) for b in range(ATT_NQ))
               for first in (0, WINDOW))
    return bc, gs, bw


def _overlap_matrix_t(n_pad):
    c_start = np.arange(n_pad)[None, :] * CMP_STRIDE
    s_start = np.arange(MASK_ROWS)[:, None] * SLC_BLOCK
    ov = np.clip(np.minimum(c_start + CMP_BLOCK, s_start + SLC_BLOCK) - np.maximum(c_start, s_start), 0, None)
    return ov.astype(np.float32) / CMP_BLOCK


def nsa_branch(h1, h2, kv, hgl, gate_b4, kct, vc, tables, bsz, seq):
    qb = Q_BLOCK
    nqb = seq // qb
    n_pad = seq // CMP_STRIDE
    n_slc = seq // SLC_BLOCK
    assert n_slc <= MASK_ROWS and seq % SEL_TK == 0 and nqb % SEL_SUB == 0
    assert (SEL_TK // qb) % ATT_NQ == 0 and nqb % ATT_NQ == 0
    bc, gs, bw = tables
    m = bsz * seq
    gw = GQA * DK

    sq = SEL_SUB * qb
    nsq = seq // sq
    ovt = jnp.asarray(_overlap_matrix_t(n_pad), BF16)
    qsel, ocg = pl.pallas_call(
        _nsa_select_kernel,
        grid=(HKV, bsz, nsq),
        in_specs=[
            pl.BlockSpec((sq, gw), lambda hh, b, i: (b * nsq + i, C1_QC // gw + hh)),
            pl.BlockSpec((sq, LANES), lambda hh, b, i: (b * nsq + i, hh)),
            pl.BlockSpec((1, LANES), lambda hh, b, i: (0, hh)),
            pl.BlockSpec((1, DK, n_pad), lambda hh, b, i: (b * HKV + hh, 0, 0)),
            pl.BlockSpec((1, n_pad, DK), lambda hh, b, i: (b * HKV + hh, 0, 0)),
            pl.BlockSpec((1, SEL_SUB) + bc.shape[2:], lambda hh, b, i: (hh, i, 0, 0)),
            pl.BlockSpec(ovt.shape, lambda hh, b, i: (0, 0)),
        ],
        out_specs=[pl.BlockSpec((1, SEL_SUB, LANES, GQA * qb), lambda hh, b, i: (hh, b * nsq + i, 0, 0)),
                   pl.BlockSpec((sq, gw), lambda hh, b, i: (b * nsq + i, hh))],
        out_shape=[jax.ShapeDtypeStruct((HKV, bsz * nqb, LANES, GQA * qb), BF16),
                   jax.ShapeDtypeStruct((m, D_BR), F32)],
        scratch_shapes=[pltpu.VMEM((SEL_SUB, MASK_ROWS, qb), F32)],
        compiler_params=_params(),
        name="nsa_select",
    )(h1, hgl, gate_b4, kct, vc, bc, ovt)

    gs_t = gs.transpose(0, 2, 1)
    bw_t = jnp.stack([jnp.concatenate([w.transpose(0, 2, 1) for w in variant], axis=2) for variant in bw],
                     axis=1)
    ks_ext, kw_ext, vs_ext, vw_ext = kv

    aq = ATT_NQ * qb
    npair = nqb // ATT_NQ
    rowblk = lambda width, c0: pl.BlockSpec((aq, width), lambda hh, b, i: (b * npair + i, c0 // width + hh))
    keys = pl.BlockSpec((seq, LANES), lambda hh, b, i: (b, hh))
    vals = pl.BlockSpec((1, LANES, seq), lambda hh, b, i: (hh, 0, b))
    return pl.pallas_call(
        _nsa_attn_kernel,
        grid=(HKV, bsz, npair),
        in_specs=[
            pl.BlockSpec((1, ATT_NQ, LANES, GQA * qb), lambda hh, b, i: (hh, b * npair + i, 0, 0)),
            rowblk(gw, 0), rowblk(gw, C2_GC),
            pl.BlockSpec((aq, LANES), lambda hh, b, i: (b * npair + i, hh)),
            pl.BlockSpec((1, LANES), lambda hh, b, i: (0, hh)),
            keys, vals, pl.BlockSpec((1,) + gs_t.shape[1:], lambda hh, b, i: (hh, 0, 0)),
            keys, vals, pl.BlockSpec((1, 1) + bw_t.shape[2:], lambda hh, b, i: (hh, jnp.minimum(i, 1), 0, 0)),
        ],
        out_specs=pl.BlockSpec((aq, gw), lambda hh, b, i: (b * npair + i, hh)),
        out_shape=jax.ShapeDtypeStruct((m, D_BR), BF16),
        scratch_shapes=[pltpu.VMEM((SEL_TK, ATT_NQ * GQA * qb), F32), pltpu.VMEM((SEL_TK, ATT_NQ * GQA * qb), BF16)],
        compiler_params=_params(),
        name="nsa_attn",
    )(qsel, ocg, h2, hgl, gate_b4, ks_ext, vs_ext, gs_t, kw_ext, vw_ext, bw_t)


def _final_kernel(oa_ref, ob_ref, oc_ref, om_ref, *rest):
    gate_refs = rest[:2 * N_BRANCHES]
    x_ref, wb_ref, wo_ref, lng_ref, lnb_ref, y_ref, yb_ref = rest[2 * N_BRANCHES:]
    o_refs = (oa_ref, ob_ref, oc_ref, om_ref)
    halves = []
    for c in range(2):
        cs = slice(c * D_BR, (c + 1) * D_BR)
        acc = None
        for k in range(N_BRANCHES):
            term = _sigmoid(gate_refs[2 * k + c][...]) * _dot(o_refs[k][...], wb_ref[k, :, cs])
            acc = term if acc is None else acc + term
        halves.append(acc.astype(BF16))
    merged = jnp.concatenate(halves, axis=1)
    z = DN_ALPHA * x_ref[...] + _dot(merged, wo_ref[...])
    mu = jnp.mean(z, axis=-1, keepdims=True)
    zc = z - mu
    var = jnp.mean(zc * zc, axis=-1, keepdims=True)
    y = zc * lax.rsqrt(var + LN_EPS) * lng_ref[...] + lnb_ref[...]
    y_ref[...] = y
    yb_ref[...] = y.astype(BF16)


def final_merge(o_a, o_b, o_c, o_m, h2, x, w_branch, w_out, ln_g, ln_b, tm=256):
    m = x.shape[0]
    tm = min(tm, m)
    br = pl.BlockSpec((tm, D_BR), lambda i: (i, 0))
    gate = lambda k: pl.BlockSpec((tm, D_BR), lambda i, k=k: (i, C2_MERGE // D_BR + k))
    xs = pl.BlockSpec((tm, D_MODEL), lambda i: (i, 0))
    resident = lambda a: pl.BlockSpec(a.shape, lambda i: (0,) * a.ndim, pipeline_mode=pl.Buffered(1))
    wb = w_branch.astype(BF16)
    wo = w_out.astype(BF16)
    lg, lb = ln_g.reshape(1, D_MODEL), ln_b.reshape(1, D_MODEL)
    ngate = 2 * N_BRANCHES
    return pl.pallas_call(
        _final_kernel,
        grid=(m // tm,),
        in_specs=[br, br, br, br] + [gate(k) for k in range(ngate)] + [xs]
                 + [resident(wb), resident(wo), resident(lg), resident(lb)],
        out_specs=[xs, xs],
        out_shape=[jax.ShapeDtypeStruct((m, D_MODEL), F32), jax.ShapeDtypeStruct((m, D_MODEL), BF16)],
        compiler_params=pltpu.CompilerParams(vmem_limit_bytes=VMEM_LIMIT_MERGE),
        name="merge_out_ln",
    )(o_a, o_b, o_c, o_m, *([h2] * ngate), x, wb, wo, lg, lb)


def _gate_spread_matrix():
    p = np.zeros((3 * NSA_HEADS, HKV * LANES), np.float32)
    for hh in range(HKV):
        for brn in range(3):
            for g in range(GQA):
                p[brn * NSA_HEADS + hh * GQA + g, hh * LANES + brn * GQA + g] = 1.0
    return p


def _spread_gate_cols(a):
    return jnp.dot(a, jnp.asarray(_gate_spread_matrix()), precision=lax.Precision.HIGHEST)


def layer(l, x, xb, mem_b, tables, bsz, seq, w_in, sgu_ln_g, sgu_ln_b, sgu_w, sgu_b, conv_w, conv_b,
          lru_wa, lru_ba, lru_wx, lru_bx, lru_lambda, cmp_pe_k, cmp_pe_v, cmp_w1_k, cmp_w1_v,
          cmp_w2_k, cmp_w2_v, nsa_gate_b, w_mem_kv, w_branch, w_out, ln_g, ln_b):
    m = bsz * seq
    tm = min(1024, m)
    tm_big = min(2048, m)
    h1 = matmul_f32w(xb, w_in, l, 0, N1, F32, tm_big, 512, w_is_nk=True)
    kv = kv_projection(xb, w_in, l, N1, seq, tm)
    h2 = matmul_f32w(xb, w_in, l, GL_OFF + 3 * NSA_HEADS, N2, F32, tm_big, 512, w_is_nk=True)
    w_gl = _spread_gate_cols(w_in[l, GL_OFF:GL_OFF + 3 * NSA_HEADS, :].T).astype(BF16)
    hgl = matmul(xb, w_gl, F32, tm, HKV * LANES)
    gate_b4 = _spread_gate_cols(nsa_gate_b[l].reshape(1, 3 * NSA_HEADS))

    o_a = gmlp_branch(h1, sgu_ln_g[l], sgu_ln_b[l], sgu_w[l], sgu_b[l])
    o_b = lru_branch(h1, bsz, seq, conv_w[l], conv_b[l], lru_wa[l], lru_ba[l], lru_wx[l], lru_bx[l],
                     lru_lambda[l])

    kct, vc = nsa_compress(h1, bsz, seq, cmp_pe_k[l], cmp_pe_v[l],
                           cmp_w1_k[l], cmp_w1_v[l], cmp_w2_k[l], cmp_w2_v[l])
    o_c = nsa_branch(h1, h2, kv, hgl, gate_b4, kct, vc, tables, bsz, seq)

    mrows = mem_b.shape[0]
    kv = matmul_f32w(mem_b, w_mem_kv, l, 0, 2 * D_BR, BF16, min(512, mrows), 512)
    o_m = mem_branch(h2, kv, bsz, seq)
    return final_merge(o_a, o_b, o_c, o_m, h2, x, w_branch[l], w_out[l], ln_g[l], ln_b[l])


def kernel(x, mem, rel_bias, w_in, sgu_ln_g, sgu_ln_b, sgu_w, sgu_b, conv_w, conv_b, lru_wa, lru_ba, lru_wx,
           lru_bx, lru_lambda, cmp_pe_k, cmp_pe_v, cmp_w1_k, cmp_w1_v, cmp_w2_k, cmp_w2_v, nsa_gate_b,
           w_mem_kv, w_branch, w_out, ln_g, ln_b):
    bsz, seq, _ = x.shape
    tables = nsa_bias_tables(rel_bias, seq)
    xf = x.reshape(bsz * seq, D_MODEL)
    xb = xf.astype(BF16)
    mem_b = mem.reshape(-1, D_MODEL).astype(BF16)
    w_in_t = jnp.swapaxes(w_in, 1, 2)
    params = (w_in_t, sgu_ln_g, sgu_ln_b, sgu_w, sgu_b, conv_w, conv_b, lru_wa, lru_ba, lru_wx, lru_bx,
              lru_lambda, cmp_pe_k, cmp_pe_v, cmp_w1_k, cmp_w1_v, cmp_w2_k, cmp_w2_v, nsa_gate_b,
              w_mem_kv, w_branch, w_out, ln_g, ln_b)
    for l in range(w_in.shape[0]):
        xf, xb = layer(l, xf, xb, mem_b, tables, bsz, seq, *params)
    return xf.reshape(bsz, seq, D_MODEL)
```

```python
import functools
import math

import numpy as np
import jax
import jax.numpy as jnp
from jax import lax
from jax.experimental import pallas as pl
from jax.experimental.pallas import tpu as pltpu

F32 = jnp.float32
BF16 = jnp.bfloat16

D_MODEL = 2048
DEPTH = 2
D_BR = D_MODEL // 2
N_BRANCHES = 4
GMLP_CHUNK = 128
GMLP_GROUPS = 8
LRU_BLOCKS = 8
LRU_BLOCK_DIM = D_BR // LRU_BLOCKS
CONV_WIDTH = 4
LRU_C = 8.0
DK = 64
NSA_HEADS = D_BR // DK
HKV = NSA_HEADS // 4
GQA = NSA_HEADS // HKV
KV_W = HKV * DK
CMP_BLOCK = 32
CMP_STRIDE = 16
CMP_HIDDEN = 256
SLC_BLOCK = 64
SLC_TOPK = 8
WINDOW = 256
Q_BLOCK = 128
MEM_HEADS = 4
MEM_HEAD_DIM = D_BR // MEM_HEADS
REL_BUCKETS = 32
REL_MAX_DIST = 1024
DN_ALPHA = (2 * DEPTH) ** 0.25
LN_EPS = 1e-5

LANES = 128
SUBLANES = 8
NEG = -1e30
SEL_TK = 512
SEL_FAR = 11
MASK_ROWS = LANES - DK
NA = Q_BLOCK // CMP_STRIDE
SEL_SUB = 4
ATT_NQ = 2
V_ROWS = DK + 16
VMEM_LIMIT = 56 * 1024 * 1024

C1_U, C1_V, C1_GA, C1_XB, C1_GB, C1_QC = (D_BR * k for k in range(6))
C1_KV = 6 * D_BR
N1 = C1_KV + 2 * KV_W
NKV = 4 * KV_W
C2_GC, C2_QM, C2_GM, C2_MERGE = (D_BR * k for k in range(4))
N2 = C2_MERGE + N_BRANCHES * D_MODEL
GL_OFF = N1 + NKV


def _sigmoid(x):
    return 1.0 / (1.0 + jnp.exp(-x))


def _silu(x):
    return x * _sigmoid(x)


def _gelu_tanh(x):
    return 0.5 * x * (1.0 + jnp.tanh(math.sqrt(2.0 / math.pi) * (x + 0.044715 * (x * x * x))))


def _dot(a, b):
    return jnp.dot(a, b, preferred_element_type=F32)


def _dot_nt(a, b):
    return lax.dot_general(a, b, (((1,), (1,)), ((), ())), preferred_element_type=F32)


def _params(**kw):
    return pltpu.CompilerParams(vmem_limit_bytes=VMEM_LIMIT, **kw)


def _mm_kernel(x_ref, w_ref, o_ref):
    o_ref[...] = _dot(x_ref[...], w_ref[...]).astype(o_ref.dtype)


def matmul(x, w, out_dtype, tm, tn):
    m, k = x.shape
    n = w.shape[1]
    assert m % tm == 0 and n % tn == 0, (m, n, tm, tn)
    return pl.pallas_call(
        _mm_kernel,
        grid=(n // tn, m // tm),
        in_specs=[pl.BlockSpec((tm, k), lambda j, i: (i, 0)),
                  pl.BlockSpec((k, tn), lambda j, i: (0, j))],
        out_specs=pl.BlockSpec((tm, tn), lambda j, i: (i, j)),
        out_shape=jax.ShapeDtypeStruct((m, n), out_dtype),
        compiler_params=_params(),
        name="matmul",
    )(x, w)


def _mm_castw_kernel(x_ref, *rest, w_is_nk, shift):
    w_refs, (o_ref, wb_ref) = rest[:-2], rest[-2:]

    @pl.when(pl.program_id(1) == 0)
    def _():
        if shift:
            w = jnp.concatenate([w_refs[0][0, shift:, :], w_refs[1][0, :shift, :]], axis=0)
        else:
            w = w_refs[0][0]
        wb_ref[...] = (w.T if w_is_nk else w).astype(BF16)

    o_ref[...] = _dot(x_ref[...], wb_ref[...]).astype(o_ref.dtype)


def matmul_f32w(x, w3, layer_idx, col0, n, out_dtype, tm, tn, w_is_nk=False, w_single_buffer=False):
    m, k = x.shape
    ncols = w3.shape[1] if w_is_nk else w3.shape[2]
    shift = col0 % tn
    assert m % tm == 0 and n % tn == 0 and col0 + n <= ncols, (m, n, col0, tm, tn)
    assert shift == 0 or (w_is_nk and shift % SUBLANES == 0), (col0, tn)
    j0 = col0 // tn
    mode = dict(pipeline_mode=pl.Buffered(1)) if w_single_buffer else {}
    if w_is_nk:
        w_specs = [pl.BlockSpec((1, tn, k), lambda j, i, d=d: (layer_idx, j0 + j + d, 0), **mode)
                   for d in range(2 if shift else 1)]
    else:
        w_specs = [pl.BlockSpec((1, k, tn), lambda j, i: (layer_idx, 0, j0 + j), **mode)]
    return pl.pallas_call(
        functools.partial(_mm_castw_kernel, w_is_nk=w_is_nk, shift=shift),
        grid=(n // tn, m // tm),
        in_specs=[pl.BlockSpec((tm, k), lambda j, i: (i, 0))] + w_specs,
        out_specs=pl.BlockSpec((tm, tn), lambda j, i: (i, j)),
        out_shape=jax.ShapeDtypeStruct((m, n), out_dtype),
        scratch_shapes=[pltpu.VMEM((k, tn), BF16)],
        compiler_params=_params(dimension_semantics=("arbitrary", "arbitrary")),
        name="matmul_f32w",
    )(x, *([w3] * len(w_specs)))


def _kv_proj_kernel(x_ref, ws_ref, ww_ref, ks_ref, kw_ref, vs_ref, vw_ref, wk_s, wv_s, *, tm, seq):
    @pl.when(pl.program_id(0) == 0)
    def _():
        wk = jnp.concatenate([ws_ref[0, 0:KV_W], ww_ref[0, 0:KV_W]], axis=0)
        wk_s[...] = wk.T.astype(BF16)
        wv_s[...] = jnp.concatenate([ws_ref[0, KV_W:2 * KV_W], ww_ref[0, KV_W:2 * KV_W]], axis=0).astype(BF16)

    x = x_ref[...]
    kk = _dot(x, wk_s[...])
    vt = _dot_nt(wv_s[...], x)
    pos = (pl.program_id(0) * tm + lax.broadcasted_iota(jnp.int32, (tm, MASK_ROWS), 0)) % seq
    blk = lax.broadcasted_iota(jnp.int32, (tm, MASK_ROWS), 1)
    onehot = jnp.where((pos >> int(math.log2(SLC_BLOCK))) == blk, 1.0, 0.0)
    zeros = jnp.zeros((tm, MASK_ROWS), F32)
    tail = jnp.where(lax.broadcasted_iota(jnp.int32, (LANES - DK, tm), 0) == 0, 1.0, 0.0)
    for h in range(HKV):
        ks_ref[:, h * LANES:(h + 1) * LANES] = jnp.concatenate(
            [kk[:, h * DK:(h + 1) * DK], onehot], axis=1).astype(ks_ref.dtype)
        kw_ref[:, h * LANES:(h + 1) * LANES] = jnp.concatenate(
            [kk[:, KV_W + h * DK:KV_W + (h + 1) * DK], zeros], axis=1).astype(kw_ref.dtype)
        vs_ref[h] = jnp.concatenate([vt[h * DK:(h + 1) * DK], tail], axis=0).astype(vs_ref.dtype)
        vw_ref[h] = jnp.concatenate([vt[KV_W + h * DK:KV_W + (h + 1) * DK], tail], axis=0).astype(vw_ref.dtype)


def kv_projection(x, wt3, layer_idx, row0, seq, tm):
    m, k = x.shape
    nrows = 2 * KV_W
    assert m % tm == 0 and row0 % nrows == 0 and seq % tm == 0
    wspec = lambda d: pl.BlockSpec((1, nrows, k), lambda i: (layer_idx, row0 // nrows + d, 0),
                                   pipeline_mode=pl.Buffered(1))
    keys = jax.ShapeDtypeStruct((m, HKV * LANES), BF16)
    vals = jax.ShapeDtypeStruct((HKV, LANES, m), BF16)
    return pl.pallas_call(
        functools.partial(_kv_proj_kernel, tm=tm, seq=seq),
        grid=(m // tm,),
        in_specs=[pl.BlockSpec((tm, k), lambda i: (i, 0)), wspec(0), wspec(1)],
        out_specs=[pl.BlockSpec((tm, HKV * LANES), lambda i: (i, 0))] * 2
                  + [pl.BlockSpec((HKV, LANES, tm), lambda i: (0, 0, i))] * 2,
        out_shape=[keys, keys, vals, vals],
        scratch_shapes=[pltpu.VMEM((k, 2 * KV_W), BF16), pltpu.VMEM((2 * KV_W, k), BF16)],
        compiler_params=_params(dimension_semantics=("arbitrary",)),
        name="kv_projection",
    )(x, wt3, wt3)


def _gmlp_kernel(u_ref, v_ref, ga_ref, lng_ref, lnb_ref, w_ref, bs_ref, o_ref, *, rows):
    gd = D_BR // GMLP_GROUPS
    for c in range(rows // GMLP_CHUNK):
        r = slice(c * GMLP_CHUNK, (c + 1) * GMLP_CHUNK)
        v = _gelu_tanh(v_ref[r, :])
        mu = jnp.mean(v, axis=-1, keepdims=True)
        vc = v - mu
        var = jnp.mean(vc * vc, axis=-1, keepdims=True)
        vb = (vc * lax.rsqrt(var + LN_EPS) * lng_ref[...] + lnb_ref[...]).astype(BF16)
        u = _gelu_tanh(u_ref[r, :]) * _silu(ga_ref[r, :])
        for g in range(GMLP_GROUPS):
            cs = slice(g * gd, (g + 1) * gd)
            mixed = _dot(w_ref[g], vb[:, cs]) + bs_ref[:, g:g + 1]
            o_ref[r, cs] = (u[:, cs] * mixed).astype(o_ref.dtype)


def gmlp_branch(h1, ln_g, ln_b, w_s, b_s, rows=512):
    m = h1.shape[0]
    rows = min(rows, m)
    causal = jnp.tril(jnp.ones((GMLP_CHUNK, GMLP_CHUNK), dtype=bool))
    w = jnp.where(causal, w_s, 0).astype(BF16)
    col = lambda c: pl.BlockSpec((rows, D_BR), lambda i, c=c: (i, c // D_BR))
    full = lambda a: pl.BlockSpec(a.shape, lambda i: (0,) * a.ndim)
    args = (ln_g.reshape(1, D_BR), ln_b.reshape(1, D_BR), w, b_s.T)
    return pl.pallas_call(
        functools.partial(_gmlp_kernel, rows=rows),
        grid=(m // rows,),
        in_specs=[col(C1_U), col(C1_V), col(C1_GA)] + [full(a) for a in args],
        out_specs=pl.BlockSpec((rows, D_BR), lambda i: (i, 0)),
        out_shape=jax.ShapeDtypeStruct((m, D_BR), BF16),
        compiler_params=_params(),
        name="gmlp",
    )(h1, h1, h1, *args)


def _lru_kernel(xb_ref, gb_ref, cw_ref, cb_ref, wa_ref, ba_ref, wx_ref, bx_ref, lam_ref, o_ref,
                xbuf, hcarry, a_s, g_s, *, ts):
    @pl.when(pl.program_id(1) == 0)
    def _():
        xbuf[0:8, :] = jnp.zeros((8, D_BR), F32)
        hcarry[...] = jnp.zeros((8, D_BR), F32)

    xbuf[8:8 + ts, :] = xb_ref[...]
    xc = cb_ref[...] + cw_ref[0:1, :] * xbuf[pl.ds(8 - (CONV_WIDTH - 1), ts), :]
    for k in range(1, CONV_WIDTH):
        xc = xc + cw_ref[k:k + 1, :] * xbuf[pl.ds(8 - (CONV_WIDTH - 1) + k, ts), :]
    xbuf[0:8, :] = xbuf[ts:ts + 8, :]

    xcb = xc.astype(BF16)
    bd = LRU_BLOCK_DIM
    r = jnp.concatenate([_dot(xcb[:, n * bd:(n + 1) * bd], wa_ref[n]) for n in range(LRU_BLOCKS)], axis=1)
    i = jnp.concatenate([_dot(xcb[:, n * bd:(n + 1) * bd], wx_ref[n]) for n in range(LRU_BLOCKS)], axis=1)
    r = _sigmoid(r + ba_ref[...])
    i = _sigmoid(i + bx_ref[...])
    nl = -lam_ref[...]
    softplus = jnp.maximum(nl, 0.0) + jnp.log1p(jnp.exp(-jnp.abs(nl)))
    log_a = (-LRU_C * softplus) * r
    a = jnp.exp(log_a)
    a_s[...] = a
    g_s[...] = jnp.sqrt(1.0 - a * a) * i * xc

    row = lax.broadcasted_iota(jnp.int32, (8, D_BR), 0)

    def body(j, carry):
        r0 = pl.multiple_of(j * 8, 8)
        av = a_s[pl.ds(r0, 8), :]
        bv = g_s[pl.ds(r0, 8), :]
        for d in (1, 2, 4):
            keep = row >= d
            a_sh = pltpu.roll(av, d, axis=0)
            b_sh = pltpu.roll(bv, d, axis=0)
            bv = jnp.where(keep, av * b_sh + bv, bv)
            av = jnp.where(keep, av * a_sh, av)
        hv = av * carry + bv
        g_s[pl.ds(r0, 8), :] = hv
        return jnp.broadcast_to(hv[7:8, :], (8, D_BR))

    hcarry[...] = lax.fori_loop(0, ts // 8, body, hcarry[...])
    o_ref[...] = (g_s[...] * _silu(gb_ref[...])).astype(o_ref.dtype)


def lru_branch(h1, bsz, seq, conv_w, conv_b, wa, ba, wx, bx, lam, ts=512):
    ts = min(ts, seq)
    ns = seq // ts
    col = lambda c: pl.BlockSpec((ts, D_BR), lambda b, s, c=c: (b * ns + s, c // D_BR))
    full = lambda a: pl.BlockSpec(a.shape, lambda b, s: (0,) * a.ndim)
    row = lambda a: a.reshape(1, D_BR)
    args = (conv_w, row(conv_b), wa.astype(BF16), row(ba), wx.astype(BF16), row(bx), row(lam))
    return pl.pallas_call(
        functools.partial(_lru_kernel, ts=ts),
        grid=(bsz, ns),
        in_specs=[col(C1_XB), col(C1_GB)] + [full(a) for a in args],
        out_specs=pl.BlockSpec((ts, D_BR), lambda b, s: (b * ns + s, 0)),
        out_shape=jax.ShapeDtypeStruct((bsz * seq, D_BR), BF16),
        scratch_shapes=[pltpu.VMEM((ts + 8, D_BR), F32), pltpu.VMEM((8, D_BR), F32),
                        pltpu.VMEM((ts, D_BR), F32), pltpu.VMEM((ts, D_BR), F32)],
        compiler_params=_params(dimension_semantics=("arbitrary", "arbitrary")),
        name="lru",
    )(h1, h1, *args)


def _mem_kernel(q_ref, g_ref, kv_ref, o_ref):
    hd = MEM_HEAD_DIM
    for hh in range(MEM_HEADS):
        cs = slice(hh * hd, (hh + 1) * hd)
        q = (q_ref[:, cs] * (hd ** -0.5)).astype(BF16)
        s = _dot_nt(q, kv_ref[:, cs])
        p = jnp.exp(s - jnp.max(s, axis=-1, keepdims=True))
        l = jnp.sum(p, axis=-1, keepdims=True)
        o = _dot(p.astype(BF16), kv_ref[:, D_BR + hh * hd:D_BR + (hh + 1) * hd]) / l
        o_ref[:, cs] = (o * _silu(g_ref[:, cs])).astype(o_ref.dtype)


def mem_branch(h2, kv, bsz, seq, tq=512):
    tq = min(tq, seq)
    nq = seq // tq
    mlen = kv.shape[0] // bsz
    col = lambda c: pl.BlockSpec((tq, D_BR), lambda b, i, c=c: (b * nq + i, c // D_BR))
    return pl.pallas_call(
        _mem_kernel,
        grid=(bsz, nq),
        in_specs=[col(C2_QM), col(C2_GM), pl.BlockSpec((mlen, 2 * D_BR), lambda b, i: (b, 0))],
        out_specs=pl.BlockSpec((tq, D_BR), lambda b, i: (b * nq + i, 0)),
        out_shape=jax.ShapeDtypeStruct((bsz * seq, D_BR), BF16),
        compiler_params=_params(),
        name="mem_attn",
    )(h2, h2, kv)


def _cmp_kernel(*refs):
    ngrp = 2 * KV_W // LANES
    t_refs = refs[:ngrp]
    pek_ref, pev_ref, w1k_ref, w1v_ref, w2kt_ref, w2v_ref, kct_ref, vc_ref = refs[ngrp:]
    n = t_refs[0].shape[0] // CMP_STRIDE
    for h in range(HKV):
        for is_v, (pe_ref, w1_ref) in enumerate(((pek_ref, w1k_ref), (pev_ref, w1v_ref))):
            grp, off = divmod(is_v * KV_W + h * DK, LANES)
            lo = hi = None
            for tok in range(CMP_STRIDE):
                x = t_refs[grp][pl.ds(tok, n, stride=CMP_STRIDE), :][:, off:off + DK]
                t2 = CMP_STRIDE + tok
                a = _dot((x + pe_ref[tok:tok + 1, :]).astype(BF16), w1_ref[tok * DK:(tok + 1) * DK, :])
                b = _dot((x + pe_ref[t2:t2 + 1, :]).astype(BF16), w1_ref[t2 * DK:(t2 + 1) * DK, :])
                lo = a if lo is None else lo + a
                hi = b if hi is None else hi + b
            hidden = _silu(lo + pltpu.roll(hi, n - 1, axis=0)).astype(BF16)
            if is_v:
                vc_ref[h] = _dot(hidden, w2v_ref[...]).astype(vc_ref.dtype)
            else:
                kct_ref[h] = _dot_nt(w2kt_ref[...], hidden).astype(kct_ref.dtype)


def nsa_compress(h1, bsz, seq, pe_k, pe_v, w1_k, w1_v, w2_k, w2_v):
    n = seq // CMP_STRIDE
    args = (pe_k, pe_v, w1_k.astype(BF16), w1_v.astype(BF16), w2_k.T.astype(BF16), w2_v.astype(BF16))
    full = lambda a: pl.BlockSpec(a.shape, lambda b: (0,) * a.ndim)
    return pl.pallas_call(
        _cmp_kernel,
        grid=(bsz,),
        in_specs=[pl.BlockSpec((seq, LANES), lambda b, g=g: (b, C1_KV // LANES + g))
                  for g in range(2 * KV_W // LANES)] + [full(a) for a in args],
        out_specs=[pl.BlockSpec((HKV, DK, n), lambda b: (b, 0, 0)), pl.BlockSpec((HKV, n, DK), lambda b: (b, 0, 0))],
        out_shape=[jax.ShapeDtypeStruct((bsz * HKV, DK, n), BF16), jax.ShapeDtypeStruct((bsz * HKV, n, DK), BF16)],
        compiler_params=_params(),
        name="nsa_compress",
    )(*([h1] * (2 * KV_W // LANES)), *args)


def _nsa_select_kernel(q_ref, gl_ref, gb_ref, kct_ref, vc_ref, bc_ref, ovt_ref, qsel_ref, ocg_ref, vt_s):
    qb = Q_BLOCK
    lane = lax.broadcasted_iota(jnp.int32, (qb, LANES), 1)
    jrow = lax.broadcasted_iota(jnp.int32, (MASK_ROWS, qb), 0).astype(F32)
    qlane = lax.broadcasted_iota(jnp.int32, (MASK_ROWS, qb), 1)
    sub = lax.broadcasted_iota(jnp.int32, (SUBLANES, qb), 0)
    nv = MASK_ROWS // SUBLANES
    ovt = ovt_ref[...]
    kct = kct_ref[0]
    vc = vc_ref[0]
    for sb in range(SEL_SUB):
        rows = slice(sb * qb, (sb + 1) * qb)
        blk = pl.program_id(2) * SEL_SUB + sb
        q = q_ref[rows, :] * (DK ** -0.5)
        gt = _sigmoid(gl_ref[rows, :] + gb_ref[...])
        qpad = []
        for g in range(GQA):
            t = q[:, (g // 2) * LANES:(g // 2 + 1) * LANES]
            if g % 2:
                t = pltpu.roll(t, DK, axis=1)
            qpad.append(jnp.where(lane < DK, t, 0.0))

        has_keys = blk * qb + lax.broadcasted_iota(jnp.int32, (qb, 1), 0) >= CMP_BLOCK - 1
        o_c = []
        psum = None
        for g in range(GQA):
            s = _dot(qpad[g][:, :DK].astype(BF16), kct) + bc_ref[0, sb, g * qb:(g + 1) * qb, :]
            m = jnp.max(s, axis=-1, keepdims=True)
            p = jnp.exp(s - m)
            p = p * jnp.where(has_keys, 1.0 / jnp.maximum(jnp.sum(p, axis=-1, keepdims=True), 1e-30), 0.0)
            o_c.append(gt[:, g:g + 1] * _dot(p.astype(BF16), vc))
            psum = p if psum is None else psum + p
        ocg_ref[rows, :] = jnp.concatenate(o_c, axis=1)

        p_hi = psum.astype(BF16)
        rem = psum - p_hi.astype(F32)
        p_mid = rem.astype(BF16)
        p_lo = (rem - p_mid.astype(F32)).astype(BF16)
        imp_t = _dot_nt(ovt, p_hi) + _dot_nt(ovt, p_mid) + _dot_nt(ovt, p_lo)

        qblk = ((blk * qb + qlane) >> int(math.log2(SLC_BLOCK))).astype(F32)
        val = jnp.where(jrow == 0.0, 3e38,
                        jnp.where(jrow == qblk, 3e38, jnp.where(jrow == qblk - 1.0, 3e38, imp_t)))
        val = jnp.where(jrow > qblk, -1.0, val)
        vt_s[sb] = val
        vals = [val[k * SUBLANES:(k + 1) * SUBLANES] for k in range(nv)]
        cnt = [jnp.zeros((SUBLANES, qb), F32) for _ in range(nv)]
        for jp in range(MASK_ROWS):
            rowv = jnp.broadcast_to(vt_s[sb, jp:jp + 1, :], (SUBLANES, qb))
            for k in range(nv):
                if k * SUBLANES > jp:
                    beat = jnp.where(rowv >= vals[k], 1.0, 0.0)
                elif (k + 1) * SUBLANES - 1 < jp:
                    beat = jnp.where(rowv > vals[k], 1.0, 0.0)
                else:
                    beat = jnp.where(sub > jp - k * SUBLANES, jnp.where(rowv >= vals[k], 1.0, 0.0),
                                     jnp.where(rowv > vals[k], 1.0, 0.0))
                cnt[k] = cnt[k] + beat
        selneg_t = jnp.concatenate([jnp.where(c < float(SLC_TOPK), 0.0, NEG) for c in cnt], axis=0)
        for g in range(GQA):
            qt = jnp.concatenate([qpad[g].T[:DK], selneg_t], axis=0)
            qsel_ref[0, sb, :, g * qb:(g + 1) * qb] = qt.astype(qsel_ref.dtype)


def _nsa_attn_kernel(qsel_ref, ocg_ref, gc_ref, gl_ref, gb_ref, ks_ref, vs_ref, gs_ref, kw_ref, vw_ref, bw_ref,
                     o_ref, s_s, p_s):
    qb = Q_BLOCK
    hq = GQA * qb
    nq = ATT_NQ * hq
    i = pl.program_id(2) * ATT_NQ
    step = SEL_TK // qb
    n_tiles = (i * qb) // SEL_TK + 1
    qt = jnp.concatenate([qsel_ref[0, b] for b in range(ATT_NQ)], axis=1)

    def stage_qk(j):
        k0 = pl.multiple_of(j * SEL_TK, SEL_TK)
        sc = _dot(ks_ref[pl.ds(k0, SEL_TK), :], qt)
        for b in range(ATT_NQ):
            dd = jnp.minimum(i + b - j * step, SEL_FAR)
            boff = pl.multiple_of((SEL_FAR - dd) * qb, qb)
            s_s[:, b * hq:(b + 1) * hq] = sc[:, b * hq:(b + 1) * hq] + gs_ref[0, pl.ds(boff, SEL_TK), :]

    def stage_softmax(m):
        m_new, alpha = [], []
        for g in range(ATT_NQ * GQA):
            cs = slice(g * qb, (g + 1) * qb)
            sc = s_s[:, cs]
            mg = jnp.maximum(m[:, cs], jnp.max(sc, axis=0, keepdims=True))
            alpha.append(jnp.exp(m[:, cs] - mg))
            p_s[:, cs] = jnp.exp(sc - mg).astype(BF16)
            m_new.append(mg)
        return jnp.concatenate(m_new, axis=1), jnp.concatenate(alpha, axis=1)

    def stage_pv(j, acc):
        k0 = pl.multiple_of(jnp.maximum(j, 0) * SEL_TK, SEL_TK)
        return acc + _dot(vs_ref[0, :V_ROWS, pl.ds(k0, SEL_TK)], p_s[...])

    def advance(j, m, acc):
        acc = stage_pv(j - 2, acc)
        m, alpha = stage_softmax(m)
        return m, alpha * acc

    def body(j, carry):
        m, acc = advance(j, *carry)
        stage_qk(j)
        return m, acc

    p_s[...] = jnp.zeros(p_s.shape, BF16)
    stage_qk(jnp.int32(0))
    init = (jnp.full((1, nq), NEG, F32), jnp.zeros((V_ROWS, nq), F32))
    m, acc = lax.fori_loop(1, n_tiles, body, init)
    _, acc = advance(n_tiles, m, acc)
    acc_s = stage_pv(n_tiles - 1, acc)

    wk = WINDOW + ATT_NQ * qb
    w0 = pl.multiple_of(jnp.maximum(i * qb - WINDOW, 0), qb)
    rowi = lax.broadcasted_iota(jnp.int32, (LANES, nq), 0)
    q_win = jnp.where(rowi < DK, qt, jnp.zeros_like(qt))
    sw = _dot(kw_ref[pl.ds(w0, wk), :], q_win) + bw_ref[0, 0]
    pw = jnp.exp(sw - jnp.max(sw, axis=0, keepdims=True)).astype(BF16)
    acc_w = _dot(vw_ref[0, :V_ROWS, pl.ds(w0, wk)], pw)

    pad_rows = jnp.zeros((LANES - V_ROWS, qb), F32)
    for b in range(ATT_NQ):
        rows = slice(b * qb, (b + 1) * qb)
        gt = _sigmoid(gl_ref[rows, :] + gb_ref[...])
        outs = []
        for g in range(GQA):
            cs = slice(b * hq + g * qb, b * hq + (g + 1) * qb)
            a_w = jnp.concatenate([acc_w[:, cs], pad_rows], axis=0).T
            a_s = jnp.concatenate([acc_s[:, cs], pad_rows], axis=0).T
            o_w = a_w[:, :DK] / a_w[:, DK:DK + 1]
            o_s = a_s[:, :DK] / a_s[:, DK:DK + 1]
            outs.append(gt[:, GQA + g:GQA + g + 1] * o_s + gt[:, 2 * GQA + g:2 * GQA + g + 1] * o_w)
        o = ocg_ref[rows, :] + jnp.concatenate(outs, axis=1)
        o_ref[rows, :] = (o * _silu(gc_ref[rows, :])).astype(o_ref.dtype)


def _rel_bucket(dist):
    n = jnp.maximum(dist, 0)
    exact = REL_BUCKETS // 2
    nf = jnp.maximum(n, 1).astype(jnp.float32)
    large = exact + (jnp.log(nf / exact) / math.log(REL_MAX_DIST / exact)
                     * (REL_BUCKETS - exact)).astype(jnp.int32)
    return jnp.where(n < exact, n, jnp.minimum(large, REL_BUCKETS - 1))


def nsa_bias_tables(rel_bias, seq):
    qb = Q_BLOCK
    n_pad = seq // CMP_STRIDE
    c0 = n_pad - NA
    wc = 2 * n_pad
    ws = SEL_FAR * qb + SEL_TK
    nmax = max(seq, SEL_FAR * qb + qb)
    padl = CMP_STRIDE * (wc + 2)
    bvec = rel_bias[_rel_bucket(jnp.arange(nmax))].T.astype(F32)
    vext = jnp.concatenate([jnp.full((NSA_HEADS, padl), NEG, F32), bvec], axis=1)
    pos = jnp.arange(padl + nmax) - padl
    vwin = jnp.where(pos < WINDOW, vext, NEG)

    mlo, mhi = c0 - wc + 1, NA - 1 + c0
    start = padl + CMP_STRIDE * mlo - (CMP_BLOCK - 1)
    assert start >= 0
    u = vext[:, start:start + CMP_STRIDE * (mhi - mlo + 1)].reshape(NSA_HEADS, mhi - mlo + 1, CMP_STRIDE)
    urev = u[:, ::-1, :]
    gcb = jnp.stack([urev[:, NA - 1 - a:NA - 1 - a + wc, :] for a in range(NA)], axis=1)
    gcb = gcb.transpose(0, 1, 3, 2).reshape(HKV, GQA * qb, wc)
    nqb = seq // qb
    bc = jnp.stack([gcb[:, :, c0 - NA * i:c0 - NA * i + n_pad] for i in range(nqb)], axis=1)

    def toeplitz(vec, d0, width):
        vrev = vec[:, ::-1]
        s0 = nmax - d0 - qb
        period = width + qb - 1
        assert s0 >= 0 and s0 + period <= padl + nmax
        sl = vrev[:, s0:s0 + period]
        y = jnp.concatenate([sl[:, qb - 1:], sl[:, :qb - 1]], axis=1)
        z = jnp.tile(y, (1, qb))[:, :qb * (period - 1)].reshape(NSA_HEADS, qb, period - 1)
        return z[:, :, :width].reshape(HKV, GQA * qb, width)

    gs = toeplitz(vext, SEL_FAR * qb, ws)
    bw = tuple(tuple(toeplitz(vwin, first + b * qb, WINDOW + ATT_NQ * qb) for b in range(ATT_NQ))
               for first in (0, WINDOW))
    return bc, gs, bw


def _overlap_matrix_t(n_pad):
    c_start = np.arange(n_pad)[None, :] * CMP_STRIDE
    s_start = np.arange(MASK_ROWS)[:, None] * SLC_BLOCK
    ov = np.clip(np.minimum(c_start + CMP_BLOCK, s_start + SLC_BLOCK) - np.maximum(c_start, s_start), 0, None)
    return ov.astype(np.float32) / CMP_BLOCK


def nsa_branch(h1, h2, kv, hgl, gate_b4, kct, vc, tables, bsz, seq):
    qb = Q_BLOCK
    nqb = seq // qb
    n_pad = seq // CMP_STRIDE
    n_slc = seq // SLC_BLOCK
    assert n_slc <= MASK_ROWS and seq % SEL_TK == 0 and nqb % SEL_SUB == 0
    assert (SEL_TK // qb) % ATT_NQ == 0 and nqb % ATT_NQ == 0
    bc, gs, bw = tables
    m = bsz * seq
    gw = GQA * DK

    sq = SEL_SUB * qb
    nsq = seq // sq
    ovt = jnp.asarray(_overlap_matrix_t(n_pad), BF16)
    qsel, ocg = pl.pallas_call(
        _nsa_select_kernel,
        grid=(HKV, bsz, nsq),
        in_specs=[
            pl.BlockSpec((sq, gw), lambda hh, b, i: (b * nsq + i, C1_QC // gw + hh)),
            pl.BlockSpec((sq, LANES), lambda hh, b, i: (b * nsq + i, hh)),
            pl.BlockSpec((1, LANES), lambda hh, b, i: (0, hh)),
            pl.BlockSpec((1, DK, n_pad), lambda hh, b, i: (b * HKV + hh, 0, 0)),
            pl.BlockSpec((1, n_pad, DK), lambda hh, b, i: (b * HKV + hh, 0, 0)),
            pl.BlockSpec((1, SEL_SUB) + bc.shape[2:], lambda hh, b, i: (hh, i, 0, 0)),
            pl.BlockSpec(ovt.shape, lambda hh, b, i: (0, 0)),
        ],
        out_specs=[pl.BlockSpec((1, SEL_SUB, LANES, GQA * qb), lambda hh, b, i: (hh, b * nsq + i, 0, 0)),
                   pl.BlockSpec((sq, gw), lambda hh, b, i: (b * nsq + i, hh))],
        out_shape=[jax.ShapeDtypeStruct((HKV, bsz * nqb, LANES, GQA * qb), BF16),
                   jax.ShapeDtypeStruct((m, D_BR), F32)],
        scratch_shapes=[pltpu.VMEM((SEL_SUB, MASK_ROWS, qb), F32)],
        compiler_params=_params(),
        name="nsa_select",
    )(h1, hgl, gate_b4, kct, vc, bc, ovt)

    gs_t = gs.transpose(0, 2, 1)
    bw_t = jnp.stack([jnp.concatenate([w.transpose(0, 2, 1) for w in variant], axis=2) for variant in bw],
                     axis=1)
    ks_ext, kw_ext, vs_ext, vw_ext = kv

    aq = ATT_NQ * qb
    npair = nqb // ATT_NQ
    rowblk = lambda width, c0: pl.BlockSpec((aq, width), lambda hh, b, i: (b * npair + i, c0 // width + hh))
    keys = pl.BlockSpec((seq, LANES), lambda hh, b, i: (b, hh))
    vals = pl.BlockSpec((1, LANES, seq), lambda hh, b, i: (hh, 0, b))
    return pl.pallas_call(
        _nsa_attn_kernel,
        grid=(HKV, bsz, npair),
        in_specs=[
            pl.BlockSpec((1, ATT_NQ, LANES, GQA * qb), lambda hh, b, i: (hh, b * npair + i, 0, 0)),
            rowblk(gw, 0), rowblk(gw, C2_GC),
            pl.BlockSpec((aq, LANES), lambda hh, b, i: (b * npair + i, hh)),
            pl.BlockSpec((1, LANES), lambda hh, b, i: (0, hh)),
            keys, vals, pl.BlockSpec((1,) + gs_t.shape[1:], lambda hh, b, i: (hh, 0, 0)),
            keys, vals, pl.BlockSpec((1, 1) + bw_t.shape[2:], lambda hh, b, i: (hh, jnp.minimum(i, 1), 0, 0)),
        ],
        out_specs=pl.BlockSpec((aq, gw), lambda hh, b, i: (b * npair + i, hh)),
        out_shape=jax.ShapeDtypeStruct((m, D_BR), BF16),
        scratch_shapes=[pltpu.VMEM((SEL_TK, ATT_NQ * GQA * qb), F32), pltpu.VMEM((SEL_TK, ATT_NQ * GQA * qb), BF16)],
        compiler_params=_params(),
        name="nsa_attn",
    )(qsel, ocg, h2, hgl, gate_b4, ks_ext, vs_ext, gs_t, kw_ext, vw_ext, bw_t)


def _final_kernel(oa_ref, ob_ref, oc_ref, om_ref, *rest):
    gate_refs = rest[:2 * N_BRANCHES]
    x_ref, wb_ref, wo_ref, lng_ref, lnb_ref, y_ref, yb_ref = rest[2 * N_BRANCHES:]
    o_refs = (oa_ref, ob_ref, oc_ref, om_ref)
    halves = []
    for c in range(2):
        cs = slice(c * D_BR, (c + 1) * D_BR)
        acc = None
        for k in range(N_BRANCHES):
            term = _sigmoid(gate_refs[2 * k + c][...]) * _dot(o_refs[k][...], wb_ref[k, :, cs])
            acc = term if acc is None else acc + term
        halves.append(acc.astype(BF16))
    merged = jnp.concatenate(halves, axis=1)
    z = DN_ALPHA * x_ref[...] + _dot(merged, wo_ref[...])
    mu = jnp.mean(z, axis=-1, keepdims=True)
    zc = z - mu
    var = jnp.mean(zc * zc, axis=-1, keepdims=True)
    y = zc * lax.rsqrt(var + LN_EPS) * lng_ref[...] + lnb_ref[...]
    y_ref[...] = y
    yb_ref[...] = y.astype(BF16)


def final_merge(o_a, o_b, o_c, o_m, h2, x, w_branch, w_out, ln_g, ln_b, tm=128):
    m = x.shape[0]
    tm = min(tm, m)
    br = pl.BlockSpec((tm, D_BR), lambda i: (i, 0))
    gate = lambda k: pl.BlockSpec((tm, D_BR), lambda i, k=k: (i, C2_MERGE // D_BR + k))
    xs = pl.BlockSpec((tm, D_MODEL), lambda i: (i, 0))
    resident = lambda a: pl.BlockSpec(a.shape, lambda i: (0,) * a.ndim, pipeline_mode=pl.Buffered(1))
    wb = w_branch.astype(BF16)
    wo = w_out.astype(BF16)
    lg, lb = ln_g.reshape(1, D_MODEL), ln_b.reshape(1, D_MODEL)
    ngate = 2 * N_BRANCHES
    return pl.pallas_call(
        _final_kernel,
        grid=(m // tm,),
        in_specs=[br, br, br, br] + [gate(k) for k in range(ngate)] + [xs]
                 + [resident(wb), resident(wo), resident(lg), resident(lb)],
        out_specs=[xs, xs],
        out_shape=[jax.ShapeDtypeStruct((m, D_MODEL), F32), jax.ShapeDtypeStruct((m, D_MODEL), BF16)],
        compiler_params=_params(),
        name="merge_out_ln",
    )(o_a, o_b, o_c, o_m, *([h2] * ngate), x, wb, wo, lg, lb)


def _gate_spread_matrix():
    p = np.zeros((3 * NSA_HEADS, HKV * LANES), np.float32)
    for hh in range(HKV):
        for brn in range(3):
            for g in range(GQA):
                p[brn * NSA_HEADS + hh * GQA + g, hh * LANES + brn * GQA + g] = 1.0
    return p


def _spread_gate_cols(a):
    return jnp.dot(a, jnp.asarray(_gate_spread_matrix()), precision=lax.Precision.HIGHEST)


def layer(l, x, xb, mem_b, tables, bsz, seq, w_in, sgu_ln_g, sgu_ln_b, sgu_w, sgu_b, conv_w, conv_b,
          lru_wa, lru_ba, lru_wx, lru_bx, lru_lambda, cmp_pe_k, cmp_pe_v, cmp_w1_k, cmp_w1_v,
          cmp_w2_k, cmp_w2_v, nsa_gate_b, w_mem_kv, w_branch, w_out, ln_g, ln_b):
    m = bsz * seq
    tm = min(1024, m)
    tm_big = min(2048, m)
    h1 = matmul_f32w(xb, w_in, l, 0, N1, F32, tm_big, 512, w_is_nk=True)
    kv = kv_projection(xb, w_in, l, N1, seq, tm)
    h2 = matmul_f32w(xb, w_in, l, GL_OFF + 3 * NSA_HEADS, N2, F32, tm, 1024, w_is_nk=True,
                     w_single_buffer=True)
    w_gl = _spread_gate_cols(w_in[l, GL_OFF:GL_OFF + 3 * NSA_HEADS, :].T).astype(BF16)
    hgl = matmul(xb, w_gl, F32, tm, HKV * LANES)
    gate_b4 = _spread_gate_cols(nsa_gate_b[l].reshape(1, 3 * NSA_HEADS))

    o_a = gmlp_branch(h1, sgu_ln_g[l], sgu_ln_b[l], sgu_w[l], sgu_b[l])
    o_b = lru_branch(h1, bsz, seq, conv_w[l], conv_b[l], lru_wa[l], lru_ba[l], lru_wx[l], lru_bx[l],
                     lru_lambda[l])

    kct, vc = nsa_compress(h1, bsz, seq, cmp_pe_k[l], cmp_pe_v[l],
                           cmp_w1_k[l], cmp_w1_v[l], cmp_w2_k[l], cmp_w2_v[l])
    o_c = nsa_branch(h1, h2, kv, hgl, gate_b4, kct, vc, tables, bsz, seq)

    mrows = mem_b.shape[0]
    kv = matmul_f32w(mem_b, w_mem_kv, l, 0, 2 * D_BR, BF16, min(512, mrows), 512)
    o_m = mem_branch(h2, kv, bsz, seq)
    return final_merge(o_a, o_b, o_c, o_m, h2, x, w_branch[l], w_out[l], ln_g[l], ln_b[l])


def kernel(x, mem, rel_bias, w_in, sgu_ln_g, sgu_ln_b, sgu_w, sgu_b, conv_w, conv_b, lru_wa, lru_ba, lru_wx,
           lru_bx, lru_lambda, cmp_pe_k, cmp_pe_v, cmp_w1_k, cmp_w1_v, cmp_w2_k, cmp_w2_v, nsa_gate_b,
           w_mem_kv, w_branch, w_out, ln_g, ln_b):
    bsz, seq, _ = x.shape
    tables = nsa_bias_tables(rel_bias, seq)
    xf = x.reshape(bsz * seq, D_MODEL)
    xb = xf.astype(BF16)
    mem_b = mem.reshape(-1, D_MODEL).astype(BF16)
    w_in_t = jnp.swapaxes(w_in, 1, 2)
    params = (w_in_t, sgu_ln_g, sgu_ln_b, sgu_w, sgu_b, conv_w, conv_b, lru_wa, lru_ba, lru_wx, lru_bx,
              lru_lambda, cmp_pe_k, cmp_pe_v, cmp_w1_k, cmp_w1_v, cmp_w2_k, cmp_w2_v, nsa_gate_b,
              w_mem_kv, w_branch, w_out, ln_g, ln_b)
    for l in range(w_in.shape[0]):
        xf, xb = layer(l, xf, xb, mem_b, tables, bsz, seq, *params)
    return xf.reshape(bsz, seq, D_MODEL)
```

```python
import functools
import math

import numpy as np
import jax
import jax.numpy as jnp
from jax import lax
from jax.experimental import pallas as pl
from jax.experimental.pallas import tpu as pltpu

F32 = jnp.float32
BF16 = jnp.bfloat16

D_MODEL = 2048
DEPTH = 2
D_BR = D_MODEL // 2
N_BRANCHES = 4
GMLP_CHUNK = 128
GMLP_GROUPS = 8
LRU_BLOCKS = 8
LRU_BLOCK_DIM = D_BR // LRU_BLOCKS
CONV_WIDTH = 4
LRU_C = 8.0
DK = 64
NSA_HEADS = D_BR // DK
HKV = NSA_HEADS // 4
GQA = NSA_HEADS // HKV
KV_W = HKV * DK
CMP_BLOCK = 32
CMP_STRIDE = 16
CMP_HIDDEN = 256
SLC_BLOCK = 64
SLC_TOPK = 8
WINDOW = 256
Q_BLOCK = 128
MEM_HEADS = 4
MEM_HEAD_DIM = D_BR // MEM_HEADS
REL_BUCKETS = 32
REL_MAX_DIST = 1024
DN_ALPHA = (2 * DEPTH) ** 0.25
LN_EPS = 1e-5

LANES = 128
SUBLANES = 8
NEG = -1e30
SEL_TK = 512
SEL_FAR = 11
MASK_ROWS = LANES - DK
NA = Q_BLOCK // CMP_STRIDE
SEL_SUB = 4
ATT_NQ = 2
V_ROWS = DK + 16
VMEM_LIMIT = 56 * 1024 * 1024

C1_U, C1_V, C1_GA, C1_XB, C1_GB, C1_QC = (D_BR * k for k in range(6))
C1_KV = 6 * D_BR
N1 = C1_KV + 2 * KV_W
NKV = 4 * KV_W
C2_GC, C2_QM, C2_GM, C2_MERGE = (D_BR * k for k in range(4))
N2 = C2_MERGE + N_BRANCHES * D_MODEL
GL_OFF = N1 + NKV


def _sigmoid(x):
    return 1.0 / (1.0 + jnp.exp(-x))


def _silu(x):
    return x * _sigmoid(x)


def _gelu_tanh(x):
    return 0.5 * x * (1.0 + jnp.tanh(math.sqrt(2.0 / math.pi) * (x + 0.044715 * (x * x * x))))


def _dot(a, b):
    return jnp.dot(a, b, preferred_element_type=F32)


def _dot_nt(a, b):
    return lax.dot_general(a, b, (((1,), (1,)), ((), ())), preferred_element_type=F32)


def _params(**kw):
    return pltpu.CompilerParams(vmem_limit_bytes=VMEM_LIMIT, **kw)


def _mm_kernel(x_ref, w_ref, o_ref):
    o_ref[...] = _dot(x_ref[...], w_ref[...]).astype(o_ref.dtype)


def matmul(x, w, out_dtype, tm, tn):
    m, k = x.shape
    n = w.shape[1]
    assert m % tm == 0 and n % tn == 0, (m, n, tm, tn)
    return pl.pallas_call(
        _mm_kernel,
        grid=(n // tn, m // tm),
        in_specs=[pl.BlockSpec((tm, k), lambda j, i: (i, 0)),
                  pl.BlockSpec((k, tn), lambda j, i: (0, j))],
        out_specs=pl.BlockSpec((tm, tn), lambda j, i: (i, j)),
        out_shape=jax.ShapeDtypeStruct((m, n), out_dtype),
        compiler_params=_params(),
        name="matmul",
    )(x, w)


def _mm_castw_kernel(x_ref, *rest, w_is_nk, shift):
    w_refs, (o_ref, wb_ref) = rest[:-2], rest[-2:]

    @pl.when(pl.program_id(1) == 0)
    def _():
        if shift:
            w = jnp.concatenate([w_refs[0][0, shift:, :], w_refs[1][0, :shift, :]], axis=0)
        else:
            w = w_refs[0][0]
        wb_ref[...] = (w.T if w_is_nk else w).astype(BF16)

    o_ref[...] = _dot(x_ref[...], wb_ref[...]).astype(o_ref.dtype)


def matmul_f32w(x, w3, layer_idx, col0, n, out_dtype, tm, tn, w_is_nk=False, w_single_buffer=False):
    m, k = x.shape
    ncols = w3.shape[1] if w_is_nk else w3.shape[2]
    shift = col0 % tn
    assert m % tm == 0 and n % tn == 0 and col0 + n <= ncols, (m, n, col0, tm, tn)
    assert shift == 0 or (w_is_nk and shift % SUBLANES == 0), (col0, tn)
    j0 = col0 // tn
    mode = dict(pipeline_mode=pl.Buffered(1)) if w_single_buffer else {}
    if w_is_nk:
        w_specs = [pl.BlockSpec((1, tn, k), lambda j, i, d=d: (layer_idx, j0 + j + d, 0), **mode)
                   for d in range(2 if shift else 1)]
    else:
        w_specs = [pl.BlockSpec((1, k, tn), lambda j, i: (layer_idx, 0, j0 + j), **mode)]
    return pl.pallas_call(
        functools.partial(_mm_castw_kernel, w_is_nk=w_is_nk, shift=shift),
        grid=(n // tn, m // tm),
        in_specs=[pl.BlockSpec((tm, k), lambda j, i: (i, 0))] + w_specs,
        out_specs=pl.BlockSpec((tm, tn), lambda j, i: (i, j)),
        out_shape=jax.ShapeDtypeStruct((m, n), out_dtype),
        scratch_shapes=[pltpu.VMEM((k, tn), BF16)],
        compiler_params=_params(dimension_semantics=("arbitrary", "arbitrary")),
        name="matmul_f32w",
    )(x, *([w3] * len(w_specs)))


def _kv_proj_kernel(x_ref, ws_ref, ww_ref, ks_ref, kw_ref, vs_ref, vw_ref, wk_s, wv_s, *, tm, seq):
    @pl.when(pl.program_id(0) == 0)
    def _():
        wk = jnp.concatenate([ws_ref[0, 0:KV_W], ww_ref[0, 0:KV_W]], axis=0)
        wk_s[...] = wk.T.astype(BF16)
        wv_s[...] = jnp.concatenate([ws_ref[0, KV_W:2 * KV_W], ww_ref[0, KV_W:2 * KV_W]], axis=0).astype(BF16)

    x = x_ref[...]
    kk = _dot(x, wk_s[...])
    vt = _dot_nt(wv_s[...], x)
    pos = (pl.program_id(0) * tm + lax.broadcasted_iota(jnp.int32, (tm, MASK_ROWS), 0)) % seq
    blk = lax.broadcasted_iota(jnp.int32, (tm, MASK_ROWS), 1)
    onehot = jnp.where((pos >> int(math.log2(SLC_BLOCK))) == blk, 1.0, 0.0)
    zeros = jnp.zeros((tm, MASK_ROWS), F32)
    tail = jnp.where(lax.broadcasted_iota(jnp.int32, (LANES - DK, tm), 0) == 0, 1.0, 0.0)
    for h in range(HKV):
        ks_ref[:, h * LANES:(h + 1) * LANES] = jnp.concatenate(
            [kk[:, h * DK:(h + 1) * DK], onehot], axis=1).astype(ks_ref.dtype)
        kw_ref[:, h * LANES:(h + 1) * LANES] = jnp.concatenate(
            [kk[:, KV_W + h * DK:KV_W + (h + 1) * DK], zeros], axis=1).astype(kw_ref.dtype)
        vs_ref[h] = jnp.concatenate([vt[h * DK:(h + 1) * DK], tail], axis=0).astype(vs_ref.dtype)
        vw_ref[h] = jnp.concatenate([vt[KV_W + h * DK:KV_W + (h + 1) * DK], tail], axis=0).astype(vw_ref.dtype)


def kv_projection(x, wt3, layer_idx, row0, seq, tm):
    m, k = x.shape
    nrows = 2 * KV_W
    assert m % tm == 0 and row0 % nrows == 0 and seq % tm == 0
    wspec = lambda d: pl.BlockSpec((1, nrows, k), lambda i: (layer_idx, row0 // nrows + d, 0),
                                   pipeline_mode=pl.Buffered(1))
    keys = jax.ShapeDtypeStruct((m, HKV * LANES), BF16)
    vals = jax.ShapeDtypeStruct((HKV, LANES, m), BF16)
    return pl.pallas_call(
        functools.partial(_kv_proj_kernel, tm=tm, seq=seq),
        grid=(m // tm,),
        in_specs=[pl.BlockSpec((tm, k), lambda i: (i, 0)), wspec(0), wspec(1)],
        out_specs=[pl.BlockSpec((tm, HKV * LANES), lambda i: (i, 0))] * 2
                  + [pl.BlockSpec((HKV, LANES, tm), lambda i: (0, 0, i))] * 2,
        out_shape=[keys, keys, vals, vals],
        scratch_shapes=[pltpu.VMEM((k, 2 * KV_W), BF16), pltpu.VMEM((2 * KV_W, k), BF16)],
        compiler_params=_params(dimension_semantics=("arbitrary",)),
        name="kv_projection",
    )(x, wt3, wt3)


def _gmlp_kernel(u_ref, v_ref, ga_ref, lng_ref, lnb_ref, w_ref, bs_ref, o_ref, *, rows):
    gd = D_BR // GMLP_GROUPS
    for c in range(rows // GMLP_CHUNK):
        r = slice(c * GMLP_CHUNK, (c + 1) * GMLP_CHUNK)
        v = _gelu_tanh(v_ref[r, :])
        mu = jnp.mean(v, axis=-1, keepdims=True)
        vc = v - mu
        var = jnp.mean(vc * vc, axis=-1, keepdims=True)
        vb = (vc * lax.rsqrt(var + LN_EPS) * lng_ref[...] + lnb_ref[...]).astype(BF16)
        u = _gelu_tanh(u_ref[r, :]) * _silu(ga_ref[r, :])
        for g in range(GMLP_GROUPS):
            cs = slice(g * gd, (g + 1) * gd)
            mixed = _dot(w_ref[g], vb[:, cs]) + bs_ref[:, g:g + 1]
            o_ref[r, cs] = (u[:, cs] * mixed).astype(o_ref.dtype)


def gmlp_branch(h1, ln_g, ln_b, w_s, b_s, rows=512):
    m = h1.shape[0]
    rows = min(rows, m)
    causal = jnp.tril(jnp.ones((GMLP_CHUNK, GMLP_CHUNK), dtype=bool))
    w = jnp.where(causal, w_s, 0).astype(BF16)
    col = lambda c: pl.BlockSpec((rows, D_BR), lambda i, c=c: (i, c // D_BR))
    full = lambda a: pl.BlockSpec(a.shape, lambda i: (0,) * a.ndim)
    args = (ln_g.reshape(1, D_BR), ln_b.reshape(1, D_BR), w, b_s.T)
    return pl.pallas_call(
        functools.partial(_gmlp_kernel, rows=rows),
        grid=(m // rows,),
        in_specs=[col(C1_U), col(C1_V), col(C1_GA)] + [full(a) for a in args],
        out_specs=pl.BlockSpec((rows, D_BR), lambda i: (i, 0)),
        out_shape=jax.ShapeDtypeStruct((m, D_BR), BF16),
        compiler_params=_params(),
        name="gmlp",
    )(h1, h1, h1, *args)


def _lru_kernel(xb_ref, gb_ref, cw_ref, cb_ref, wa_ref, ba_ref, wx_ref, bx_ref, lam_ref, o_ref,
                xbuf, hcarry, a_s, g_s, *, ts):
    @pl.when(pl.program_id(1) == 0)
    def _():
        xbuf[0:8, :] = jnp.zeros((8, D_BR), F32)
        hcarry[...] = jnp.zeros((8, D_BR), F32)

    xbuf[8:8 + ts, :] = xb_ref[...]
    xc = cb_ref[...] + cw_ref[0:1, :] * xbuf[pl.ds(8 - (CONV_WIDTH - 1), ts), :]
    for k in range(1, CONV_WIDTH):
        xc = xc + cw_ref[k:k + 1, :] * xbuf[pl.ds(8 - (CONV_WIDTH - 1) + k, ts), :]
    xbuf[0:8, :] = xbuf[ts:ts + 8, :]

    xcb = xc.astype(BF16)
    bd = LRU_BLOCK_DIM
    r = jnp.concatenate([_dot(xcb[:, n * bd:(n + 1) * bd], wa_ref[n]) for n in range(LRU_BLOCKS)], axis=1)
    i = jnp.concatenate([_dot(xcb[:, n * bd:(n + 1) * bd], wx_ref[n]) for n in range(LRU_BLOCKS)], axis=1)
    r = _sigmoid(r + ba_ref[...])
    i = _sigmoid(i + bx_ref[...])
    nl = -lam_ref[...]
    softplus = jnp.maximum(nl, 0.0) + jnp.log1p(jnp.exp(-jnp.abs(nl)))
    log_a = (-LRU_C * softplus) * r
    a = jnp.exp(log_a)
    a_s[...] = a
    g_s[...] = jnp.sqrt(1.0 - a * a) * i * xc

    row = lax.broadcasted_iota(jnp.int32, (8, D_BR), 0)

    def body(j, carry):
        r0 = pl.multiple_of(j * 8, 8)
        av = a_s[pl.ds(r0, 8), :]
        bv = g_s[pl.ds(r0, 8), :]
        for d in (1, 2, 4):
            keep = row >= d
            a_sh = pltpu.roll(av, d, axis=0)
            b_sh = pltpu.roll(bv, d, axis=0)
            bv = jnp.where(keep, av * b_sh + bv, bv)
            av = jnp.where(keep, av * a_sh, av)
        hv = av * carry + bv
        g_s[pl.ds(r0, 8), :] = hv
        return jnp.broadcast_to(hv[7:8, :], (8, D_BR))

    hcarry[...] = lax.fori_loop(0, ts // 8, body, hcarry[...])
    o_ref[...] = (g_s[...] * _silu(gb_ref[...])).astype(o_ref.dtype)


def lru_branch(h1, bsz, seq, conv_w, conv_b, wa, ba, wx, bx, lam, ts=512):
    ts = min(ts, seq)
    ns = seq // ts
    col = lambda c: pl.BlockSpec((ts, D_BR), lambda b, s, c=c: (b * ns + s, c // D_BR))
    full = lambda a: pl.BlockSpec(a.shape, lambda b, s: (0,) * a.ndim)
    row = lambda a: a.reshape(1, D_BR)
    args = (conv_w, row(conv_b), wa.astype(BF16), row(ba), wx.astype(BF16), row(bx), row(lam))
    return pl.pallas_call(
        functools.partial(_lru_kernel, ts=ts),
        grid=(bsz, ns),
        in_specs=[col(C1_XB), col(C1_GB)] + [full(a) for a in args],
        out_specs=pl.BlockSpec((ts, D_BR), lambda b, s: (b * ns + s, 0)),
        out_shape=jax.ShapeDtypeStruct((bsz * seq, D_BR), BF16),
        scratch_shapes=[pltpu.VMEM((ts + 8, D_BR), F32), pltpu.VMEM((8, D_BR), F32),
                        pltpu.VMEM((ts, D_BR), F32), pltpu.VMEM((ts, D_BR), F32)],
        compiler_params=_params(dimension_semantics=("arbitrary", "arbitrary")),
        name="lru",
    )(h1, h1, *args)


def _mem_kernel(q_ref, g_ref, kv_ref, o_ref):
    hd = MEM_HEAD_DIM
    for hh in range(MEM_HEADS):
        cs = slice(hh * hd, (hh + 1) * hd)
        q = (q_ref[:, cs] * (hd ** -0.5)).astype(BF16)
        s = _dot_nt(q, kv_ref[:, cs])
        p = jnp.exp(s - jnp.max(s, axis=-1, keepdims=True))
        l = jnp.sum(p, axis=-1, keepdims=True)
        o = _dot(p.astype(BF16), kv_ref[:, D_BR + hh * hd:D_BR + (hh + 1) * hd]) / l
        o_ref[:, cs] = (o * _silu(g_ref[:, cs])).astype(o_ref.dtype)


def mem_branch(h2, kv, bsz, seq, tq=512):
    tq = min(tq, seq)
    nq = seq // tq
    mlen = kv.shape[0] // bsz
    col = lambda c: pl.BlockSpec((tq, D_BR), lambda b, i, c=c: (b * nq + i, c // D_BR))
    return pl.pallas_call(
        _mem_kernel,
        grid=(bsz, nq),
        in_specs=[col(C2_QM), col(C2_GM), pl.BlockSpec((mlen, 2 * D_BR), lambda b, i: (b, 0))],
        out_specs=pl.BlockSpec((tq, D_BR), lambda b, i: (b * nq + i, 0)),
        out_shape=jax.ShapeDtypeStruct((bsz * seq, D_BR), BF16),
        compiler_params=_params(),
        name="mem_attn",
    )(h2, h2, kv)


def _cmp_kernel(*refs):
    ngrp = 2 * KV_W // LANES
    t_refs = refs[:ngrp]
    pek_ref, pev_ref, w1k_ref, w1v_ref, w2kt_ref, w2v_ref, kct_ref, vc_ref = refs[ngrp:]
    n = t_refs[0].shape[0] // CMP_STRIDE
    for h in range(HKV):
        for is_v, (pe_ref, w1_ref) in enumerate(((pek_ref, w1k_ref), (pev_ref, w1v_ref))):
            grp, off = divmod(is_v * KV_W + h * DK, LANES)
            lo = hi = None
            for tok in range(CMP_STRIDE):
                x = t_refs[grp][pl.ds(tok, n, stride=CMP_STRIDE), :][:, off:off + DK]
                t2 = CMP_STRIDE + tok
                a = _dot((x + pe_ref[tok:tok + 1, :]).astype(BF16), w1_ref[tok * DK:(tok + 1) * DK, :])
                b = _dot((x + pe_ref[t2:t2 + 1, :]).astype(BF16), w1_ref[t2 * DK:(t2 + 1) * DK, :])
                lo = a if lo is None else lo + a
                hi = b if hi is None else hi + b
            hidden = _silu(lo + pltpu.roll(hi, n - 1, axis=0)).astype(BF16)
            if is_v:
                vc_ref[h] = _dot(hidden, w2v_ref[...]).astype(vc_ref.dtype)
            else:
                kct_ref[h] = _dot_nt(w2kt_ref[...], hidden).astype(kct_ref.dtype)


def nsa_compress(h1, bsz, seq, pe_k, pe_v, w1_k, w1_v, w2_k, w2_v):
    n = seq // CMP_STRIDE
    args = (pe_k, pe_v, w1_k.astype(BF16), w1_v.astype(BF16), w2_k.T.astype(BF16), w2_v.astype(BF16))
    full = lambda a: pl.BlockSpec(a.shape, lambda b: (0,) * a.ndim)
    return pl.pallas_call(
        _cmp_kernel,
        grid=(bsz,),
        in_specs=[pl.BlockSpec((seq, LANES), lambda b, g=g: (b, C1_KV // LANES + g))
                  for g in range(2 * KV_W // LANES)] + [full(a) for a in args],
        out_specs=[pl.BlockSpec((HKV, DK, n), lambda b: (b, 0, 0)), pl.BlockSpec((HKV, n, DK), lambda b: (b, 0, 0))],
        out_shape=[jax.ShapeDtypeStruct((bsz * HKV, DK, n), BF16), jax.ShapeDtypeStruct((bsz * HKV, n, DK), BF16)],
        compiler_params=_params(),
        name="nsa_compress",
    )(*([h1] * (2 * KV_W // LANES)), *args)


def _nsa_select_kernel(q_ref, gl_ref, gb_ref, kct_ref, vc_ref, bc_ref, ovt_ref, qsel_ref, ocg_ref, vt_s):
    qb = Q_BLOCK
    lane = lax.broadcasted_iota(jnp.int32, (qb, LANES), 1)
    jrow = lax.broadcasted_iota(jnp.int32, (MASK_ROWS, qb), 0).astype(F32)
    qlane = lax.broadcasted_iota(jnp.int32, (MASK_ROWS, qb), 1)
    sub = lax.broadcasted_iota(jnp.int32, (SUBLANES, qb), 0)
    nv = MASK_ROWS // SUBLANES
    ovt = ovt_ref[...]
    kct = kct_ref[0]
    vc = vc_ref[0]
    for sb in range(SEL_SUB):
        rows = slice(sb * qb, (sb + 1) * qb)
        blk = pl.program_id(2) * SEL_SUB + sb
        q = q_ref[rows, :] * (DK ** -0.5)
        gt = _sigmoid(gl_ref[rows, :] + gb_ref[...])
        qpad = []
        for g in range(GQA):
            t = q[:, (g // 2) * LANES:(g // 2 + 1) * LANES]
            if g % 2:
                t = pltpu.roll(t, DK, axis=1)
            qpad.append(jnp.where(lane < DK, t, 0.0))

        has_keys = blk * qb + lax.broadcasted_iota(jnp.int32, (qb, 1), 0) >= CMP_BLOCK - 1
        o_c = []
        psum = None
        for g in range(GQA):
            s = _dot(qpad[g][:, :DK].astype(BF16), kct) + bc_ref[0, sb, g * qb:(g + 1) * qb, :]
            m = jnp.max(s, axis=-1, keepdims=True)
            p = jnp.exp(s - m)
            p = p * jnp.where(has_keys, 1.0 / jnp.maximum(jnp.sum(p, axis=-1, keepdims=True), 1e-30), 0.0)
            o_c.append(gt[:, g:g + 1] * _dot(p.astype(BF16), vc))
            psum = p if psum is None else psum + p
        ocg_ref[rows, :] = jnp.concatenate(o_c, axis=1)

        p_hi = psum.astype(BF16)
        rem = psum - p_hi.astype(F32)
        p_mid = rem.astype(BF16)
        p_lo = (rem - p_mid.astype(F32)).astype(BF16)
        imp_t = _dot_nt(ovt, p_hi) + _dot_nt(ovt, p_mid) + _dot_nt(ovt, p_lo)

        qblk = ((blk * qb + qlane) >> int(math.log2(SLC_BLOCK))).astype(F32)
        val = jnp.where(jrow == 0.0, 3e38,
                        jnp.where(jrow == qblk, 3e38, jnp.where(jrow == qblk - 1.0, 3e38, imp_t)))
        val = jnp.where(jrow > qblk, -1.0, val)
        vt_s[sb] = val
        vals = [val[k * SUBLANES:(k + 1) * SUBLANES] for k in range(nv)]
        cnt = [jnp.zeros((SUBLANES, qb), F32) for _ in range(nv)]
        for jp in range(MASK_ROWS):
            rowv = jnp.broadcast_to(vt_s[sb, jp:jp + 1, :], (SUBLANES, qb))
            for k in range(nv):
                if k * SUBLANES > jp:
                    beat = jnp.where(rowv >= vals[k], 1.0, 0.0)
                elif (k + 1) * SUBLANES - 1 < jp:
                    beat = jnp.where(rowv > vals[k], 1.0, 0.0)
                else:
                    beat = jnp.where(sub > jp - k * SUBLANES, jnp.where(rowv >= vals[k], 1.0, 0.0),
                                     jnp.where(rowv > vals[k], 1.0, 0.0))
                cnt[k] = cnt[k] + beat
        selneg_t = jnp.concatenate([jnp.where(c < float(SLC_TOPK), 0.0, NEG) for c in cnt], axis=0)
        for g in range(GQA):
            qt = jnp.concatenate([qpad[g].T[:DK], selneg_t], axis=0)
            qsel_ref[0, sb, :, g * qb:(g + 1) * qb] = qt.astype(qsel_ref.dtype)


def _nsa_attn_kernel(qsel_ref, ocg_ref, gc_ref, gl_ref, gb_ref, ks_ref, vs_ref, gs_ref, kw_ref, vw_ref, bw_ref,
                     o_ref, s_s, p_s):
    qb = Q_BLOCK
    hq = GQA * qb
    nq = ATT_NQ * hq
    i = pl.program_id(2) * ATT_NQ
    step = SEL_TK // qb
    n_tiles = (i * qb) // SEL_TK + 1
    qt = jnp.concatenate([qsel_ref[0, b] for b in range(ATT_NQ)], axis=1)

    def stage_qk(j):
        k0 = pl.multiple_of(j * SEL_TK, SEL_TK)
        sc = _dot(ks_ref[pl.ds(k0, SEL_TK), :], qt)
        for b in range(ATT_NQ):
            dd = jnp.minimum(i + b - j * step, SEL_FAR)
            boff = pl.multiple_of((SEL_FAR - dd) * qb, qb)
            s_s[:, b * hq:(b + 1) * hq] = sc[:, b * hq:(b + 1) * hq] + gs_ref[0, pl.ds(boff, SEL_TK), :]

    def stage_softmax(m):
        m_new, alpha = [], []
        for g in range(ATT_NQ * GQA):
            cs = slice(g * qb, (g + 1) * qb)
            sc = s_s[:, cs]
            mg = jnp.maximum(m[:, cs], jnp.max(sc, axis=0, keepdims=True))
            alpha.append(jnp.exp(m[:, cs] - mg))
            p_s[:, cs] = jnp.exp(sc - mg).astype(BF16)
            m_new.append(mg)
        return jnp.concatenate(m_new, axis=1), jnp.concatenate(alpha, axis=1)

    def stage_pv(j, acc):
        k0 = pl.multiple_of(jnp.maximum(j, 0) * SEL_TK, SEL_TK)
        return acc + _dot(vs_ref[0, :V_ROWS, pl.ds(k0, SEL_TK)], p_s[...])

    def advance(j, m, acc):
        acc = stage_pv(j - 2, acc)
        m, alpha = stage_softmax(m)
        return m, alpha * acc

    def body(j, carry):
        m, acc = advance(j, *carry)
        stage_qk(j)
        return m, acc

    p_s[...] = jnp.zeros(p_s.shape, BF16)
    stage_qk(jnp.int32(0))
    init = (jnp.full((1, nq), NEG, F32), jnp.zeros((V_ROWS, nq), F32))
    m, acc = lax.fori_loop(1, n_tiles, body, init)
    _, acc = advance(n_tiles, m, acc)
    acc_s = stage_pv(n_tiles - 1, acc)

    wk = WINDOW + ATT_NQ * qb
    w0 = pl.multiple_of(jnp.maximum(i * qb - WINDOW, 0), qb)
    rowi = lax.broadcasted_iota(jnp.int32, (LANES, nq), 0)
    q_win = jnp.where(rowi < DK, qt, jnp.zeros_like(qt))
    sw = _dot(kw_ref[pl.ds(w0, wk), :], q_win) + bw_ref[0, 0]
    pw = jnp.exp(sw - jnp.max(sw, axis=0, keepdims=True)).astype(BF16)
    acc_w = _dot(vw_ref[0, :V_ROWS, pl.ds(w0, wk)], pw)

    pad_rows = jnp.zeros((LANES - DK, qb), F32)
    for b in range(ATT_NQ):
        rows = slice(b * qb, (b + 1) * qb)
        gt_t = _sigmoid(gl_ref[rows, :] + gb_ref[...]).T
        outs = []
        for g in range(GQA):
            cs = slice(b * hq + g * qb, b * hq + (g + 1) * qb)
            o_s = acc_s[:DK, cs] * (gt_t[GQA + g:GQA + g + 1, :] / acc_s[DK:DK + 1, cs])
            o_w = acc_w[:DK, cs] * (gt_t[2 * GQA + g:2 * GQA + g + 1, :] / acc_w[DK:DK + 1, cs])
            outs.append(jnp.concatenate([o_s + o_w, pad_rows], axis=0).T[:, :DK])
        o = ocg_ref[rows, :] + jnp.concatenate(outs, axis=1)
        o_ref[rows, :] = (o * _silu(gc_ref[rows, :])).astype(o_ref.dtype)


def _rel_bucket(dist):
    n = jnp.maximum(dist, 0)
    exact = REL_BUCKETS // 2
    nf = jnp.maximum(n, 1).astype(jnp.float32)
    large = exact + (jnp.log(nf / exact) / math.log(REL_MAX_DIST / exact)
                     * (REL_BUCKETS - exact)).astype(jnp.int32)
    return jnp.where(n < exact, n, jnp.minimum(large, REL_BUCKETS - 1))


def nsa_bias_tables(rel_bias, seq):
    qb = Q_BLOCK
    n_pad = seq // CMP_STRIDE
    c0 = n_pad - NA
    wc = 2 * n_pad
    ws = SEL_FAR * qb + SEL_TK
    nmax = max(seq, SEL_FAR * qb + qb)
    padl = CMP_STRIDE * (wc + 2)
    bvec = rel_bias[_rel_bucket(jnp.arange(nmax))].T.astype(F32)
    vext = jnp.concatenate([jnp.full((NSA_HEADS, padl), NEG, F32), bvec], axis=1)
    pos = jnp.arange(padl + nmax) - padl
    vwin = jnp.where(pos < WINDOW, vext, NEG)

    mlo, mhi = c0 - wc + 1, NA - 1 + c0
    start = padl + CMP_STRIDE * mlo - (CMP_BLOCK - 1)
    assert start >= 0
    u = vext[:, start:start + CMP_STRIDE * (mhi - mlo + 1)].reshape(NSA_HEADS, mhi - mlo + 1, CMP_STRIDE)
    urev = u[:, ::-1, :]
    gcb = jnp.stack([urev[:, NA - 1 - a:NA - 1 - a + wc, :] for a in range(NA)], axis=1)
    gcb = gcb.transpose(0, 1, 3, 2).reshape(HKV, GQA * qb, wc)
    nqb = seq // qb
    bc = jnp.stack([gcb[:, :, c0 - NA * i:c0 - NA * i + n_pad] for i in range(nqb)], axis=1)

    def toeplitz(vec, d0, width):
        vrev = vec[:, ::-1]
        s0 = nmax - d0 - qb
        period = width + qb - 1
        assert s0 >= 0 and s0 + period <= padl + nmax
        sl = vrev[:, s0:s0 + period]
        y = jnp.concatenate([sl[:, qb - 1:], sl[:, :qb - 1]], axis=1)
        z = jnp.tile(y, (1, qb))[:, :qb * (period - 1)].reshape(NSA_HEADS, qb, period - 1)
        return z[:, :, :width].reshape(HKV, GQA * qb, width)

    gs = toeplitz(vext, SEL_FAR * qb, ws)
    bw = tuple(tuple(toeplitz(vwin, first + b * qb, WINDOW + ATT_NQ * qb) for b in range(ATT_NQ))
               for first in (0, WINDOW))
    return bc, gs, bw


def _overlap_matrix_t(n_pad):
    c_start = np.arange(n_pad)[None, :] * CMP_STRIDE
    s_start = np.arange(MASK_ROWS)[:, None] * SLC_BLOCK
    ov = np.clip(np.minimum(c_start + CMP_BLOCK, s_start + SLC_BLOCK) - np.maximum(c_start, s_start), 0, None)
    return ov.astype(np.float32) / CMP_BLOCK


def nsa_branch(h1, h2, kv, hgl, gate_b4, kct, vc, tables, bsz, seq):
    qb = Q_BLOCK
    nqb = seq // qb
    n_pad = seq // CMP_STRIDE
    n_slc = seq // SLC_BLOCK
    assert n_slc <= MASK_ROWS and seq % SEL_TK == 0 and nqb % SEL_SUB == 0
    assert (SEL_TK // qb) % ATT_NQ == 0 and nqb % ATT_NQ == 0
    bc, gs, bw = tables
    m = bsz * seq
    gw = GQA * DK

    sq = SEL_SUB * qb
    nsq = seq // sq
    ovt = jnp.asarray(_overlap_matrix_t(n_pad), BF16)
    qsel, ocg = pl.pallas_call(
        _nsa_select_kernel,
        grid=(HKV, bsz, nsq),
        in_specs=[
            pl.BlockSpec((sq, gw), lambda hh, b, i: (b * nsq + i, C1_QC // gw + hh)),
            pl.BlockSpec((sq, LANES), lambda hh, b, i: (b * nsq + i, hh)),
            pl.BlockSpec((1, LANES), lambda hh, b, i: (0, hh)),
            pl.BlockSpec((1, DK, n_pad), lambda hh, b, i: (b * HKV + hh, 0, 0)),
            pl.BlockSpec((1, n_pad, DK), lambda hh, b, i: (b * HKV + hh, 0, 0)),
            pl.BlockSpec((1, SEL_SUB) + bc.shape[2:], lambda hh, b, i: (hh, i, 0, 0)),
            pl.BlockSpec(ovt.shape, lambda hh, b, i: (0, 0)),
        ],
        out_specs=[pl.BlockSpec((1, SEL_SUB, LANES, GQA * qb), lambda hh, b, i: (hh, b * nsq + i, 0, 0)),
                   pl.BlockSpec((sq, gw), lambda hh, b, i: (b * nsq + i, hh))],
        out_shape=[jax.ShapeDtypeStruct((HKV, bsz * nqb, LANES, GQA * qb), BF16),
                   jax.ShapeDtypeStruct((m, D_BR), F32)],
        scratch_shapes=[pltpu.VMEM((SEL_SUB, MASK_ROWS, qb), F32)],
        compiler_params=_params(),
        name="nsa_select",
    )(h1, hgl, gate_b4, kct, vc, bc, ovt)

    gs_t = gs.transpose(0, 2, 1)
    bw_t = jnp.stack([jnp.concatenate([w.transpose(0, 2, 1) for w in variant], axis=2) for variant in bw],
                     axis=1)
    ks_ext, kw_ext, vs_ext, vw_ext = kv

    aq = ATT_NQ * qb
    npair = nqb // ATT_NQ
    rowblk = lambda width, c0: pl.BlockSpec((aq, width), lambda hh, b, i: (b * npair + i, c0 // width + hh))
    keys = pl.BlockSpec((seq, LANES), lambda hh, b, i: (b, hh))
    vals = pl.BlockSpec((1, LANES, seq), lambda hh, b, i: (hh, 0, b))
    return pl.pallas_call(
        _nsa_attn_kernel,
        grid=(HKV, bsz, npair),
        in_specs=[
            pl.BlockSpec((1, ATT_NQ, LANES, GQA * qb), lambda hh, b, i: (hh, b * npair + i, 0, 0)),
            rowblk(gw, 0), rowblk(gw, C2_GC),
            pl.BlockSpec((aq, LANES), lambda hh, b, i: (b * npair + i, hh)),
            pl.BlockSpec((1, LANES), lambda hh, b, i: (0, hh)),
            keys, vals, pl.BlockSpec((1,) + gs_t.shape[1:], lambda hh, b, i: (hh, 0, 0)),
            keys, vals, pl.BlockSpec((1, 1) + bw_t.shape[2:], lambda hh, b, i: (hh, jnp.minimum(i, 1), 0, 0)),
        ],
        out_specs=pl.BlockSpec((aq, gw), lambda hh, b, i: (b * npair + i, hh)),
        out_shape=jax.ShapeDtypeStruct((m, D_BR), BF16),
        scratch_shapes=[pltpu.VMEM((SEL_TK, ATT_NQ * GQA * qb), F32), pltpu.VMEM((SEL_TK, ATT_NQ * GQA * qb), BF16)],
        compiler_params=_params(),
        name="nsa_attn",
    )(qsel, ocg, h2, hgl, gate_b4, ks_ext, vs_ext, gs_t, kw_ext, vw_ext, bw_t)


def _final_kernel(oa_ref, ob_ref, oc_ref, om_ref, *rest):
    gate_refs = rest[:2 * N_BRANCHES]
    x_ref, wb_ref, wo_ref, lng_ref, lnb_ref, y_ref, yb_ref = rest[2 * N_BRANCHES:]
    o_refs = (oa_ref, ob_ref, oc_ref, om_ref)
    halves = []
    for c in range(2):
        cs = slice(c * D_BR, (c + 1) * D_BR)
        acc = None
        for k in range(N_BRANCHES):
            term = _sigmoid(gate_refs[2 * k + c][...]) * _dot(o_refs[k][...], wb_ref[k, :, cs])
            acc = term if acc is None else acc + term
        halves.append(acc.astype(BF16))
    merged = jnp.concatenate(halves, axis=1)
    z = DN_ALPHA * x_ref[...] + _dot(merged, wo_ref[...])
    mu = jnp.mean(z, axis=-1, keepdims=True)
    zc = z - mu
    var = jnp.mean(zc * zc, axis=-1, keepdims=True)
    y = zc * lax.rsqrt(var + LN_EPS) * lng_ref[...] + lnb_ref[...]
    y_ref[...] = y
    yb_ref[...] = y.astype(BF16)


def final_merge(o_a, o_b, o_c, o_m, h2, x, w_branch, w_out, ln_g, ln_b, tm=128):
    m = x.shape[0]
    tm = min(tm, m)
    br = pl.BlockSpec((tm, D_BR), lambda i: (i, 0))
    gate = lambda k: pl.BlockSpec((tm, D_BR), lambda i, k=k: (i, C2_MERGE // D_BR + k))
    xs = pl.BlockSpec((tm, D_MODEL), lambda i: (i, 0))
    resident = lambda a: pl.BlockSpec(a.shape, lambda i: (0,) * a.ndim, pipeline_mode=pl.Buffered(1))
    wb = w_branch.astype(BF16)
    wo = w_out.astype(BF16)
    lg, lb = ln_g.reshape(1, D_MODEL), ln_b.reshape(1, D_MODEL)
    ngate = 2 * N_BRANCHES
    return pl.pallas_call(
        _final_kernel,
        grid=(m // tm,),
        in_specs=[br, br, br, br] + [gate(k) for k in range(ngate)] + [xs]
                 + [resident(wb), resident(wo), resident(lg), resident(lb)],
        out_specs=[xs, xs],
        out_shape=[jax.ShapeDtypeStruct((m, D_MODEL), F32), jax.ShapeDtypeStruct((m, D_MODEL), BF16)],
        compiler_params=_params(),
        name="merge_out_ln",
    )(o_a, o_b, o_c, o_m, *([h2] * ngate), x, wb, wo, lg, lb)


def _gate_spread_matrix():
    p = np.zeros((3 * NSA_HEADS, HKV * LANES), np.float32)
    for hh in range(HKV):
        for brn in range(3):
            for g in range(GQA):
                p[brn * NSA_HEADS + hh * GQA + g, hh * LANES + brn * GQA + g] = 1.0
    return p


def _spread_gate_cols(a):
    return jnp.dot(a, jnp.asarray(_gate_spread_matrix()), precision=lax.Precision.HIGHEST)


def layer(l, x, xb, mem_b, tables, bsz, seq, w_in, sgu_ln_g, sgu_ln_b, sgu_w, sgu_b, conv_w, conv_b,
          lru_wa, lru_ba, lru_wx, lru_bx, lru_lambda, cmp_pe_k, cmp_pe_v, cmp_w1_k, cmp_w1_v,
          cmp_w2_k, cmp_w2_v, nsa_gate_b, w_mem_kv, w_branch, w_out, ln_g, ln_b):
    m = bsz * seq
    tm = min(1024, m)
    tm_big = min(2048, m)
    h1 = matmul_f32w(xb, w_in, l, 0, N1, F32, tm_big, 512, w_is_nk=True)
    kv = kv_projection(xb, w_in, l, N1, seq, tm)
    h2 = matmul_f32w(xb, w_in, l, GL_OFF + 3 * NSA_HEADS, N2, F32, tm, 1024, w_is_nk=True,
                     w_single_buffer=True)
    w_gl = _spread_gate_cols(w_in[l, GL_OFF:GL_OFF + 3 * NSA_HEADS, :].T).astype(BF16)
    hgl = matmul(xb, w_gl, F32, tm, HKV * LANES)
    gate_b4 = _spread_gate_cols(nsa_gate_b[l].reshape(1, 3 * NSA_HEADS))

    o_a = gmlp_branch(h1, sgu_ln_g[l], sgu_ln_b[l], sgu_w[l], sgu_b[l])
    o_b = lru_branch(h1, bsz, seq, conv_w[l], conv_b[l], lru_wa[l], lru_ba[l], lru_wx[l], lru_bx[l],
                     lru_lambda[l])

    kct, vc = nsa_compress(h1, bsz, seq, cmp_pe_k[l], cmp_pe_v[l],
                           cmp_w1_k[l], cmp_w1_v[l], cmp_w2_k[l], cmp_w2_v[l])
    o_c = nsa_branch(h1, h2, kv, hgl, gate_b4, kct, vc, tables, bsz, seq)

    mrows = mem_b.shape[0]
    kv = matmul_f32w(mem_b, w_mem_kv, l, 0, 2 * D_BR, BF16, min(512, mrows), 512)
    o_m = mem_branch(h2, kv, bsz, seq)
    return final_merge(o_a, o_b, o_c, o_m, h2, x, w_branch[l], w_out[l], ln_g[l], ln_b[l])


def kernel(x, mem, rel_bias, w_in, sgu_ln_g, sgu_ln_b, sgu_w, sgu_b, conv_w, conv_b, lru_wa, lru_ba, lru_wx,
           lru_bx, lru_lambda, cmp_pe_k, cmp_pe_v, cmp_w1_k, cmp_w1_v, cmp_w2_k, cmp_w2_v, nsa_gate_b,
           w_mem_kv, w_branch, w_out, ln_g, ln_b):
    bsz, seq, _ = x.shape
    tables = nsa_bias_tables(rel_bias, seq)
    xf = x.reshape(bsz * seq, D_MODEL)
    xb = xf.astype(BF16)
    mem_b = mem.reshape(-1, D_MODEL).astype(BF16)
    w_in_t = jnp.swapaxes(w_in, 1, 2)
    params = (w_in_t, sgu_ln_g, sgu_ln_b, sgu_w, sgu_b, conv_w, conv_b, lru_wa, lru_ba, lru_wx, lru_bx,
              lru_lambda, cmp_pe_k, cmp_pe_v, cmp_w1_k, cmp_w1_v, cmp_w2_k, cmp_w2_v, nsa_gate_b,
              w_mem_kv, w_branch, w_out, ln_g, ln_b)
    for l in range(w_in.shape[0]):
        xf, xb = layer(l, xf, xb, mem_b, tables, bsz, seq, *params)
    return xf.reshape(bsz, seq, D_MODEL)
```

```python
import functools
import math

import numpy as np
import jax
import jax.numpy as jnp
from jax import lax
from jax.experimental import pallas as pl
from jax.experimental.pallas import tpu as pltpu

F32 = jnp.float32
BF16 = jnp.bfloat16

D_MODEL = 2048
DEPTH = 2
D_BR = D_MODEL // 2
N_BRANCHES = 4
GMLP_CHUNK = 128
GMLP_GROUPS = 8
LRU_BLOCKS = 8
LRU_BLOCK_DIM = D_BR // LRU_BLOCKS
CONV_WIDTH = 4
LRU_C = 8.0
DK = 64
NSA_HEADS = D_BR // DK
HKV = NSA_HEADS // 4
GQA = NSA_HEADS // HKV
KV_W = HKV * DK
CMP_BLOCK = 32
CMP_STRIDE = 16
CMP_HIDDEN = 256
SLC_BLOCK = 64
SLC_TOPK = 8
WINDOW = 256
Q_BLOCK = 128
MEM_HEADS = 4
MEM_HEAD_DIM = D_BR // MEM_HEADS
REL_BUCKETS = 32
REL_MAX_DIST = 1024
DN_ALPHA = (2 * DEPTH) ** 0.25
LN_EPS = 1e-5

LANES = 128
SUBLANES = 8
NEG = -1e30
SEL_TK = 512
SEL_FAR = 11
MASK_ROWS = LANES - DK
NA = Q_BLOCK // CMP_STRIDE
SEL_SUB = 4
ATT_NQ = 2
V_ROWS = DK + 16
VMEM_LIMIT = 56 * 1024 * 1024

C1_U, C1_V, C1_GA, C1_XB, C1_GB, C1_QC = (D_BR * k for k in range(6))
C1_KV = 6 * D_BR
N1 = C1_KV + 2 * KV_W
NKV = 4 * KV_W
C2_GC, C2_QM, C2_GM, C2_MERGE = (D_BR * k for k in range(4))
N2 = C2_MERGE + N_BRANCHES * D_MODEL
GL_OFF = N1 + NKV


def _sigmoid(x):
    return 1.0 / (1.0 + jnp.exp(-x))


def _silu(x):
    return x * _sigmoid(x)


def _gelu_tanh(x):
    return 0.5 * x * (1.0 + jnp.tanh(math.sqrt(2.0 / math.pi) * (x + 0.044715 * (x * x * x))))


def _dot(a, b):
    return jnp.dot(a, b, preferred_element_type=F32)


def _dot_nt(a, b):
    return lax.dot_general(a, b, (((1,), (1,)), ((), ())), preferred_element_type=F32)


def _params(**kw):
    return pltpu.CompilerParams(vmem_limit_bytes=VMEM_LIMIT, **kw)


def _mm_kernel(x_ref, w_ref, o_ref):
    o_ref[...] = _dot(x_ref[...], w_ref[...]).astype(o_ref.dtype)


def matmul(x, w, out_dtype, tm, tn):
    m, k = x.shape
    n = w.shape[1]
    assert m % tm == 0 and n % tn == 0, (m, n, tm, tn)
    return pl.pallas_call(
        _mm_kernel,
        grid=(n // tn, m // tm),
        in_specs=[pl.BlockSpec((tm, k), lambda j, i: (i, 0)),
                  pl.BlockSpec((k, tn), lambda j, i: (0, j))],
        out_specs=pl.BlockSpec((tm, tn), lambda j, i: (i, j)),
        out_shape=jax.ShapeDtypeStruct((m, n), out_dtype),
        compiler_params=_params(),
        name="matmul",
    )(x, w)


def _mm_castw_kernel(x_ref, *rest, w_is_nk, shift):
    w_refs, (o_ref, wb_ref) = rest[:-2], rest[-2:]

    @pl.when(pl.program_id(1) == 0)
    def _():
        if shift:
            w = jnp.concatenate([w_refs[0][0, shift:, :], w_refs[1][0, :shift, :]], axis=0)
        else:
            w = w_refs[0][0]
        wb_ref[...] = (w.T if w_is_nk else w).astype(BF16)

    o_ref[...] = _dot(x_ref[...], wb_ref[...]).astype(o_ref.dtype)


def matmul_f32w(x, w3, layer_idx, col0, n, out_dtype, tm, tn, w_is_nk=False, w_single_buffer=False):
    m, k = x.shape
    ncols = w3.shape[1] if w_is_nk else w3.shape[2]
    shift = col0 % tn
    assert m % tm == 0 and n % tn == 0 and col0 + n <= ncols, (m, n, col0, tm, tn)
    assert shift == 0 or (w_is_nk and shift % SUBLANES == 0), (col0, tn)
    j0 = col0 // tn
    mode = dict(pipeline_mode=pl.Buffered(1)) if w_single_buffer else {}
    if w_is_nk:
        w_specs = [pl.BlockSpec((1, tn, k), lambda j, i, d=d: (layer_idx, j0 + j + d, 0), **mode)
                   for d in range(2 if shift else 1)]
    else:
        w_specs = [pl.BlockSpec((1, k, tn), lambda j, i: (layer_idx, 0, j0 + j), **mode)]
    return pl.pallas_call(
        functools.partial(_mm_castw_kernel, w_is_nk=w_is_nk, shift=shift),
        grid=(n // tn, m // tm),
        in_specs=[pl.BlockSpec((tm, k), lambda j, i: (i, 0))] + w_specs,
        out_specs=pl.BlockSpec((tm, tn), lambda j, i: (i, j)),
        out_shape=jax.ShapeDtypeStruct((m, n), out_dtype),
        scratch_shapes=[pltpu.VMEM((k, tn), BF16)],
        compiler_params=_params(dimension_semantics=("arbitrary", "arbitrary")),
        name="matmul_f32w",
    )(x, *([w3] * len(w_specs)))


def _kv_proj_kernel(x_ref, ws_ref, ww_ref, ks_ref, kw_ref, vs_ref, vw_ref, wk_s, wv_s, *, tm, seq):
    @pl.when(pl.program_id(0) == 0)
    def _():
        wk = jnp.concatenate([ws_ref[0, 0:KV_W], ww_ref[0, 0:KV_W]], axis=0)
        wk_s[...] = wk.T.astype(BF16)
        wv_s[...] = jnp.concatenate([ws_ref[0, KV_W:2 * KV_W], ww_ref[0, KV_W:2 * KV_W]], axis=0).astype(BF16)

    x = x_ref[...]
    kk = _dot(x, wk_s[...])
    vt = _dot_nt(wv_s[...], x)
    pos = (pl.program_id(0) * tm + lax.broadcasted_iota(jnp.int32, (tm, MASK_ROWS), 0)) % seq
    blk = lax.broadcasted_iota(jnp.int32, (tm, MASK_ROWS), 1)
    onehot = jnp.where((pos >> int(math.log2(SLC_BLOCK))) == blk, 1.0, 0.0)
    zeros = jnp.zeros((tm, MASK_ROWS), F32)
    tail = jnp.where(lax.broadcasted_iota(jnp.int32, (LANES - DK, tm), 0) == 0, 1.0, 0.0)
    for h in range(HKV):
        ks_ref[:, h * LANES:(h + 1) * LANES] = jnp.concatenate(
            [kk[:, h * DK:(h + 1) * DK], onehot], axis=1).astype(ks_ref.dtype)
        kw_ref[:, h * LANES:(h + 1) * LANES] = jnp.concatenate(
            [kk[:, KV_W + h * DK:KV_W + (h + 1) * DK], zeros], axis=1).astype(kw_ref.dtype)
        vs_ref[h] = jnp.concatenate([vt[h * DK:(h + 1) * DK], tail], axis=0).astype(vs_ref.dtype)
        vw_ref[h] = jnp.concatenate([vt[KV_W + h * DK:KV_W + (h + 1) * DK], tail], axis=0).astype(vw_ref.dtype)


def kv_projection(x, wt3, layer_idx, row0, seq, tm):
    m, k = x.shape
    nrows = 2 * KV_W
    assert m % tm == 0 and row0 % nrows == 0 and seq % tm == 0
    wspec = lambda d: pl.BlockSpec((1, nrows, k), lambda i: (layer_idx, row0 // nrows + d, 0),
                                   pipeline_mode=pl.Buffered(1))
    keys = jax.ShapeDtypeStruct((m, HKV * LANES), BF16)
    vals = jax.ShapeDtypeStruct((HKV, LANES, m), BF16)
    return pl.pallas_call(
        functools.partial(_kv_proj_kernel, tm=tm, seq=seq),
        grid=(m // tm,),
        in_specs=[pl.BlockSpec((tm, k), lambda i: (i, 0)), wspec(0), wspec(1)],
        out_specs=[pl.BlockSpec((tm, HKV * LANES), lambda i: (i, 0))] * 2
                  + [pl.BlockSpec((HKV, LANES, tm), lambda i: (0, 0, i))] * 2,
        out_shape=[keys, keys, vals, vals],
        scratch_shapes=[pltpu.VMEM((k, 2 * KV_W), BF16), pltpu.VMEM((2 * KV_W, k), BF16)],
        compiler_params=_params(dimension_semantics=("arbitrary",)),
        name="kv_projection",
    )(x, wt3, wt3)


def _gmlp_kernel(u_ref, v_ref, ga_ref, lng_ref, lnb_ref, w_ref, bs_ref, o_ref, *, rows):
    gd = D_BR // GMLP_GROUPS
    for c in range(rows // GMLP_CHUNK):
        r = slice(c * GMLP_CHUNK, (c + 1) * GMLP_CHUNK)
        v = _gelu_tanh(v_ref[r, :])
        mu = jnp.mean(v, axis=-1, keepdims=True)
        vc = v - mu
        var = jnp.mean(vc * vc, axis=-1, keepdims=True)
        vb = (vc * lax.rsqrt(var + LN_EPS) * lng_ref[...] + lnb_ref[...]).astype(BF16)
        u = _gelu_tanh(u_ref[r, :]) * _silu(ga_ref[r, :])
        for g in range(GMLP_GROUPS):
            cs = slice(g * gd, (g + 1) * gd)
            mixed = _dot(w_ref[g], vb[:, cs]) + bs_ref[:, g:g + 1]
            o_ref[r, cs] = (u[:, cs] * mixed).astype(o_ref.dtype)


def gmlp_branch(h1, ln_g, ln_b, w_s, b_s, rows=512):
    m = h1.shape[0]
    rows = min(rows, m)
    causal = jnp.tril(jnp.ones((GMLP_CHUNK, GMLP_CHUNK), dtype=bool))
    w = jnp.where(causal, w_s, 0).astype(BF16)
    col = lambda c: pl.BlockSpec((rows, D_BR), lambda i, c=c: (i, c // D_BR))
    full = lambda a: pl.BlockSpec(a.shape, lambda i: (0,) * a.ndim)
    args = (ln_g.reshape(1, D_BR), ln_b.reshape(1, D_BR), w, b_s.T)
    return pl.pallas_call(
        functools.partial(_gmlp_kernel, rows=rows),
        grid=(m // rows,),
        in_specs=[col(C1_U), col(C1_V), col(C1_GA)] + [full(a) for a in args],
        out_specs=pl.BlockSpec((rows, D_BR), lambda i: (i, 0)),
        out_shape=jax.ShapeDtypeStruct((m, D_BR), BF16),
        compiler_params=_params(),
        name="gmlp",
    )(h1, h1, h1, *args)


def _lru_kernel(xb_ref, gb_ref, cw_ref, cb_ref, wa_ref, ba_ref, wx_ref, bx_ref, lam_ref, o_ref,
                xbuf, hcarry, a_s, g_s, *, ts):
    @pl.when(pl.program_id(1) == 0)
    def _():
        xbuf[0:8, :] = jnp.zeros((8, D_BR), F32)
        hcarry[...] = jnp.zeros((8, D_BR), F32)

    xbuf[8:8 + ts, :] = xb_ref[...]
    xc = cb_ref[...] + cw_ref[0:1, :] * xbuf[pl.ds(8 - (CONV_WIDTH - 1), ts), :]
    for k in range(1, CONV_WIDTH):
        xc = xc + cw_ref[k:k + 1, :] * xbuf[pl.ds(8 - (CONV_WIDTH - 1) + k, ts), :]
    xbuf[0:8, :] = xbuf[ts:ts + 8, :]

    xcb = xc.astype(BF16)
    bd = LRU_BLOCK_DIM
    r = jnp.concatenate([_dot(xcb[:, n * bd:(n + 1) * bd], wa_ref[n]) for n in range(LRU_BLOCKS)], axis=1)
    i = jnp.concatenate([_dot(xcb[:, n * bd:(n + 1) * bd], wx_ref[n]) for n in range(LRU_BLOCKS)], axis=1)
    r = _sigmoid(r + ba_ref[...])
    i = _sigmoid(i + bx_ref[...])
    nl = -lam_ref[...]
    softplus = jnp.maximum(nl, 0.0) + jnp.log1p(jnp.exp(-jnp.abs(nl)))
    log_a = (-LRU_C * softplus) * r
    a = jnp.exp(log_a)
    a_s[...] = a
    g_s[...] = jnp.sqrt(1.0 - a * a) * i * xc

    row = lax.broadcasted_iota(jnp.int32, (8, D_BR), 0)

    def body(j, carry):
        r0 = pl.multiple_of(j * 8, 8)
        av = a_s[pl.ds(r0, 8), :]
        bv = g_s[pl.ds(r0, 8), :]
        for d in (1, 2, 4):
            keep = row >= d
            a_sh = pltpu.roll(av, d, axis=0)
            b_sh = pltpu.roll(bv, d, axis=0)
            bv = jnp.where(keep, av * b_sh + bv, bv)
            av = jnp.where(keep, av * a_sh, av)
        hv = av * carry + bv
        g_s[pl.ds(r0, 8), :] = hv
        return jnp.broadcast_to(hv[7:8, :], (8, D_BR))

    hcarry[...] = lax.fori_loop(0, ts // 8, body, hcarry[...])
    o_ref[...] = (g_s[...] * _silu(gb_ref[...])).astype(o_ref.dtype)


def lru_branch(h1, bsz, seq, conv_w, conv_b, wa, ba, wx, bx, lam, ts=512):
    ts = min(ts, seq)
    ns = seq // ts
    col = lambda c: pl.BlockSpec((ts, D_BR), lambda b, s, c=c: (b * ns + s, c // D_BR))
    full = lambda a: pl.BlockSpec(a.shape, lambda b, s: (0,) * a.ndim)
    row = lambda a: a.reshape(1, D_BR)
    args = (conv_w, row(conv_b), wa.astype(BF16), row(ba), wx.astype(BF16), row(bx), row(lam))
    return pl.pallas_call(
        functools.partial(_lru_kernel, ts=ts),
        grid=(bsz, ns),
        in_specs=[col(C1_XB), col(C1_GB)] + [full(a) for a in args],
        out_specs=pl.BlockSpec((ts, D_BR), lambda b, s: (b * ns + s, 0)),
        out_shape=jax.ShapeDtypeStruct((bsz * seq, D_BR), BF16),
        scratch_shapes=[pltpu.VMEM((ts + 8, D_BR), F32), pltpu.VMEM((8, D_BR), F32),
                        pltpu.VMEM((ts, D_BR), F32), pltpu.VMEM((ts, D_BR), F32)],
        compiler_params=_params(dimension_semantics=("arbitrary", "arbitrary")),
        name="lru",
    )(h1, h1, *args)


def _mem_kernel(q_ref, g_ref, kv_ref, o_ref):
    hd = MEM_HEAD_DIM
    for hh in range(MEM_HEADS):
        cs = slice(hh * hd, (hh + 1) * hd)
        q = (q_ref[:, cs] * (hd ** -0.5)).astype(BF16)
        s = _dot_nt(q, kv_ref[:, cs])
        p = jnp.exp(s - jnp.max(s, axis=-1, keepdims=True))
        l = jnp.sum(p, axis=-1, keepdims=True)
        o = _dot(p.astype(BF16), kv_ref[:, D_BR + hh * hd:D_BR + (hh + 1) * hd]) / l
        o_ref[:, cs] = (o * _silu(g_ref[:, cs])).astype(o_ref.dtype)


def mem_branch(h2, kv, bsz, seq, tq=512):
    tq = min(tq, seq)
    nq = seq // tq
    mlen = kv.shape[0] // bsz
    col = lambda c: pl.BlockSpec((tq, D_BR), lambda b, i, c=c: (b * nq + i, c // D_BR))
    return pl.pallas_call(
        _mem_kernel,
        grid=(bsz, nq),
        in_specs=[col(C2_QM), col(C2_GM), pl.BlockSpec((mlen, 2 * D_BR), lambda b, i: (b, 0))],
        out_specs=pl.BlockSpec((tq, D_BR), lambda b, i: (b * nq + i, 0)),
        out_shape=jax.ShapeDtypeStruct((bsz * seq, D_BR), BF16),
        compiler_params=_params(),
        name="mem_attn",
    )(h2, h2, kv)


def _cmp_kernel(*refs):
    ngrp = 2 * KV_W // LANES
    t_refs = refs[:ngrp]
    pek_ref, pev_ref, w1k_ref, w1v_ref, w2kt_ref, w2v_ref, kct_ref, vc_ref = refs[ngrp:]
    n = t_refs[0].shape[0] // CMP_STRIDE
    for h in range(HKV):
        for is_v, (pe_ref, w1_ref) in enumerate(((pek_ref, w1k_ref), (pev_ref, w1v_ref))):
            grp, off = divmod(is_v * KV_W + h * DK, LANES)
            lo = hi = None
            for tok in range(CMP_STRIDE):
                x = t_refs[grp][pl.ds(tok, n, stride=CMP_STRIDE), :][:, off:off + DK]
                t2 = CMP_STRIDE + tok
                a = _dot((x + pe_ref[tok:tok + 1, :]).astype(BF16), w1_ref[tok * DK:(tok + 1) * DK, :])
                b = _dot((x + pe_ref[t2:t2 + 1, :]).astype(BF16), w1_ref[t2 * DK:(t2 + 1) * DK, :])
                lo = a if lo is None else lo + a
                hi = b if hi is None else hi + b
            hidden = _silu(lo + pltpu.roll(hi, n - 1, axis=0)).astype(BF16)
            if is_v:
                vc_ref[h] = _dot(hidden, w2v_ref[...]).astype(vc_ref.dtype)
            else:
                kct_ref[h] = _dot_nt(w2kt_ref[...], hidden).astype(kct_ref.dtype)


def nsa_compress(h1, bsz, seq, pe_k, pe_v, w1_k, w1_v, w2_k, w2_v):
    n = seq // CMP_STRIDE
    args = (pe_k, pe_v, w1_k.astype(BF16), w1_v.astype(BF16), w2_k.T.astype(BF16), w2_v.astype(BF16))
    full = lambda a: pl.BlockSpec(a.shape, lambda b: (0,) * a.ndim)
    return pl.pallas_call(
        _cmp_kernel,
        grid=(bsz,),
        in_specs=[pl.BlockSpec((seq, LANES), lambda b, g=g: (b, C1_KV // LANES + g))
                  for g in range(2 * KV_W // LANES)] + [full(a) for a in args],
        out_specs=[pl.BlockSpec((HKV, DK, n), lambda b: (b, 0, 0)), pl.BlockSpec((HKV, n, DK), lambda b: (b, 0, 0))],
        out_shape=[jax.ShapeDtypeStruct((bsz * HKV, DK, n), BF16), jax.ShapeDtypeStruct((bsz * HKV, n, DK), BF16)],
        compiler_params=_params(),
        name="nsa_compress",
    )(*([h1] * (2 * KV_W // LANES)), *args)


def _nsa_select_kernel(q_ref, gl_ref, gb_ref, kct_ref, vc_ref, bc_ref, ovt_ref, qsel_ref, ocg_ref, vt_s):
    qb = Q_BLOCK
    lane = lax.broadcasted_iota(jnp.int32, (qb, LANES), 1)
    jrow = lax.broadcasted_iota(jnp.int32, (MASK_ROWS, qb), 0).astype(F32)
    qlane = lax.broadcasted_iota(jnp.int32, (MASK_ROWS, qb), 1)
    sub = lax.broadcasted_iota(jnp.int32, (SUBLANES, qb), 0)
    nv = MASK_ROWS // SUBLANES
    ovt = ovt_ref[...]
    kct = kct_ref[0]
    vc = vc_ref[0]
    for sb in range(SEL_SUB):
        rows = slice(sb * qb, (sb + 1) * qb)
        blk = pl.program_id(2) * SEL_SUB + sb
        q = q_ref[rows, :] * (DK ** -0.5)
        gt = _sigmoid(gl_ref[rows, :] + gb_ref[...])
        qpad = []
        for g in range(GQA):
            t = q[:, (g // 2) * LANES:(g // 2 + 1) * LANES]
            if g % 2:
                t = pltpu.roll(t, DK, axis=1)
            qpad.append(jnp.where(lane < DK, t, 0.0))

        has_keys = blk * qb + lax.broadcasted_iota(jnp.int32, (qb, 1), 0) >= CMP_BLOCK - 1
        o_c = []
        psum = None
        for g in range(GQA):
            s = _dot(qpad[g][:, :DK].astype(BF16), kct) + bc_ref[0, sb, g * qb:(g + 1) * qb, :]
            m = jnp.max(s, axis=-1, keepdims=True)
            p = jnp.exp(s - m)
            p = p * jnp.where(has_keys, 1.0 / jnp.maximum(jnp.sum(p, axis=-1, keepdims=True), 1e-30), 0.0)
            o_c.append(gt[:, g:g + 1] * _dot(p.astype(BF16), vc))
            psum = p if psum is None else psum + p
        ocg_ref[rows, :] = jnp.concatenate(o_c, axis=1)

        p_hi = psum.astype(BF16)
        rem = psum - p_hi.astype(F32)
        p_mid = rem.astype(BF16)
        p_lo = (rem - p_mid.astype(F32)).astype(BF16)
        imp_t = _dot_nt(ovt, p_hi) + _dot_nt(ovt, p_mid) + _dot_nt(ovt, p_lo)

        qblk = ((blk * qb + qlane) >> int(math.log2(SLC_BLOCK))).astype(F32)
        val = jnp.where(jrow == 0.0, 3e38,
                        jnp.where(jrow == qblk, 3e38, jnp.where(jrow == qblk - 1.0, 3e38, imp_t)))
        val = jnp.where(jrow > qblk, -1.0, val)
        vt_s[sb] = val
        vals = [val[k * SUBLANES:(k + 1) * SUBLANES] for k in range(nv)]
        cnt = [jnp.zeros((SUBLANES, qb), F32) for _ in range(nv)]
        for jp in range(MASK_ROWS):
            rowv = jnp.broadcast_to(vt_s[sb, jp:jp + 1, :], (SUBLANES, qb))
            for k in range(nv):
                if k * SUBLANES > jp:
                    beat = jnp.where(rowv >= vals[k], 1.0, 0.0)
                elif (k + 1) * SUBLANES - 1 < jp:
                    beat = jnp.where(rowv > vals[k], 1.0, 0.0)
                else:
                    beat = jnp.where(sub > jp - k * SUBLANES, jnp.where(rowv >= vals[k], 1.0, 0.0),
                                     jnp.where(rowv > vals[k], 1.0, 0.0))
                cnt[k] = cnt[k] + beat
        selneg_t = jnp.concatenate([jnp.where(c < float(SLC_TOPK), 0.0, NEG) for c in cnt], axis=0)
        for g in range(GQA):
            qt = jnp.concatenate([qpad[g].T[:DK], selneg_t], axis=0)
            qsel_ref[0, sb, :, g * qb:(g + 1) * qb] = qt.astype(qsel_ref.dtype)


def _nsa_attn_kernel(qsel_ref, ocg_ref, gc_ref, gl_ref, gb_ref, ks_ref, vs_ref, gs_ref, kw_ref, vw_ref, bw_ref,
                     o_ref, s_s, p_s):
    qb = Q_BLOCK
    hq = GQA * qb
    nq = ATT_NQ * hq
    i = pl.program_id(2) * ATT_NQ
    step = SEL_TK // qb
    n_tiles = (i * qb) // SEL_TK + 1
    qt = jnp.concatenate([qsel_ref[0, b] for b in range(ATT_NQ)], axis=1)

    def stage_qk(j):
        k0 = pl.multiple_of(j * SEL_TK, SEL_TK)
        sc = _dot(ks_ref[pl.ds(k0, SEL_TK), :], qt)
        for b in range(ATT_NQ):
            dd = jnp.minimum(i + b - j * step, SEL_FAR)
            boff = pl.multiple_of((SEL_FAR - dd) * qb, qb)
            s_s[:, b * hq:(b + 1) * hq] = sc[:, b * hq:(b + 1) * hq] + gs_ref[0, pl.ds(boff, SEL_TK), :]

    def stage_softmax(m):
        m_new, alpha = [], []
        for g in range(ATT_NQ * GQA):
            cs = slice(g * qb, (g + 1) * qb)
            sc = s_s[:, cs]
            mg = jnp.maximum(m[:, cs], jnp.max(sc, axis=0, keepdims=True))
            alpha.append(jnp.exp(m[:, cs] - mg))
            p_s[:, cs] = jnp.exp(sc - mg).astype(BF16)
            m_new.append(mg)
        return jnp.concatenate(m_new, axis=1), jnp.concatenate(alpha, axis=1)

    def stage_pv(j, acc):
        k0 = pl.multiple_of(jnp.maximum(j, 0) * SEL_TK, SEL_TK)
        return acc + _dot(vs_ref[0, :V_ROWS, pl.ds(k0, SEL_TK)], p_s[...])

    def advance(j, m, acc):
        acc = stage_pv(j - 2, acc)
        m, alpha = stage_softmax(m)
        return m, alpha * acc

    def body(j, carry):
        m, acc = advance(j, *carry)
        stage_qk(j)
        return m, acc

    p_s[...] = jnp.zeros(p_s.shape, BF16)
    stage_qk(jnp.int32(0))
    init = (jnp.full((1, nq), NEG, F32), jnp.zeros((V_ROWS, nq), F32))
    m, acc = lax.fori_loop(1, n_tiles, body, init)
    _, acc = advance(n_tiles, m, acc)
    acc_s = stage_pv(n_tiles - 1, acc)

    wk = WINDOW + qb
    rowi = lax.broadcasted_iota(jnp.int32, (LANES, nq), 0)
    q_win = jnp.where(rowi < DK, qt, jnp.zeros_like(qt))
    acc_w = []
    for b in range(ATT_NQ):
        cs = slice(b * hq, (b + 1) * hq)
        w0 = pl.multiple_of(jnp.maximum((i + b) * qb - WINDOW, 0), qb)
        sw = _dot(kw_ref[pl.ds(w0, wk), :], q_win[:, cs]) + bw_ref[0, 0, :, cs]
        pw = jnp.exp(sw - jnp.max(sw, axis=0, keepdims=True)).astype(BF16)
        acc_w.append(_dot(vw_ref[0, :V_ROWS, pl.ds(w0, wk)], pw))
    acc_w = jnp.concatenate(acc_w, axis=1)

    pad_rows = jnp.zeros((LANES - DK, qb), F32)
    for b in range(ATT_NQ):
        rows = slice(b * qb, (b + 1) * qb)
        gt_t = _sigmoid(gl_ref[rows, :] + gb_ref[...]).T
        outs = []
        for g in range(GQA):
            cs = slice(b * hq + g * qb, b * hq + (g + 1) * qb)
            o_s = acc_s[:DK, cs] * (gt_t[GQA + g:GQA + g + 1, :] / acc_s[DK:DK + 1, cs])
            o_w = acc_w[:DK, cs] * (gt_t[2 * GQA + g:2 * GQA + g + 1, :] / acc_w[DK:DK + 1, cs])
            outs.append(jnp.concatenate([o_s + o_w, pad_rows], axis=0).T[:, :DK])
        o = ocg_ref[rows, :] + jnp.concatenate(outs, axis=1)
        o_ref[rows, :] = (o * _silu(gc_ref[rows, :])).astype(o_ref.dtype)


def _rel_bucket(dist):
    n = jnp.maximum(dist, 0)
    exact = REL_BUCKETS // 2
    nf = jnp.maximum(n, 1).astype(jnp.float32)
    large = exact + (jnp.log(nf / exact) / math.log(REL_MAX_DIST / exact)
                     * (REL_BUCKETS - exact)).astype(jnp.int32)
    return jnp.where(n < exact, n, jnp.minimum(large, REL_BUCKETS - 1))


def nsa_bias_tables(rel_bias, seq):
    qb = Q_BLOCK
    n_pad = seq // CMP_STRIDE
    c0 = n_pad - NA
    wc = 2 * n_pad
    ws = SEL_FAR * qb + SEL_TK
    nmax = max(seq, SEL_FAR * qb + qb)
    padl = CMP_STRIDE * (wc + 2)
    bvec = rel_bias[_rel_bucket(jnp.arange(nmax))].T.astype(F32)
    vext = jnp.concatenate([jnp.full((NSA_HEADS, padl), NEG, F32), bvec], axis=1)
    pos = jnp.arange(padl + nmax) - padl
    vwin = jnp.where(pos < WINDOW, vext, NEG)

    mlo, mhi = c0 - wc + 1, NA - 1 + c0
    start = padl + CMP_STRIDE * mlo - (CMP_BLOCK - 1)
    assert start >= 0
    u = vext[:, start:start + CMP_STRIDE * (mhi - mlo + 1)].reshape(NSA_HEADS, mhi - mlo + 1, CMP_STRIDE)
    urev = u[:, ::-1, :]
    gcb = jnp.stack([urev[:, NA - 1 - a:NA - 1 - a + wc, :] for a in range(NA)], axis=1)
    gcb = gcb.transpose(0, 1, 3, 2).reshape(HKV, GQA * qb, wc)
    nqb = seq // qb
    bc = jnp.stack([gcb[:, :, c0 - NA * i:c0 - NA * i + n_pad] for i in range(nqb)], axis=1)

    def toeplitz(vec, d0, width):
        vrev = vec[:, ::-1]
        s0 = nmax - d0 - qb
        period = width + qb - 1
        assert s0 >= 0 and s0 + period <= padl + nmax
        sl = vrev[:, s0:s0 + period]
        y = jnp.concatenate([sl[:, qb - 1:], sl[:, :qb - 1]], axis=1)
        z = jnp.tile(y, (1, qb))[:, :qb * (period - 1)].reshape(NSA_HEADS, qb, period - 1)
        return z[:, :, :width].reshape(HKV, GQA * qb, width)

    gs = toeplitz(vext, SEL_FAR * qb, ws)
    bw = tuple(tuple(toeplitz(vwin, d0, WINDOW + qb) for d0 in starts)
               for starts in ([min(b * qb, WINDOW) for b in range(ATT_NQ)], [WINDOW] * ATT_NQ))
    return bc, gs, bw


def _overlap_matrix_t(n_pad):
    c_start = np.arange(n_pad)[None, :] * CMP_STRIDE
    s_start = np.arange(MASK_ROWS)[:, None] * SLC_BLOCK
    ov = np.clip(np.minimum(c_start + CMP_BLOCK, s_start + SLC_BLOCK) - np.maximum(c_start, s_start), 0, None)
    return ov.astype(np.float32) / CMP_BLOCK


def nsa_branch(h1, h2, kv, hgl, gate_b4, kct, vc, tables, bsz, seq):
    qb = Q_BLOCK
    nqb = seq // qb
    n_pad = seq // CMP_STRIDE
    n_slc = seq // SLC_BLOCK
    assert n_slc <= MASK_ROWS and seq % SEL_TK == 0 and nqb % SEL_SUB == 0
    assert (SEL_TK // qb) % ATT_NQ == 0 and nqb % ATT_NQ == 0
    bc, gs, bw = tables
    m = bsz * seq
    gw = GQA * DK

    sq = SEL_SUB * qb
    nsq = seq // sq
    ovt = jnp.asarray(_overlap_matrix_t(n_pad), BF16)
    qsel, ocg = pl.pallas_call(
        _nsa_select_kernel,
        grid=(HKV, bsz, nsq),
        in_specs=[
            pl.BlockSpec((sq, gw), lambda hh, b, i: (b * nsq + i, C1_QC // gw + hh)),
            pl.BlockSpec((sq, LANES), lambda hh, b, i: (b * nsq + i, hh)),
            pl.BlockSpec((1, LANES), lambda hh, b, i: (0, hh)),
            pl.BlockSpec((1, DK, n_pad), lambda hh, b, i: (b * HKV + hh, 0, 0)),
            pl.BlockSpec((1, n_pad, DK), lambda hh, b, i: (b * HKV + hh, 0, 0)),
            pl.BlockSpec((1, SEL_SUB) + bc.shape[2:], lambda hh, b, i: (hh, i, 0, 0)),
            pl.BlockSpec(ovt.shape, lambda hh, b, i: (0, 0)),
        ],
        out_specs=[pl.BlockSpec((1, SEL_SUB, LANES, GQA * qb), lambda hh, b, i: (hh, b * nsq + i, 0, 0)),
                   pl.BlockSpec((sq, gw), lambda hh, b, i: (b * nsq + i, hh))],
        out_shape=[jax.ShapeDtypeStruct((HKV, bsz * nqb, LANES, GQA * qb), BF16),
                   jax.ShapeDtypeStruct((m, D_BR), F32)],
        scratch_shapes=[pltpu.VMEM((SEL_SUB, MASK_ROWS, qb), F32)],
        compiler_params=_params(),
        name="nsa_select",
    )(h1, hgl, gate_b4, kct, vc, bc, ovt)

    gs_t = gs.transpose(0, 2, 1)
    bw_t = jnp.stack([jnp.concatenate([w.transpose(0, 2, 1) for w in variant], axis=2) for variant in bw],
                     axis=1)
    ks_ext, kw_ext, vs_ext, vw_ext = kv

    aq = ATT_NQ * qb
    npair = nqb // ATT_NQ
    rowblk = lambda width, c0: pl.BlockSpec((aq, width), lambda hh, b, i: (b * npair + i, c0 // width + hh))
    keys = pl.BlockSpec((seq, LANES), lambda hh, b, i: (b, hh))
    vals = pl.BlockSpec((1, LANES, seq), lambda hh, b, i: (hh, 0, b))
    return pl.pallas_call(
        _nsa_attn_kernel,
        grid=(HKV, bsz, npair),
        in_specs=[
            pl.BlockSpec((1, ATT_NQ, LANES, GQA * qb), lambda hh, b, i: (hh, b * npair + i, 0, 0)),
            rowblk(gw, 0), rowblk(gw, C2_GC),
            pl.BlockSpec((aq, LANES), lambda hh, b, i: (b * npair + i, hh)),
            pl.BlockSpec((1, LANES), lambda hh, b, i: (0, hh)),
            keys, vals, pl.BlockSpec((1,) + gs_t.shape[1:], lambda hh, b, i: (hh, 0, 0)),
            keys, vals, pl.BlockSpec((1, 1) + bw_t.shape[2:], lambda hh, b, i: (hh, jnp.minimum(i, 1), 0, 0)),
        ],
        out_specs=pl.BlockSpec((aq, gw), lambda hh, b, i: (b * npair + i, hh)),
        out_shape=jax.ShapeDtypeStruct((m, D_BR), BF16),
        scratch_shapes=[pltpu.VMEM((SEL_TK, ATT_NQ * GQA * qb), F32), pltpu.VMEM((SEL_TK, ATT_NQ * GQA * qb), BF16)],
        compiler_params=_params(),
        name="nsa_attn",
    )(qsel, ocg, h2, hgl, gate_b4, ks_ext, vs_ext, gs_t, kw_ext, vw_ext, bw_t)


def _final_kernel(oa_ref, ob_ref, oc_ref, om_ref, *rest):
    gate_refs = rest[:2 * N_BRANCHES]
    x_ref, wb_ref, wo_ref, lng_ref, lnb_ref, y_ref, yb_ref = rest[2 * N_BRANCHES:]
    o_refs = (oa_ref, ob_ref, oc_ref, om_ref)
    halves = []
    for c in range(2):
        cs = slice(c * D_BR, (c + 1) * D_BR)
        acc = None
        for k in range(N_BRANCHES):
            term = _sigmoid(gate_refs[2 * k + c][...]) * _dot(o_refs[k][...], wb_ref[k, :, cs])
            acc = term if acc is None else acc + term
        halves.append(acc.astype(BF16))
    merged = jnp.concatenate(halves, axis=1)
    z = DN_ALPHA * x_ref[...] + _dot(merged, wo_ref[...])
    mu = jnp.mean(z, axis=-1, keepdims=True)
    zc = z - mu
    var = jnp.mean(zc * zc, axis=-1, keepdims=True)
    y = zc * lax.rsqrt(var + LN_EPS) * lng_ref[...] + lnb_ref[...]
    y_ref[...] = y
    yb_ref[...] = y.astype(BF16)


def final_merge(o_a, o_b, o_c, o_m, h2, x, w_branch, w_out, ln_g, ln_b, tm=128):
    m = x.shape[0]
    tm = min(tm, m)
    br = pl.BlockSpec((tm, D_BR), lambda i: (i, 0))
    gate = lambda k: pl.BlockSpec((tm, D_BR), lambda i, k=k: (i, C2_MERGE // D_BR + k))
    xs = pl.BlockSpec((tm, D_MODEL), lambda i: (i, 0))
    resident = lambda a: pl.BlockSpec(a.shape, lambda i: (0,) * a.ndim, pipeline_mode=pl.Buffered(1))
    wb = w_branch.astype(BF16)
    wo = w_out.astype(BF16)
    lg, lb = ln_g.reshape(1, D_MODEL), ln_b.reshape(1, D_MODEL)
    ngate = 2 * N_BRANCHES
    return pl.pallas_call(
        _final_kernel,
        grid=(m // tm,),
        in_specs=[br, br, br, br] + [gate(k) for k in range(ngate)] + [xs]
                 + [resident(wb), resident(wo), resident(lg), resident(lb)],
        out_specs=[xs, xs],
        out_shape=[jax.ShapeDtypeStruct((m, D_MODEL), F32), jax.ShapeDtypeStruct((m, D_MODEL), BF16)],
        compiler_params=_params(),
        name="merge_out_ln",
    )(o_a, o_b, o_c, o_m, *([h2] * ngate), x, wb, wo, lg, lb)


def _gate_spread_matrix():
    p = np.zeros((3 * NSA_HEADS, HKV * LANES), np.float32)
    for hh in range(HKV):
        for brn in range(3):
            for g in range(GQA):
                p[brn * NSA_HEADS + hh * GQA + g, hh * LANES + brn * GQA + g] = 1.0
    return p


def _spread_gate_cols(a):
    return jnp.dot(a, jnp.asarray(_gate_spread_matrix()), precision=lax.Precision.HIGHEST)


def layer(l, x, xb, mem_b, tables, bsz, seq, w_in, sgu_ln_g, sgu_ln_b, sgu_w, sgu_b, conv_w, conv_b,
          lru_wa, lru_ba, lru_wx, lru_bx, lru_lambda, cmp_pe_k, cmp_pe_v, cmp_w1_k, cmp_w1_v,
          cmp_w2_k, cmp_w2_v, nsa_gate_b, w_mem_kv, w_branch, w_out, ln_g, ln_b):
    m = bsz * seq
    tm = min(1024, m)
    tm_big = min(2048, m)
    h1 = matmul_f32w(xb, w_in, l, 0, N1, F32, tm_big, 512, w_is_nk=True)
    kv = kv_projection(xb, w_in, l, N1, seq, tm)
    h2 = matmul_f32w(xb, w_in, l, GL_OFF + 3 * NSA_HEADS, N2, F32, tm, 1024, w_is_nk=True,
                     w_single_buffer=True)
    w_gl = _spread_gate_cols(w_in[l, GL_OFF:GL_OFF + 3 * NSA_HEADS, :].T).astype(BF16)
    hgl = matmul(xb, w_gl, F32, tm, HKV * LANES)
    gate_b4 = _spread_gate_cols(nsa_gate_b[l].reshape(1, 3 * NSA_HEADS))

    o_a = gmlp_branch(h1, sgu_ln_g[l], sgu_ln_b[l], sgu_w[l], sgu_b[l])
    o_b = lru_branch(h1, bsz, seq, conv_w[l], conv_b[l], lru_wa[l], lru_ba[l], lru_wx[l], lru_bx[l],
                     lru_lambda[l])

    kct, vc = nsa_compress(h1, bsz, seq, cmp_pe_k[l], cmp_pe_v[l],
                           cmp_w1_k[l], cmp_w1_v[l], cmp_w2_k[l], cmp_w2_v[l])
    o_c = nsa_branch(h1, h2, kv, hgl, gate_b4, kct, vc, tables, bsz, seq)

    mrows = mem_b.shape[0]
    kv = matmul_f32w(mem_b, w_mem_kv, l, 0, 2 * D_BR, BF16, min(512, mrows), 512)
    o_m = mem_branch(h2, kv, bsz, seq)
    return final_merge(o_a, o_b, o_c, o_m, h2, x, w_branch[l], w_out[l], ln_g[l], ln_b[l])


def kernel(x, mem, rel_bias, w_in, sgu_ln_g, sgu_ln_b, sgu_w, sgu_b, conv_w, conv_b, lru_wa, lru_ba, lru_wx,
           lru_bx, lru_lambda, cmp_pe_k, cmp_pe_v, cmp_w1_k, cmp_w1_v, cmp_w2_k, cmp_w2_v, nsa_gate_b,
           w_mem_kv, w_branch, w_out, ln_g, ln_b):
    bsz, seq, _ = x.shape
    tables = nsa_bias_tables(rel_bias, seq)
    xf = x.reshape(bsz * seq, D_MODEL)
    xb = xf.astype(BF16)
    mem_b = mem.reshape(-1, D_MODEL).astype(BF16)
    w_in_t = jnp.swapaxes(w_in, 1, 2)
    params = (w_in_t, sgu_ln_g, sgu_ln_b, sgu_w, sgu_b, conv_w, conv_b, lru_wa, lru_ba, lru_wx, lru_bx,
              lru_lambda, cmp_pe_k, cmp_pe_v, cmp_w1_k, cmp_w1_v, cmp_w2_k, cmp_w2_v, nsa_gate_b,
              w_mem_kv, w_branch, w_out, ln_g, ln_b)
    for l in range(w_in.shape[0]):
        xf, xb = layer(l, xf, xb, mem_b, tables, bsz, seq, *params)
    return xf.reshape(bsz, seq, D_MODEL)
```

```python
import functools
import math

import numpy as np
import jax
import jax.numpy as jnp
from jax import lax
from jax.experimental import pallas as pl
from jax.experimental.pallas import tpu as pltpu

F32 = jnp.float32
BF16 = jnp.bfloat16

D_MODEL = 2048
DEPTH = 2
D_BR = D_MODEL // 2
N_BRANCHES = 4
GMLP_CHUNK = 128
GMLP_GROUPS = 8
LRU_BLOCKS = 8
LRU_BLOCK_DIM = D_BR // LRU_BLOCKS
CONV_WIDTH = 4
LRU_C = 8.0
DK = 64
NSA_HEADS = D_BR // DK
HKV = NSA_HEADS // 4
GQA = NSA_HEADS // HKV
KV_W = HKV * DK
CMP_BLOCK = 32
CMP_STRIDE = 16
CMP_HIDDEN = 256
SLC_BLOCK = 64
SLC_TOPK = 8
WINDOW = 256
Q_BLOCK = 128
MEM_HEADS = 4
MEM_HEAD_DIM = D_BR // MEM_HEADS
REL_BUCKETS = 32
REL_MAX_DIST = 1024
DN_ALPHA = (2 * DEPTH) ** 0.25
LN_EPS = 1e-5

LANES = 128
SUBLANES = 8
NEG = -1e30
SEL_TK = 512
SEL_FAR = 11
MASK_ROWS = LANES - DK
NA = Q_BLOCK // CMP_STRIDE
SEL_SUB = 4
ATT_NQ = 2
V_ROWS = DK + 16
VMEM_LIMIT = 56 * 1024 * 1024

C1_U, C1_V, C1_GA, C1_XB, C1_GB, C1_QC = (D_BR * k for k in range(6))
C1_KV = 6 * D_BR
N1 = C1_KV + 2 * KV_W
NKV = 4 * KV_W
C2_GC, C2_QM, C2_GM, C2_MERGE = (D_BR * k for k in range(4))
N2 = C2_MERGE + N_BRANCHES * D_MODEL
GL_OFF = N1 + NKV


def _sigmoid(x):
    return 1.0 / (1.0 + jnp.exp(-x))


def _silu(x):
    return x * _sigmoid(x)


def _gelu_tanh(x):
    return 0.5 * x * (1.0 + jnp.tanh(math.sqrt(2.0 / math.pi) * (x + 0.044715 * (x * x * x))))


def _dot(a, b):
    return jnp.dot(a, b, preferred_element_type=F32)


def _dot_nt(a, b):
    return lax.dot_general(a, b, (((1,), (1,)), ((), ())), preferred_element_type=F32)


def _params(**kw):
    return pltpu.CompilerParams(vmem_limit_bytes=VMEM_LIMIT, **kw)


def _mm_kernel(x_ref, w_ref, o_ref):
    o_ref[...] = _dot(x_ref[...], w_ref[...]).astype(o_ref.dtype)


def matmul(x, w, out_dtype, tm, tn):
    m, k = x.shape
    n = w.shape[1]
    assert m % tm == 0 and n % tn == 0, (m, n, tm, tn)
    return pl.pallas_call(
        _mm_kernel,
        grid=(n // tn, m // tm),
        in_specs=[pl.BlockSpec((tm, k), lambda j, i: (i, 0)),
                  pl.BlockSpec((k, tn), lambda j, i: (0, j))],
        out_specs=pl.BlockSpec((tm, tn), lambda j, i: (i, j)),
        out_shape=jax.ShapeDtypeStruct((m, n), out_dtype),
        compiler_params=_params(),
        name="matmul",
    )(x, w)


def _mm_castw_kernel(x_ref, *rest, w_is_nk, shift):
    w_refs, (o_ref, wb_ref) = rest[:-2], rest[-2:]

    @pl.when(pl.program_id(1) == 0)
    def _():
        if shift:
            w = jnp.concatenate([w_refs[0][0, shift:, :], w_refs[1][0, :shift, :]], axis=0)
        else:
            w = w_refs[0][0]
        wb_ref[...] = (w.T if w_is_nk else w).astype(BF16)

    o_ref[...] = _dot(x_ref[...], wb_ref[...]).astype(o_ref.dtype)


def matmul_f32w(x, w3, layer_idx, col0, n, out_dtype, tm, tn, w_is_nk=False, w_single_buffer=False):
    m, k = x.shape
    ncols = w3.shape[1] if w_is_nk else w3.shape[2]
    shift = col0 % tn
    assert m % tm == 0 and n % tn == 0 and col0 + n <= ncols, (m, n, col0, tm, tn)
    assert shift == 0 or (w_is_nk and shift % SUBLANES == 0), (col0, tn)
    j0 = col0 // tn
    mode = dict(pipeline_mode=pl.Buffered(1)) if w_single_buffer else {}
    if w_is_nk:
        w_specs = [pl.BlockSpec((1, tn, k), lambda j, i, d=d: (layer_idx, j0 + j + d, 0), **mode)
                   for d in range(2 if shift else 1)]
    else:
        w_specs = [pl.BlockSpec((1, k, tn), lambda j, i: (layer_idx, 0, j0 + j), **mode)]
    return pl.pallas_call(
        functools.partial(_mm_castw_kernel, w_is_nk=w_is_nk, shift=shift),
        grid=(n // tn, m // tm),
        in_specs=[pl.BlockSpec((tm, k), lambda j, i: (i, 0))] + w_specs,
        out_specs=pl.BlockSpec((tm, tn), lambda j, i: (i, j)),
        out_shape=jax.ShapeDtypeStruct((m, n), out_dtype),
        scratch_shapes=[pltpu.VMEM((k, tn), BF16)],
        compiler_params=_params(dimension_semantics=("arbitrary", "arbitrary")),
        name="matmul_f32w",
    )(x, *([w3] * len(w_specs)))


def _kv_proj_kernel(x_ref, ws_ref, ww_ref, ks_ref, kw_ref, vs_ref, vw_ref, wk_s, wv_s, *, tm, seq):
    @pl.when(pl.program_id(0) == 0)
    def _():
        wk = jnp.concatenate([ws_ref[0, 0:KV_W], ww_ref[0, 0:KV_W]], axis=0)
        wk_s[...] = wk.T.astype(BF16)
        wv_s[...] = jnp.concatenate([ws_ref[0, KV_W:2 * KV_W], ww_ref[0, KV_W:2 * KV_W]], axis=0).astype(BF16)

    x = x_ref[...]
    kk = _dot(x, wk_s[...])
    vt = _dot_nt(wv_s[...], x)
    pos = (pl.program_id(0) * tm + lax.broadcasted_iota(jnp.int32, (tm, MASK_ROWS), 0)) % seq
    blk = lax.broadcasted_iota(jnp.int32, (tm, MASK_ROWS), 1)
    onehot = jnp.where((pos >> int(math.log2(SLC_BLOCK))) == blk, 1.0, 0.0)
    zeros = jnp.zeros((tm, MASK_ROWS), F32)
    tail = jnp.where(lax.broadcasted_iota(jnp.int32, (LANES - DK, tm), 0) == 0, 1.0, 0.0)
    for h in range(HKV):
        ks_ref[:, h * LANES:(h + 1) * LANES] = jnp.concatenate(
            [kk[:, h * DK:(h + 1) * DK], onehot], axis=1).astype(ks_ref.dtype)
        kw_ref[:, h * LANES:(h + 1) * LANES] = jnp.concatenate(
            [kk[:, KV_W + h * DK:KV_W + (h + 1) * DK], zeros], axis=1).astype(kw_ref.dtype)
        vs_ref[h] = jnp.concatenate([vt[h * DK:(h + 1) * DK], tail], axis=0).astype(vs_ref.dtype)
        vw_ref[h] = jnp.concatenate([vt[KV_W + h * DK:KV_W + (h + 1) * DK], tail], axis=0).astype(vw_ref.dtype)


def kv_projection(x, wt3, layer_idx, row0, seq, tm):
    m, k = x.shape
    nrows = 2 * KV_W
    assert m % tm == 0 and row0 % nrows == 0 and seq % tm == 0
    wspec = lambda d: pl.BlockSpec((1, nrows, k), lambda i: (layer_idx, row0 // nrows + d, 0),
                                   pipeline_mode=pl.Buffered(1))
    keys = jax.ShapeDtypeStruct((m, HKV * LANES), BF16)
    vals = jax.ShapeDtypeStruct((HKV, LANES, m), BF16)
    return pl.pallas_call(
        functools.partial(_kv_proj_kernel, tm=tm, seq=seq),
        grid=(m // tm,),
        in_specs=[pl.BlockSpec((tm, k), lambda i: (i, 0)), wspec(0), wspec(1)],
        out_specs=[pl.BlockSpec((tm, HKV * LANES), lambda i: (i, 0))] * 2
                  + [pl.BlockSpec((HKV, LANES, tm), lambda i: (0, 0, i))] * 2,
        out_shape=[keys, keys, vals, vals],
        scratch_shapes=[pltpu.VMEM((k, 2 * KV_W), BF16), pltpu.VMEM((2 * KV_W, k), BF16)],
        compiler_params=_params(dimension_semantics=("arbitrary",)),
        name="kv_projection",
    )(x, wt3, wt3)


def _gmlp_kernel(u_ref, v_ref, ga_ref, lng_ref, lnb_ref, w_ref, bs_ref, o_ref, *, rows):
    gd = D_BR // GMLP_GROUPS
    for c in range(rows // GMLP_CHUNK):
        r = slice(c * GMLP_CHUNK, (c + 1) * GMLP_CHUNK)
        v = _gelu_tanh(v_ref[r, :])
        mu = jnp.mean(v, axis=-1, keepdims=True)
        vc = v - mu
        var = jnp.mean(vc * vc, axis=-1, keepdims=True)
        vb = (vc * lax.rsqrt(var + LN_EPS) * lng_ref[...] + lnb_ref[...]).astype(BF16)
        u = _gelu_tanh(u_ref[r, :]) * _silu(ga_ref[r, :])
        for g in range(GMLP_GROUPS):
            cs = slice(g * gd, (g + 1) * gd)
            mixed = _dot(w_ref[g], vb[:, cs]) + bs_ref[:, g:g + 1]
            o_ref[r, cs] = (u[:, cs] * mixed).astype(o_ref.dtype)


def gmlp_branch(h1, ln_g, ln_b, w_s, b_s, rows=1024):
    m = h1.shape[0]
    rows = min(rows, m)
    causal = jnp.tril(jnp.ones((GMLP_CHUNK, GMLP_CHUNK), dtype=bool))
    w = jnp.where(causal, w_s, 0).astype(BF16)
    col = lambda c: pl.BlockSpec((rows, D_BR), lambda i, c=c: (i, c // D_BR))
    full = lambda a: pl.BlockSpec(a.shape, lambda i: (0,) * a.ndim)
    args = (ln_g.reshape(1, D_BR), ln_b.reshape(1, D_BR), w, b_s.T)
    return pl.pallas_call(
        functools.partial(_gmlp_kernel, rows=rows),
        grid=(m // rows,),
        in_specs=[col(C1_U), col(C1_V), col(C1_GA)] + [full(a) for a in args],
        out_specs=pl.BlockSpec((rows, D_BR), lambda i: (i, 0)),
        out_shape=jax.ShapeDtypeStruct((m, D_BR), BF16),
        compiler_params=_params(),
        name="gmlp",
    )(h1, h1, h1, *args)


def _lru_kernel(xb_ref, gb_ref, cw_ref, cb_ref, wa_ref, ba_ref, wx_ref, bx_ref, lam_ref, o_ref,
                xbuf, hcarry, a_s, g_s, *, ts):
    @pl.when(pl.program_id(1) == 0)
    def _():
        xbuf[0:8, :] = jnp.zeros((8, D_BR), F32)
        hcarry[...] = jnp.zeros((8, D_BR), F32)

    xbuf[8:8 + ts, :] = xb_ref[...]
    xc = cb_ref[...] + cw_ref[0:1, :] * xbuf[pl.ds(8 - (CONV_WIDTH - 1), ts), :]
    for k in range(1, CONV_WIDTH):
        xc = xc + cw_ref[k:k + 1, :] * xbuf[pl.ds(8 - (CONV_WIDTH - 1) + k, ts), :]
    xbuf[0:8, :] = xbuf[ts:ts + 8, :]

    xcb = xc.astype(BF16)
    bd = LRU_BLOCK_DIM
    r = jnp.concatenate([_dot(xcb[:, n * bd:(n + 1) * bd], wa_ref[n]) for n in range(LRU_BLOCKS)], axis=1)
    i = jnp.concatenate([_dot(xcb[:, n * bd:(n + 1) * bd], wx_ref[n]) for n in range(LRU_BLOCKS)], axis=1)
    r = _sigmoid(r + ba_ref[...])
    i = _sigmoid(i + bx_ref[...])
    nl = -lam_ref[...]
    softplus = jnp.maximum(nl, 0.0) + jnp.log1p(jnp.exp(-jnp.abs(nl)))
    log_a = (-LRU_C * softplus) * r
    a = jnp.exp(log_a)
    a_s[...] = a
    g_s[...] = jnp.sqrt(1.0 - a * a) * i * xc

    row = lax.broadcasted_iota(jnp.int32, (8, D_BR), 0)

    def body(j, carry):
        r0 = pl.multiple_of(j * 8, 8)
        av = a_s[pl.ds(r0, 8), :]
        bv = g_s[pl.ds(r0, 8), :]
        for d in (1, 2, 4):
            keep = row >= d
            a_sh = pltpu.roll(av, d, axis=0)
            b_sh = pltpu.roll(bv, d, axis=0)
            bv = jnp.where(keep, av * b_sh + bv, bv)
            av = jnp.where(keep, av * a_sh, av)
        hv = av * carry + bv
        g_s[pl.ds(r0, 8), :] = hv
        return jnp.broadcast_to(hv[7:8, :], (8, D_BR))

    hcarry[...] = lax.fori_loop(0, ts // 8, body, hcarry[...])
    o_ref[...] = (g_s[...] * _silu(gb_ref[...])).astype(o_ref.dtype)


def lru_branch(h1, bsz, seq, conv_w, conv_b, wa, ba, wx, bx, lam, ts=512):
    ts = min(ts, seq)
    ns = seq // ts
    col = lambda c: pl.BlockSpec((ts, D_BR), lambda b, s, c=c: (b * ns + s, c // D_BR))
    full = lambda a: pl.BlockSpec(a.shape, lambda b, s: (0,) * a.ndim)
    row = lambda a: a.reshape(1, D_BR)
    args = (conv_w, row(conv_b), wa.astype(BF16), row(ba), wx.astype(BF16), row(bx), row(lam))
    return pl.pallas_call(
        functools.partial(_lru_kernel, ts=ts),
        grid=(bsz, ns),
        in_specs=[col(C1_XB), col(C1_GB)] + [full(a) for a in args],
        out_specs=pl.BlockSpec((ts, D_BR), lambda b, s: (b * ns + s, 0)),
        out_shape=jax.ShapeDtypeStruct((bsz * seq, D_BR), BF16),
        scratch_shapes=[pltpu.VMEM((ts + 8, D_BR), F32), pltpu.VMEM((8, D_BR), F32),
                        pltpu.VMEM((ts, D_BR), F32), pltpu.VMEM((ts, D_BR), F32)],
        compiler_params=_params(dimension_semantics=("arbitrary", "arbitrary")),
        name="lru",
    )(h1, h1, *args)


def _mem_kernel(q_ref, g_ref, kv_ref, o_ref):
    hd = MEM_HEAD_DIM
    for hh in range(MEM_HEADS):
        cs = slice(hh * hd, (hh + 1) * hd)
        q = (q_ref[:, cs] * (hd ** -0.5)).astype(BF16)
        s = _dot_nt(q, kv_ref[:, cs])
        p = jnp.exp(s - jnp.max(s, axis=-1, keepdims=True))
        l = jnp.sum(p, axis=-1, keepdims=True)
        o = _dot(p.astype(BF16), kv_ref[:, D_BR + hh * hd:D_BR + (hh + 1) * hd]) / l
        o_ref[:, cs] = (o * _silu(g_ref[:, cs])).astype(o_ref.dtype)


def mem_branch(h2, kv, bsz, seq, tq=1024):
    tq = min(tq, seq)
    nq = seq // tq
    mlen = kv.shape[0] // bsz
    col = lambda c: pl.BlockSpec((tq, D_BR), lambda b, i, c=c: (b * nq + i, c // D_BR))
    return pl.pallas_call(
        _mem_kernel,
        grid=(bsz, nq),
        in_specs=[col(C2_QM), col(C2_GM), pl.BlockSpec((mlen, 2 * D_BR), lambda b, i: (b, 0))],
        out_specs=pl.BlockSpec((tq, D_BR), lambda b, i: (b * nq + i, 0)),
        out_shape=jax.ShapeDtypeStruct((bsz * seq, D_BR), BF16),
        compiler_params=_params(),
        name="mem_attn",
    )(h2, h2, kv)


def _cmp_kernel(*refs):
    ngrp = 2 * KV_W // LANES
    t_refs = refs[:ngrp]
    pek_ref, pev_ref, w1k_ref, w1v_ref, w2kt_ref, w2v_ref, kct_ref, vc_ref = refs[ngrp:]
    n = t_refs[0].shape[0] // CMP_STRIDE
    for h in range(HKV):
        for is_v, (pe_ref, w1_ref) in enumerate(((pek_ref, w1k_ref), (pev_ref, w1v_ref))):
            grp, off = divmod(is_v * KV_W + h * DK, LANES)
            lo = hi = None
            for tok in range(CMP_STRIDE):
                x = t_refs[grp][pl.ds(tok, n, stride=CMP_STRIDE), :][:, off:off + DK]
                t2 = CMP_STRIDE + tok
                a = _dot((x + pe_ref[tok:tok + 1, :]).astype(BF16), w1_ref[tok * DK:(tok + 1) * DK, :])
                b = _dot((x + pe_ref[t2:t2 + 1, :]).astype(BF16), w1_ref[t2 * DK:(t2 + 1) * DK, :])
                lo = a if lo is None else lo + a
                hi = b if hi is None else hi + b
            hidden = _silu(lo + pltpu.roll(hi, n - 1, axis=0)).astype(BF16)
            if is_v:
                vc_ref[h] = _dot(hidden, w2v_ref[...]).astype(vc_ref.dtype)
            else:
                kct_ref[h] = _dot_nt(w2kt_ref[...], hidden).astype(kct_ref.dtype)


def nsa_compress(h1, bsz, seq, pe_k, pe_v, w1_k, w1_v, w2_k, w2_v):
    n = seq // CMP_STRIDE
    args = (pe_k, pe_v, w1_k.astype(BF16), w1_v.astype(BF16), w2_k.T.astype(BF16), w2_v.astype(BF16))
    full = lambda a: pl.BlockSpec(a.shape, lambda b: (0,) * a.ndim)
    return pl.pallas_call(
        _cmp_kernel,
        grid=(bsz,),
        in_specs=[pl.BlockSpec((seq, LANES), lambda b, g=g: (b, C1_KV // LANES + g))
                  for g in range(2 * KV_W // LANES)] + [full(a) for a in args],
        out_specs=[pl.BlockSpec((HKV, DK, n), lambda b: (b, 0, 0)), pl.BlockSpec((HKV, n, DK), lambda b: (b, 0, 0))],
        out_shape=[jax.ShapeDtypeStruct((bsz * HKV, DK, n), BF16), jax.ShapeDtypeStruct((bsz * HKV, n, DK), BF16)],
        compiler_params=_params(),
        name="nsa_compress",
    )(*([h1] * (2 * KV_W // LANES)), *args)


def _nsa_select_kernel(q_ref, gl_ref, gb_ref, kct_ref, vc_ref, bc_ref, ovt_ref, qsel_ref, ocg_ref, vt_s):
    qb = Q_BLOCK
    lane = lax.broadcasted_iota(jnp.int32, (qb, LANES), 1)
    jrow = lax.broadcasted_iota(jnp.int32, (MASK_ROWS, qb), 0).astype(F32)
    qlane = lax.broadcasted_iota(jnp.int32, (MASK_ROWS, qb), 1)
    sub = lax.broadcasted_iota(jnp.int32, (SUBLANES, qb), 0)
    nv = MASK_ROWS // SUBLANES
    ovt = ovt_ref[...]
    kct = kct_ref[0]
    vc = vc_ref[0]
    for sb in range(SEL_SUB):
        rows = slice(sb * qb, (sb + 1) * qb)
        blk = pl.program_id(2) * SEL_SUB + sb
        q = q_ref[rows, :] * (DK ** -0.5)
        gt = _sigmoid(gl_ref[rows, :] + gb_ref[...])
        qpad = []
        for g in range(GQA):
            t = q[:, (g // 2) * LANES:(g // 2 + 1) * LANES]
            if g % 2:
                t = pltpu.roll(t, DK, axis=1)
            qpad.append(jnp.where(lane < DK, t, 0.0))

        has_keys = blk * qb + lax.broadcasted_iota(jnp.int32, (qb, 1), 0) >= CMP_BLOCK - 1
        o_c = []
        psum = None
        for g in range(GQA):
            s = _dot(qpad[g][:, :DK].astype(BF16), kct) + bc_ref[0, sb, g * qb:(g + 1) * qb, :]
            m = jnp.max(s, axis=-1, keepdims=True)
            p = jnp.exp(s - m)
            p = p * jnp.where(has_keys, 1.0 / jnp.maximum(jnp.sum(p, axis=-1, keepdims=True), 1e-30), 0.0)
            o_c.append(gt[:, g:g + 1] * _dot(p.astype(BF16), vc))
            psum = p if psum is None else psum + p
        ocg_ref[rows, :] = jnp.concatenate(o_c, axis=1)

        p_hi = psum.astype(BF16)
        rem = psum - p_hi.astype(F32)
        p_mid = rem.astype(BF16)
        p_lo = (rem - p_mid.astype(F32)).astype(BF16)
        imp_t = _dot_nt(ovt, p_hi) + _dot_nt(ovt, p_mid) + _dot_nt(ovt, p_lo)

        qblk = ((blk * qb + qlane) >> int(math.log2(SLC_BLOCK))).astype(F32)
        val = jnp.where(jrow == 0.0, 3e38,
                        jnp.where(jrow == qblk, 3e38, jnp.where(jrow == qblk - 1.0, 3e38, imp_t)))
        val = jnp.where(jrow > qblk, -1.0, val)
        vt_s[sb] = val
        vals = [val[k * SUBLANES:(k + 1) * SUBLANES] for k in range(nv)]
        cnt = [jnp.zeros((SUBLANES, qb), F32) for _ in range(nv)]
        for jp in range(MASK_ROWS):
            rowv = jnp.broadcast_to(vt_s[sb, jp:jp + 1, :], (SUBLANES, qb))
            for k in range(nv):
                if k * SUBLANES > jp:
                    beat = jnp.where(rowv >= vals[k], 1.0, 0.0)
                elif (k + 1) * SUBLANES - 1 < jp:
                    beat = jnp.where(rowv > vals[k], 1.0, 0.0)
                else:
                    beat = jnp.where(sub > jp - k * SUBLANES, jnp.where(rowv >= vals[k], 1.0, 0.0),
                                     jnp.where(rowv > vals[k], 1.0, 0.0))
                cnt[k] = cnt[k] + beat
        selneg_t = jnp.concatenate([jnp.where(c < float(SLC_TOPK), 0.0, NEG) for c in cnt], axis=0)
        for g in range(GQA):
            qt = jnp.concatenate([qpad[g].T[:DK], selneg_t], axis=0)
            qsel_ref[0, sb, :, g * qb:(g + 1) * qb] = qt.astype(qsel_ref.dtype)


def _nsa_attn_kernel(qsel_ref, ocg_ref, gc_ref, gl_ref, gb_ref, ks_ref, vs_ref, gs_ref, kw_ref, vw_ref, bw_ref,
                     o_ref, s_s, p_s):
    qb = Q_BLOCK
    hq = GQA * qb
    nq = ATT_NQ * hq
    i = pl.program_id(2) * ATT_NQ
    step = SEL_TK // qb
    n_tiles = (i * qb) // SEL_TK + 1
    qt = jnp.concatenate([qsel_ref[0, b] for b in range(ATT_NQ)], axis=1)

    def stage_qk(j):
        k0 = pl.multiple_of(j * SEL_TK, SEL_TK)
        sc = _dot(ks_ref[pl.ds(k0, SEL_TK), :], qt)
        for b in range(ATT_NQ):
            dd = jnp.minimum(i + b - j * step, SEL_FAR)
            boff = pl.multiple_of((SEL_FAR - dd) * qb, qb)
            s_s[:, b * hq:(b + 1) * hq] = sc[:, b * hq:(b + 1) * hq] + gs_ref[0, pl.ds(boff, SEL_TK), :]

    def stage_softmax(m):
        m_new, alpha = [], []
        for g in range(ATT_NQ * GQA):
            cs = slice(g * qb, (g + 1) * qb)
            sc = s_s[:, cs]
            mg = jnp.maximum(m[:, cs], jnp.max(sc, axis=0, keepdims=True))
            alpha.append(jnp.exp(m[:, cs] - mg))
            p_s[:, cs] = jnp.exp(sc - mg).astype(BF16)
            m_new.append(mg)
        return jnp.concatenate(m_new, axis=1), jnp.concatenate(alpha, axis=1)

    def stage_pv(j, acc):
        k0 = pl.multiple_of(jnp.maximum(j, 0) * SEL_TK, SEL_TK)
        return acc + _dot(vs_ref[0, :V_ROWS, pl.ds(k0, SEL_TK)], p_s[...])

    def advance(j, m, acc):
        acc = stage_pv(j - 2, acc)
        m, alpha = stage_softmax(m)
        return m, alpha * acc

    def body(j, carry):
        m, acc = advance(j, *carry)
        stage_qk(j)
        return m, acc

    p_s[...] = jnp.zeros(p_s.shape, BF16)
    stage_qk(jnp.int32(0))
    init = (jnp.full((1, nq), NEG, F32), jnp.zeros((V_ROWS, nq), F32))
    m, acc = lax.fori_loop(1, n_tiles, body, init)
    _, acc = advance(n_tiles, m, acc)
    acc_s = stage_pv(n_tiles - 1, acc)

    wk = WINDOW + qb
    rowi = lax.broadcasted_iota(jnp.int32, (LANES, nq), 0)
    q_win = jnp.where(rowi < DK, qt, jnp.zeros_like(qt))
    acc_w = []
    for b in range(ATT_NQ):
        cs = slice(b * hq, (b + 1) * hq)
        w0 = pl.multiple_of(jnp.maximum((i + b) * qb - WINDOW, 0), qb)
        sw = _dot(kw_ref[pl.ds(w0, wk), :], q_win[:, cs]) + bw_ref[0, 0, :, cs]
        pw = jnp.exp(sw - jnp.max(sw, axis=0, keepdims=True)).astype(BF16)
        acc_w.append(_dot(vw_ref[0, :V_ROWS, pl.ds(w0, wk)], pw))
    acc_w = jnp.concatenate(acc_w, axis=1)

    pad_rows = jnp.zeros((LANES - DK, qb), F32)
    for b in range(ATT_NQ):
        rows = slice(b * qb, (b + 1) * qb)
        gt_t = _sigmoid(gl_ref[rows, :] + gb_ref[...]).T
        outs = []
        for g in range(GQA):
            cs = slice(b * hq + g * qb, b * hq + (g + 1) * qb)
            o_s = acc_s[:DK, cs] * (gt_t[GQA + g:GQA + g + 1, :] / acc_s[DK:DK + 1, cs])
            o_w = acc_w[:DK, cs] * (gt_t[2 * GQA + g:2 * GQA + g + 1, :] / acc_w[DK:DK + 1, cs])
            outs.append(jnp.concatenate([o_s + o_w, pad_rows], axis=0).T[:, :DK])
        o = ocg_ref[rows, :] + jnp.concatenate(outs, axis=1)
        o_ref[rows, :] = (o * _silu(gc_ref[rows, :])).astype(o_ref.dtype)


def _rel_bucket(dist):
    n = jnp.maximum(dist, 0)
    exact = REL_BUCKETS // 2
    nf = jnp.maximum(n, 1).astype(jnp.float32)
    large = exact + (jnp.log(nf / exact) / math.log(REL_MAX_DIST / exact)
                     * (REL_BUCKETS - exact)).astype(jnp.int32)
    return jnp.where(n < exact, n, jnp.minimum(large, REL_BUCKETS - 1))


def nsa_bias_tables(rel_bias, seq):
    qb = Q_BLOCK
    n_pad = seq // CMP_STRIDE
    c0 = n_pad - NA
    wc = 2 * n_pad
    ws = SEL_FAR * qb + SEL_TK
    nmax = max(seq, SEL_FAR * qb + qb)
    padl = CMP_STRIDE * (wc + 2)
    bvec = rel_bias[_rel_bucket(jnp.arange(nmax))].T.astype(F32)
    vext = jnp.concatenate([jnp.full((NSA_HEADS, padl), NEG, F32), bvec], axis=1)
    pos = jnp.arange(padl + nmax) - padl
    vwin = jnp.where(pos < WINDOW, vext, NEG)

    mlo, mhi = c0 - wc + 1, NA - 1 + c0
    start = padl + CMP_STRIDE * mlo - (CMP_BLOCK - 1)
    assert start >= 0
    u = vext[:, start:start + CMP_STRIDE * (mhi - mlo + 1)].reshape(NSA_HEADS, mhi - mlo + 1, CMP_STRIDE)
    urev = u[:, ::-1, :]
    gcb = jnp.stack([urev[:, NA - 1 - a:NA - 1 - a + wc, :] for a in range(NA)], axis=1)
    gcb = gcb.transpose(0, 1, 3, 2).reshape(HKV, GQA * qb, wc)
    nqb = seq // qb
    bc = jnp.stack([gcb[:, :, c0 - NA * i:c0 - NA * i + n_pad] for i in range(nqb)], axis=1)

    def toeplitz(vec, d0, width):
        vrev = vec[:, ::-1]
        s0 = nmax - d0 - qb
        period = width + qb - 1
        assert s0 >= 0 and s0 + period <= padl + nmax
        sl = vrev[:, s0:s0 + period]
        y = jnp.concatenate([sl[:, qb - 1:], sl[:, :qb - 1]], axis=1)
        z = jnp.tile(y, (1, qb))[:, :qb * (period - 1)].reshape(NSA_HEADS, qb, period - 1)
        return z[:, :, :width].reshape(HKV, GQA * qb, width)

    gs = toeplitz(vext, SEL_FAR * qb, ws)
    bw = tuple(tuple(toeplitz(vwin, d0, WINDOW + qb) for d0 in starts)
               for starts in ([min(b * qb, WINDOW) for b in range(ATT_NQ)], [WINDOW] * ATT_NQ))
    return bc, gs, bw


def _overlap_matrix_t(n_pad):
    c_start = np.arange(n_pad)[None, :] * CMP_STRIDE
    s_start = np.arange(MASK_ROWS)[:, None] * SLC_BLOCK
    ov = np.clip(np.minimum(c_start + CMP_BLOCK, s_start + SLC_BLOCK) - np.maximum(c_start, s_start), 0, None)
    return ov.astype(np.float32) / CMP_BLOCK


def nsa_branch(h1, h2, kv, hgl, gate_b4, kct, vc, tables, bsz, seq):
    qb = Q_BLOCK
    nqb = seq // qb
    n_pad = seq // CMP_STRIDE
    n_slc = seq // SLC_BLOCK
    assert n_slc <= MASK_ROWS and seq % SEL_TK == 0 and nqb % SEL_SUB == 0
    assert (SEL_TK // qb) % ATT_NQ == 0 and nqb % ATT_NQ == 0
    bc, gs, bw = tables
    m = bsz * seq
    gw = GQA * DK

    sq = SEL_SUB * qb
    nsq = seq // sq
    ovt = jnp.asarray(_overlap_matrix_t(n_pad), BF16)
    qsel, ocg = pl.pallas_call(
        _nsa_select_kernel,
        grid=(HKV, bsz, nsq),
        in_specs=[
            pl.BlockSpec((sq, gw), lambda hh, b, i: (b * nsq + i, C1_QC // gw + hh)),
            pl.BlockSpec((sq, LANES), lambda hh, b, i: (b * nsq + i, hh)),
            pl.BlockSpec((1, LANES), lambda hh, b, i: (0, hh)),
            pl.BlockSpec((1, DK, n_pad), lambda hh, b, i: (b * HKV + hh, 0, 0)),
            pl.BlockSpec((1, n_pad, DK), lambda hh, b, i: (b * HKV + hh, 0, 0)),
            pl.BlockSpec((1, SEL_SUB) + bc.shape[2:], lambda hh, b, i: (hh, i, 0, 0)),
            pl.BlockSpec(ovt.shape, lambda hh, b, i: (0, 0)),
        ],
        out_specs=[pl.BlockSpec((1, SEL_SUB, LANES, GQA * qb), lambda hh, b, i: (hh, b * nsq + i, 0, 0)),
                   pl.BlockSpec((sq, gw), lambda hh, b, i: (b * nsq + i, hh))],
        out_shape=[jax.ShapeDtypeStruct((HKV, bsz * nqb, LANES, GQA * qb), BF16),
                   jax.ShapeDtypeStruct((m, D_BR), F32)],
        scratch_shapes=[pltpu.VMEM((SEL_SUB, MASK_ROWS, qb), F32)],
        compiler_params=_params(),
        name="nsa_select",
    )(h1, hgl, gate_b4, kct, vc, bc, ovt)

    gs_t = gs.transpose(0, 2, 1)
    bw_t = jnp.stack([jnp.concatenate([w.transpose(0, 2, 1) for w in variant], axis=2) for variant in bw],
                     axis=1)
    ks_ext, kw_ext, vs_ext, vw_ext = kv

    aq = ATT_NQ * qb
    npair = nqb // ATT_NQ
    rowblk = lambda width, c0: pl.BlockSpec((aq, width), lambda hh, b, i: (b * npair + i, c0 // width + hh))
    keys = pl.BlockSpec((seq, LANES), lambda hh, b, i: (b, hh))
    vals = pl.BlockSpec((1, LANES, seq), lambda hh, b, i: (hh, 0, b))
    return pl.pallas_call(
        _nsa_attn_kernel,
        grid=(HKV, bsz, npair),
        in_specs=[
            pl.BlockSpec((1, ATT_NQ, LANES, GQA * qb), lambda hh, b, i: (hh, b * npair + i, 0, 0)),
            rowblk(gw, 0), rowblk(gw, C2_GC),
            pl.BlockSpec((aq, LANES), lambda hh, b, i: (b * npair + i, hh)),
            pl.BlockSpec((1, LANES), lambda hh, b, i: (0, hh)),
            keys, vals, pl.BlockSpec((1,) + gs_t.shape[1:], lambda hh, b, i: (hh, 0, 0)),
            keys, vals, pl.BlockSpec((1, 1) + bw_t.shape[2:], lambda hh, b, i: (hh, jnp.minimum(i, 1), 0, 0)),
        ],
        out_specs=pl.BlockSpec((aq, gw), lambda hh, b, i: (b * npair + i, hh)),
        out_shape=jax.ShapeDtypeStruct((m, D_BR), BF16),
        scratch_shapes=[pltpu.VMEM((SEL_TK, ATT_NQ * GQA * qb), F32), pltpu.VMEM((SEL_TK, ATT_NQ * GQA * qb), BF16)],
        compiler_params=_params(),
        name="nsa_attn",
    )(qsel, ocg, h2, hgl, gate_b4, ks_ext, vs_ext, gs_t, kw_ext, vw_ext, bw_t)


def _final_kernel(oa_ref, ob_ref, oc_ref, om_ref, *rest):
    gate_refs = rest[:2 * N_BRANCHES]
    x_ref, wb_ref, wo_ref, lng_ref, lnb_ref, y_ref, yb_ref = rest[2 * N_BRANCHES:]
    o_refs = (oa_ref, ob_ref, oc_ref, om_ref)
    halves = []
    for c in range(2):
        cs = slice(c * D_BR, (c + 1) * D_BR)
        acc = None
        for k in range(N_BRANCHES):
            term = _sigmoid(gate_refs[2 * k + c][...]) * _dot(o_refs[k][...], wb_ref[k, :, cs])
            acc = term if acc is None else acc + term
        halves.append(acc.astype(BF16))
    merged = jnp.concatenate(halves, axis=1)
    z = DN_ALPHA * x_ref[...] + _dot(merged, wo_ref[...])
    mu = jnp.mean(z, axis=-1, keepdims=True)
    zc = z - mu
    var = jnp.mean(zc * zc, axis=-1, keepdims=True)
    y = zc * lax.rsqrt(var + LN_EPS) * lng_ref[...] + lnb_ref[...]
    y_ref[...] = y
    yb_ref[...] = y.astype(BF16)


def final_merge(o_a, o_b, o_c, o_m, h2, x, w_branch, w_out, ln_g, ln_b, tm=128):
    m = x.shape[0]
    tm = min(tm, m)
    br = pl.BlockSpec((tm, D_BR), lambda i: (i, 0))
    gate = lambda k: pl.BlockSpec((tm, D_BR), lambda i, k=k: (i, C2_MERGE // D_BR + k))
    xs = pl.BlockSpec((tm, D_MODEL), lambda i: (i, 0))
    resident = lambda a: pl.BlockSpec(a.shape, lambda i: (0,) * a.ndim, pipeline_mode=pl.Buffered(1))
    wb = w_branch.astype(BF16)
    wo = w_out.astype(BF16)
    lg, lb = ln_g.reshape(1, D_MODEL), ln_b.reshape(1, D_MODEL)
    ngate = 2 * N_BRANCHES
    return pl.pallas_call(
        _final_kernel,
        grid=(m // tm,),
        in_specs=[br, br, br, br] + [gate(k) for k in range(ngate)] + [xs]
                 + [resident(wb), resident(wo), resident(lg), resident(lb)],
        out_specs=[xs, xs],
        out_shape=[jax.ShapeDtypeStruct((m, D_MODEL), F32), jax.ShapeDtypeStruct((m, D_MODEL), BF16)],
        compiler_params=_params(),
        name="merge_out_ln",
    )(o_a, o_b, o_c, o_m, *([h2] * ngate), x, wb, wo, lg, lb)


def _gate_spread_matrix():
    p = np.zeros((3 * NSA_HEADS, HKV * LANES), np.float32)
    for hh in range(HKV):
        for brn in range(3):
            for g in range(GQA):
                p[brn * NSA_HEADS + hh * GQA + g, hh * LANES + brn * GQA + g] = 1.0
    return p


def _spread_gate_cols(a):
    return jnp.dot(a, jnp.asarray(_gate_spread_matrix()), precision=lax.Precision.HIGHEST)


def layer(l, x, xb, mem_b, tables, bsz, seq, w_in, sgu_ln_g, sgu_ln_b, sgu_w, sgu_b, conv_w, conv_b,
          lru_wa, lru_ba, lru_wx, lru_bx, lru_lambda, cmp_pe_k, cmp_pe_v, cmp_w1_k, cmp_w1_v,
          cmp_w2_k, cmp_w2_v, nsa_gate_b, w_mem_kv, w_branch, w_out, ln_g, ln_b):
    m = bsz * seq
    tm = min(1024, m)
    tm_big = min(2048, m)
    h1 = matmul_f32w(xb, w_in, l, 0, N1, F32, tm_big, 512, w_is_nk=True)
    kv = kv_projection(xb, w_in, l, N1, seq, tm)
    h2 = matmul_f32w(xb, w_in, l, GL_OFF + 3 * NSA_HEADS, N2, F32, tm, 1024, w_is_nk=True,
                     w_single_buffer=True)
    w_gl = _spread_gate_cols(w_in[l, GL_OFF:GL_OFF + 3 * NSA_HEADS, :].T).astype(BF16)
    hgl = matmul(xb, w_gl, F32, tm_big, HKV * LANES)
    gate_b4 = _spread_gate_cols(nsa_gate_b[l].reshape(1, 3 * NSA_HEADS))

    o_a = gmlp_branch(h1, sgu_ln_g[l], sgu_ln_b[l], sgu_w[l], sgu_b[l])
    o_b = lru_branch(h1, bsz, seq, conv_w[l], conv_b[l], lru_wa[l], lru_ba[l], lru_wx[l], lru_bx[l],
                     lru_lambda[l])

    kct, vc = nsa_compress(h1, bsz, seq, cmp_pe_k[l], cmp_pe_v[l],
                           cmp_w1_k[l], cmp_w1_v[l], cmp_w2_k[l], cmp_w2_v[l])
    o_c = nsa_branch(h1, h2, kv, hgl, gate_b4, kct, vc, tables, bsz, seq)

    mrows = mem_b.shape[0]
    kv = matmul_f32w(mem_b, w_mem_kv, l, 0, 2 * D_BR, BF16, min(512, mrows), 512)
    o_m = mem_branch(h2, kv, bsz, seq)
    return final_merge(o_a, o_b, o_c, o_m, h2, x, w_branch[l], w_out[l], ln_g[l], ln_b[l])


def kernel(x, mem, rel_bias, w_in, sgu_ln_g, sgu_ln_b, sgu_w, sgu_b, conv_w, conv_b, lru_wa, lru_ba, lru_wx,
           lru_bx, lru_lambda, cmp_pe_k, cmp_pe_v, cmp_w1_k, cmp_w1_v, cmp_w2_k, cmp_w2_v, nsa_gate_b,
           w_mem_kv, w_branch, w_out, ln_g, ln_b):
    bsz, seq, _ = x.shape
    tables = nsa_bias_tables(rel_bias, seq)
    xf = x.reshape(bsz * seq, D_MODEL)
    xb = xf.astype(BF16)
    mem_b = mem.reshape(-1, D_MODEL).astype(BF16)
    w_in_t = jnp.swapaxes(w_in, 1, 2)
    params = (w_in_t, sgu_ln_g, sgu_ln_b, sgu_w, sgu_b, conv_w, conv_b, lru_wa, lru_ba, lru_wx, lru_bx,
              lru_lambda, cmp_pe_k, cmp_pe_v, cmp_w1_k, cmp_w1_v, cmp_w2_k, cmp_w2_v, nsa_gate_b,
              w_mem_kv, w_branch, w_out, ln_g, ln_b)
    for l in range(w_in.shape[0]):
        xf, xb = layer(l, xf, xb, mem_b, tables, bsz, seq, *params)
    return xf.reshape(bsz, seq, D_MODEL)
```

```python
import functools
import math

import numpy as np
import jax
import jax.numpy as jnp
from jax import lax
from jax.experimental import pallas as pl
from jax.experimental.pallas import tpu as pltpu

F32 = jnp.float32
BF16 = jnp.bfloat16

D_MODEL = 2048
DEPTH = 2
D_BR = D_MODEL // 2
N_BRANCHES = 4
GMLP_CHUNK = 128
GMLP_GROUPS = 8
LRU_BLOCKS = 8
LRU_BLOCK_DIM = D_BR // LRU_BLOCKS
CONV_WIDTH = 4
LRU_C = 8.0
DK = 64
NSA_HEADS = D_BR // DK
HKV = NSA_HEADS // 4
GQA = NSA_HEADS // HKV
KV_W = HKV * DK
CMP_BLOCK = 32
CMP_STRIDE = 16
CMP_HIDDEN = 256
SLC_BLOCK = 64
SLC_TOPK = 8
WINDOW = 256
Q_BLOCK = 128
MEM_HEADS = 4
MEM_HEAD_DIM = D_BR // MEM_HEADS
REL_BUCKETS = 32
REL_MAX_DIST = 1024
DN_ALPHA = (2 * DEPTH) ** 0.25
LN_EPS = 1e-5

LANES = 128
SUBLANES = 8
NEG = -1e30
SEL_TK = 512
SEL_FAR = 11
MASK_ROWS = LANES - DK
NA = Q_BLOCK // CMP_STRIDE
SEL_SUB = 4
ATT_NQ = 2
V_ROWS = DK + 16
VMEM_LIMIT = 56 * 1024 * 1024

C1_U, C1_V, C1_GA, C1_XB, C1_GB, C1_QC = (D_BR * k for k in range(6))
C1_KV = 6 * D_BR
N1 = C1_KV + 2 * KV_W
NKV = 4 * KV_W
C2_GC, C2_QM, C2_GM, C2_MERGE = (D_BR * k for k in range(4))
N2 = C2_MERGE + N_BRANCHES * D_MODEL
GL_OFF = N1 + NKV


def _sigmoid(x):
    return 1.0 / (1.0 + jnp.exp(-x))


def _silu(x):
    return x * _sigmoid(x)


def _gelu_tanh(x):
    return 0.5 * x * (1.0 + jnp.tanh(math.sqrt(2.0 / math.pi) * (x + 0.044715 * (x * x * x))))


def _dot(a, b):
    return jnp.dot(a, b, preferred_element_type=F32)


def _dot_nt(a, b):
    return lax.dot_general(a, b, (((1,), (1,)), ((), ())), preferred_element_type=F32)


def _params(**kw):
    return pltpu.CompilerParams(vmem_limit_bytes=VMEM_LIMIT, **kw)


def _mm_kernel(x_ref, w_ref, o_ref):
    o_ref[...] = _dot(x_ref[...], w_ref[...]).astype(o_ref.dtype)


def matmul(x, w, out_dtype, tm, tn):
    m, k = x.shape
    n = w.shape[1]
    assert m % tm == 0 and n % tn == 0, (m, n, tm, tn)
    return pl.pallas_call(
        _mm_kernel,
        grid=(n // tn, m // tm),
        in_specs=[pl.BlockSpec((tm, k), lambda j, i: (i, 0)),
                  pl.BlockSpec((k, tn), lambda j, i: (0, j))],
        out_specs=pl.BlockSpec((tm, tn), lambda j, i: (i, j)),
        out_shape=jax.ShapeDtypeStruct((m, n), out_dtype),
        compiler_params=_params(),
        name="matmul",
    )(x, w)


def _mm_castw_kernel(x_ref, *rest, w_is_nk, shift):
    w_refs, (o_ref, wb_ref) = rest[:-2], rest[-2:]

    @pl.when(pl.program_id(1) == 0)
    def _():
        if shift:
            w = jnp.concatenate([w_refs[0][0, shift:, :], w_refs[1][0, :shift, :]], axis=0)
        else:
            w = w_refs[0][0]
        wb_ref[...] = (w.T if w_is_nk else w).astype(BF16)

    o_ref[...] = _dot(x_ref[...], wb_ref[...]).astype(o_ref.dtype)


def matmul_f32w(x, w3, layer_idx, col0, n, out_dtype, tm, tn, w_is_nk=False, w_single_buffer=False):
    m, k = x.shape
    ncols = w3.shape[1] if w_is_nk else w3.shape[2]
    shift = col0 % tn
    assert m % tm == 0 and n % tn == 0 and col0 + n <= ncols, (m, n, col0, tm, tn)
    assert shift == 0 or (w_is_nk and shift % SUBLANES == 0), (col0, tn)
    j0 = col0 // tn
    mode = dict(pipeline_mode=pl.Buffered(1)) if w_single_buffer else {}
    if w_is_nk:
        w_specs = [pl.BlockSpec((1, tn, k), lambda j, i, d=d: (layer_idx, j0 + j + d, 0), **mode)
                   for d in range(2 if shift else 1)]
    else:
        w_specs = [pl.BlockSpec((1, k, tn), lambda j, i: (layer_idx, 0, j0 + j), **mode)]
    return pl.pallas_call(
        functools.partial(_mm_castw_kernel, w_is_nk=w_is_nk, shift=shift),
        grid=(n // tn, m // tm),
        in_specs=[pl.BlockSpec((tm, k), lambda j, i: (i, 0))] + w_specs,
        out_specs=pl.BlockSpec((tm, tn), lambda j, i: (i, j)),
        out_shape=jax.ShapeDtypeStruct((m, n), out_dtype),
        scratch_shapes=[pltpu.VMEM((k, tn), BF16)],
        compiler_params=_params(dimension_semantics=("arbitrary", "arbitrary")),
        name="matmul_f32w",
    )(x, *([w3] * len(w_specs)))


def _kv_proj_kernel(x_ref, ws_ref, ww_ref, wgl_ref, ks_ref, kw_ref, vs_ref, vw_ref, gl_ref, wk_s, wv_s, *, tm, seq):
    @pl.when(pl.program_id(0) == 0)
    def _():
        wk = jnp.concatenate([ws_ref[0, 0:KV_W], ww_ref[0, 0:KV_W]], axis=0)
        wk_s[...] = wk.T.astype(BF16)
        wv_s[...] = jnp.concatenate([ws_ref[0, KV_W:2 * KV_W], ww_ref[0, KV_W:2 * KV_W]], axis=0).astype(BF16)

    x = x_ref[...]
    gl_ref[...] = _dot(x, wgl_ref[...])
    kk = _dot(x, wk_s[...])
    vt = _dot_nt(wv_s[...], x)
    pos = (pl.program_id(0) * tm + lax.broadcasted_iota(jnp.int32, (tm, MASK_ROWS), 0)) % seq
    blk = lax.broadcasted_iota(jnp.int32, (tm, MASK_ROWS), 1)
    onehot = jnp.where((pos >> int(math.log2(SLC_BLOCK))) == blk, 1.0, 0.0)
    zeros = jnp.zeros((tm, MASK_ROWS), F32)
    tail = jnp.where(lax.broadcasted_iota(jnp.int32, (LANES - DK, tm), 0) == 0, 1.0, 0.0)
    for h in range(HKV):
        ks_ref[:, h * LANES:(h + 1) * LANES] = jnp.concatenate(
            [kk[:, h * DK:(h + 1) * DK], onehot], axis=1).astype(ks_ref.dtype)
        kw_ref[:, h * LANES:(h + 1) * LANES] = jnp.concatenate(
            [kk[:, KV_W + h * DK:KV_W + (h + 1) * DK], zeros], axis=1).astype(kw_ref.dtype)
        vs_ref[h] = jnp.concatenate([vt[h * DK:(h + 1) * DK], tail], axis=0).astype(vs_ref.dtype)
        vw_ref[h] = jnp.concatenate([vt[KV_W + h * DK:KV_W + (h + 1) * DK], tail], axis=0).astype(vw_ref.dtype)


def kv_projection(x, wt3, layer_idx, row0, w_gl, seq, tm):
    m, k = x.shape
    nrows = 2 * KV_W
    assert m % tm == 0 and row0 % nrows == 0 and seq % tm == 0
    wspec = lambda d: pl.BlockSpec((1, nrows, k), lambda i: (layer_idx, row0 // nrows + d, 0),
                                   pipeline_mode=pl.Buffered(1))
    keys = jax.ShapeDtypeStruct((m, HKV * LANES), BF16)
    vals = jax.ShapeDtypeStruct((HKV, LANES, m), BF16)
    return pl.pallas_call(
        functools.partial(_kv_proj_kernel, tm=tm, seq=seq),
        grid=(m // tm,),
        in_specs=[pl.BlockSpec((tm, k), lambda i: (i, 0)), wspec(0), wspec(1),
                  pl.BlockSpec(w_gl.shape, lambda i: (0, 0), pipeline_mode=pl.Buffered(1))],
        out_specs=[pl.BlockSpec((tm, HKV * LANES), lambda i: (i, 0))] * 2
                  + [pl.BlockSpec((HKV, LANES, tm), lambda i: (0, 0, i))] * 2
                  + [pl.BlockSpec((tm, w_gl.shape[1]), lambda i: (i, 0))],
        out_shape=[keys, keys, vals, vals, jax.ShapeDtypeStruct((m, w_gl.shape[1]), F32)],
        scratch_shapes=[pltpu.VMEM((k, 2 * KV_W), BF16), pltpu.VMEM((2 * KV_W, k), BF16)],
        compiler_params=_params(dimension_semantics=("arbitrary",)),
        name="kv_projection",
    )(x, wt3, wt3, w_gl)


def _gmlp_kernel(u_ref, v_ref, ga_ref, lng_ref, lnb_ref, w_ref, bs_ref, o_ref, *, rows):
    gd = D_BR // GMLP_GROUPS
    for c in range(rows // GMLP_CHUNK):
        r = slice(c * GMLP_CHUNK, (c + 1) * GMLP_CHUNK)
        v = _gelu_tanh(v_ref[r, :])
        mu = jnp.mean(v, axis=-1, keepdims=True)
        vc = v - mu
        var = jnp.mean(vc * vc, axis=-1, keepdims=True)
        vb = (vc * lax.rsqrt(var + LN_EPS) * lng_ref[...] + lnb_ref[...]).astype(BF16)
        u = _gelu_tanh(u_ref[r, :]) * _silu(ga_ref[r, :])
        for g in range(GMLP_GROUPS):
            cs = slice(g * gd, (g + 1) * gd)
            mixed = _dot(w_ref[g], vb[:, cs]) + bs_ref[:, g:g + 1]
            o_ref[r, cs] = (u[:, cs] * mixed).astype(o_ref.dtype)


def gmlp_branch(h1, ln_g, ln_b, w_s, b_s, rows=1024):
    m = h1.shape[0]
    rows = min(rows, m)
    causal = jnp.tril(jnp.ones((GMLP_CHUNK, GMLP_CHUNK), dtype=bool))
    w = jnp.where(causal, w_s, 0).astype(BF16)
    col = lambda c: pl.BlockSpec((rows, D_BR), lambda i, c=c: (i, c // D_BR))
    full = lambda a: pl.BlockSpec(a.shape, lambda i: (0,) * a.ndim)
    args = (ln_g.reshape(1, D_BR), ln_b.reshape(1, D_BR), w, b_s.T)
    return pl.pallas_call(
        functools.partial(_gmlp_kernel, rows=rows),
        grid=(m // rows,),
        in_specs=[col(C1_U), col(C1_V), col(C1_GA)] + [full(a) for a in args],
        out_specs=pl.BlockSpec((rows, D_BR), lambda i: (i, 0)),
        out_shape=jax.ShapeDtypeStruct((m, D_BR), BF16),
        compiler_params=_params(),
        name="gmlp",
    )(h1, h1, h1, *args)


def _lru_kernel(xb_ref, gb_ref, cw_ref, cb_ref, wa_ref, ba_ref, wx_ref, bx_ref, lam_ref, o_ref,
                xbuf, hcarry, a_s, g_s, *, ts):
    @pl.when(pl.program_id(1) == 0)
    def _():
        xbuf[0:8, :] = jnp.zeros((8, D_BR), F32)
        hcarry[...] = jnp.zeros((8, D_BR), F32)

    xbuf[8:8 + ts, :] = xb_ref[...]
    xc = cb_ref[...] + cw_ref[0:1, :] * xbuf[pl.ds(8 - (CONV_WIDTH - 1), ts), :]
    for k in range(1, CONV_WIDTH):
        xc = xc + cw_ref[k:k + 1, :] * xbuf[pl.ds(8 - (CONV_WIDTH - 1) + k, ts), :]
    xbuf[0:8, :] = xbuf[ts:ts + 8, :]

    xcb = xc.astype(BF16)
    bd = LRU_BLOCK_DIM
    r = jnp.concatenate([_dot(xcb[:, n * bd:(n + 1) * bd], wa_ref[n]) for n in range(LRU_BLOCKS)], axis=1)
    i = jnp.concatenate([_dot(xcb[:, n * bd:(n + 1) * bd], wx_ref[n]) for n in range(LRU_BLOCKS)], axis=1)
    r = _sigmoid(r + ba_ref[...])
    i = _sigmoid(i + bx_ref[...])
    nl = -lam_ref[...]
    softplus = jnp.maximum(nl, 0.0) + jnp.log1p(jnp.exp(-jnp.abs(nl)))
    log_a = (-LRU_C * softplus) * r
    a = jnp.exp(log_a)
    a_s[...] = a
    g_s[...] = jnp.sqrt(1.0 - a * a) * i * xc

    row = lax.broadcasted_iota(jnp.int32, (8, D_BR), 0)

    def body(j, carry):
        r0 = pl.multiple_of(j * 8, 8)
        av = a_s[pl.ds(r0, 8), :]
        bv = g_s[pl.ds(r0, 8), :]
        for d in (1, 2, 4):
            keep = row >= d
            a_sh = pltpu.roll(av, d, axis=0)
            b_sh = pltpu.roll(bv, d, axis=0)
            bv = jnp.where(keep, av * b_sh + bv, bv)
            av = jnp.where(keep, av * a_sh, av)
        hv = av * carry + bv
        g_s[pl.ds(r0, 8), :] = hv
        return jnp.broadcast_to(hv[7:8, :], (8, D_BR))

    hcarry[...] = lax.fori_loop(0, ts // 8, body, hcarry[...])
    o_ref[...] = (g_s[...] * _silu(gb_ref[...])).astype(o_ref.dtype)


def lru_branch(h1, bsz, seq, conv_w, conv_b, wa, ba, wx, bx, lam, ts=512):
    ts = min(ts, seq)
    ns = seq // ts
    col = lambda c: pl.BlockSpec((ts, D_BR), lambda b, s, c=c: (b * ns + s, c // D_BR))
    full = lambda a: pl.BlockSpec(a.shape, lambda b, s: (0,) * a.ndim)
    row = lambda a: a.reshape(1, D_BR)
    args = (conv_w, row(conv_b), wa.astype(BF16), row(ba), wx.astype(BF16), row(bx), row(lam))
    return pl.pallas_call(
        functools.partial(_lru_kernel, ts=ts),
        grid=(bsz, ns),
        in_specs=[col(C1_XB), col(C1_GB)] + [full(a) for a in args],
        out_specs=pl.BlockSpec((ts, D_BR), lambda b, s: (b * ns + s, 0)),
        out_shape=jax.ShapeDtypeStruct((bsz * seq, D_BR), BF16),
        scratch_shapes=[pltpu.VMEM((ts + 8, D_BR), F32), pltpu.VMEM((8, D_BR), F32),
                        pltpu.VMEM((ts, D_BR), F32), pltpu.VMEM((ts, D_BR), F32)],
        compiler_params=_params(dimension_semantics=("arbitrary", "arbitrary")),
        name="lru",
    )(h1, h1, *args)


def _mem_kernel(q_ref, g_ref, kv_ref, o_ref):
    hd = MEM_HEAD_DIM
    for hh in range(MEM_HEADS):
        cs = slice(hh * hd, (hh + 1) * hd)
        q = (q_ref[:, cs] * (hd ** -0.5)).astype(BF16)
        s = _dot_nt(q, kv_ref[:, cs])
        p = jnp.exp(s - jnp.max(s, axis=-1, keepdims=True))
        l = jnp.sum(p, axis=-1, keepdims=True)
        o = _dot(p.astype(BF16), kv_ref[:, D_BR + hh * hd:D_BR + (hh + 1) * hd]) / l
        o_ref[:, cs] = (o * _silu(g_ref[:, cs])).astype(o_ref.dtype)


def mem_branch(h2, kv, bsz, seq, tq=1024):
    tq = min(tq, seq)
    nq = seq // tq
    mlen = kv.shape[0] // bsz
    col = lambda c: pl.BlockSpec((tq, D_BR), lambda b, i, c=c: (b * nq + i, c // D_BR))
    return pl.pallas_call(
        _mem_kernel,
        grid=(bsz, nq),
        in_specs=[col(C2_QM), col(C2_GM), pl.BlockSpec((mlen, 2 * D_BR), lambda b, i: (b, 0))],
        out_specs=pl.BlockSpec((tq, D_BR), lambda b, i: (b * nq + i, 0)),
        out_shape=jax.ShapeDtypeStruct((bsz * seq, D_BR), BF16),
        compiler_params=_params(),
        name="mem_attn",
    )(h2, h2, kv)


def _cmp_kernel(*refs):
    ngrp = 2 * KV_W // LANES
    t_refs = refs[:ngrp]
    pek_ref, pev_ref, w1k_ref, w1v_ref, w2kt_ref, w2v_ref, kct_ref, vc_ref = refs[ngrp:]
    n = t_refs[0].shape[0] // CMP_STRIDE
    for h in range(HKV):
        for is_v, (pe_ref, w1_ref) in enumerate(((pek_ref, w1k_ref), (pev_ref, w1v_ref))):
            grp, off = divmod(is_v * KV_W + h * DK, LANES)
            lo = hi = None
            for tok in range(CMP_STRIDE):
                x = t_refs[grp][pl.ds(tok, n, stride=CMP_STRIDE), :][:, off:off + DK]
                t2 = CMP_STRIDE + tok
                a = _dot((x + pe_ref[tok:tok + 1, :]).astype(BF16), w1_ref[tok * DK:(tok + 1) * DK, :])
                b = _dot((x + pe_ref[t2:t2 + 1, :]).astype(BF16), w1_ref[t2 * DK:(t2 + 1) * DK, :])
                lo = a if lo is None else lo + a
                hi = b if hi is None else hi + b
            hidden = _silu(lo + pltpu.roll(hi, n - 1, axis=0)).astype(BF16)
            if is_v:
                vc_ref[h] = _dot(hidden, w2v_ref[...]).astype(vc_ref.dtype)
            else:
                kct_ref[h] = _dot_nt(w2kt_ref[...], hidden).astype(kct_ref.dtype)


def nsa_compress(h1, bsz, seq, pe_k, pe_v, w1_k, w1_v, w2_k, w2_v):
    n = seq // CMP_STRIDE
    args = (pe_k, pe_v, w1_k.astype(BF16), w1_v.astype(BF16), w2_k.T.astype(BF16), w2_v.astype(BF16))
    full = lambda a: pl.BlockSpec(a.shape, lambda b: (0,) * a.ndim)
    return pl.pallas_call(
        _cmp_kernel,
        grid=(bsz,),
        in_specs=[pl.BlockSpec((seq, LANES), lambda b, g=g: (b, C1_KV // LANES + g))
                  for g in range(2 * KV_W // LANES)] + [full(a) for a in args],
        out_specs=[pl.BlockSpec((HKV, DK, n), lambda b: (b, 0, 0)), pl.BlockSpec((HKV, n, DK), lambda b: (b, 0, 0))],
        out_shape=[jax.ShapeDtypeStruct((bsz * HKV, DK, n), BF16), jax.ShapeDtypeStruct((bsz * HKV, n, DK), BF16)],
        compiler_params=_params(),
        name="nsa_compress",
    )(*([h1] * (2 * KV_W // LANES)), *args)


def _nsa_select_kernel(q_ref, gl_ref, gb_ref, kct_ref, vc_ref, bc_ref, ovt_ref, qsel_ref, ocg_ref, vt_s):
    qb = Q_BLOCK
    lane = lax.broadcasted_iota(jnp.int32, (qb, LANES), 1)
    jrow = lax.broadcasted_iota(jnp.int32, (MASK_ROWS, qb), 0).astype(F32)
    qlane = lax.broadcasted_iota(jnp.int32, (MASK_ROWS, qb), 1)
    sub = lax.broadcasted_iota(jnp.int32, (SUBLANES, qb), 0)
    nv = MASK_ROWS // SUBLANES
    ovt = ovt_ref[...]
    kct = kct_ref[0]
    vc = vc_ref[0]
    for sb in range(SEL_SUB):
        rows = slice(sb * qb, (sb + 1) * qb)
        blk = pl.program_id(2) * SEL_SUB + sb
        q = q_ref[rows, :] * (DK ** -0.5)
        gt = _sigmoid(gl_ref[rows, :] + gb_ref[...])
        qpad = []
        for g in range(GQA):
            t = q[:, (g // 2) * LANES:(g // 2 + 1) * LANES]
            if g % 2:
                t = pltpu.roll(t, DK, axis=1)
            qpad.append(jnp.where(lane < DK, t, 0.0))

        has_keys = blk * qb + lax.broadcasted_iota(jnp.int32, (qb, 1), 0) >= CMP_BLOCK - 1
        o_c = []
        psum = None
        for g in range(GQA):
            s = _dot(qpad[g][:, :DK].astype(BF16), kct) + bc_ref[0, sb, g * qb:(g + 1) * qb, :]
            m = jnp.max(s, axis=-1, keepdims=True)
            p = jnp.exp(s - m)
            p = p * jnp.where(has_keys, 1.0 / jnp.maximum(jnp.sum(p, axis=-1, keepdims=True), 1e-30), 0.0)
            o_c.append(gt[:, g:g + 1] * _dot(p.astype(BF16), vc))
            psum = p if psum is None else psum + p
        ocg_ref[rows, :] = jnp.concatenate(o_c, axis=1)

        p_hi = psum.astype(BF16)
        rem = psum - p_hi.astype(F32)
        p_mid = rem.astype(BF16)
        p_lo = (rem - p_mid.astype(F32)).astype(BF16)
        imp_t = _dot_nt(ovt, p_hi) + _dot_nt(ovt, p_mid) + _dot_nt(ovt, p_lo)

        qblk = ((blk * qb + qlane) >> int(math.log2(SLC_BLOCK))).astype(F32)
        val = jnp.where(jrow == 0.0, 3e38,
                        jnp.where(jrow == qblk, 3e38, jnp.where(jrow == qblk - 1.0, 3e38, imp_t)))
        val = jnp.where(jrow > qblk, -1.0, val)
        vt_s[sb] = val
        vals = [val[k * SUBLANES:(k + 1) * SUBLANES] for k in range(nv)]
        cnt = [jnp.zeros((SUBLANES, qb), F32) for _ in range(nv)]
        for jp in range(MASK_ROWS):
            rowv = jnp.broadcast_to(vt_s[sb, jp:jp + 1, :], (SUBLANES, qb))
            for k in range(nv):
                if k * SUBLANES > jp:
                    beat = jnp.where(rowv >= vals[k], 1.0, 0.0)
                elif (k + 1) * SUBLANES - 1 < jp:
                    beat = jnp.where(rowv > vals[k], 1.0, 0.0)
                else:
                    beat = jnp.where(sub > jp - k * SUBLANES, jnp.where(rowv >= vals[k], 1.0, 0.0),
                                     jnp.where(rowv > vals[k], 1.0, 0.0))
                cnt[k] = cnt[k] + beat
        selneg_t = jnp.concatenate([jnp.where(c < float(SLC_TOPK), 0.0, NEG) for c in cnt], axis=0)
        for g in range(GQA):
            qt = jnp.concatenate([qpad[g].T[:DK], selneg_t], axis=0)
            qsel_ref[0, sb, :, g * qb:(g + 1) * qb] = qt.astype(qsel_ref.dtype)


def _nsa_attn_kernel(qsel_ref, ocg_ref, gc_ref, gl_ref, gb_ref, ks_ref, vs_ref, gs_ref, kw_ref, vw_ref, bw_ref,
                     o_ref, s_s, p_s):
    qb = Q_BLOCK
    hq = GQA * qb
    nq = ATT_NQ * hq
    i = pl.program_id(2) * ATT_NQ
    step = SEL_TK // qb
    n_tiles = (i * qb) // SEL_TK + 1
    qt = jnp.concatenate([qsel_ref[0, b] for b in range(ATT_NQ)], axis=1)

    def stage_qk(j):
        k0 = pl.multiple_of(j * SEL_TK, SEL_TK)
        sc = _dot(ks_ref[pl.ds(k0, SEL_TK), :], qt)
        for b in range(ATT_NQ):
            dd = jnp.minimum(i + b - j * step, SEL_FAR)
            boff = pl.multiple_of((SEL_FAR - dd) * qb, qb)
            s_s[:, b * hq:(b + 1) * hq] = sc[:, b * hq:(b + 1) * hq] + gs_ref[0, pl.ds(boff, SEL_TK), :]

    def stage_softmax(m):
        m_new, alpha = [], []
        for g in range(ATT_NQ * GQA):
            cs = slice(g * qb, (g + 1) * qb)
            sc = s_s[:, cs]
            mg = jnp.maximum(m[:, cs], jnp.max(sc, axis=0, keepdims=True))
            alpha.append(jnp.exp(m[:, cs] - mg))
            p_s[:, cs] = jnp.exp(sc - mg).astype(BF16)
            m_new.append(mg)
        return jnp.concatenate(m_new, axis=1), jnp.concatenate(alpha, axis=1)

    def stage_pv(j, acc):
        k0 = pl.multiple_of(jnp.maximum(j, 0) * SEL_TK, SEL_TK)
        return acc + _dot(vs_ref[0, :V_ROWS, pl.ds(k0, SEL_TK)], p_s[...])

    def advance(j, m, acc):
        acc = stage_pv(j - 2, acc)
        m, alpha = stage_softmax(m)
        return m, alpha * acc

    def body(j, carry):
        m, acc = advance(j, *carry)
        stage_qk(j)
        return m, acc

    p_s[...] = jnp.zeros(p_s.shape, BF16)
    stage_qk(jnp.int32(0))
    init = (jnp.full((1, nq), NEG, F32), jnp.zeros((V_ROWS, nq), F32))
    m, acc = lax.fori_loop(1, n_tiles, body, init)
    _, acc = advance(n_tiles, m, acc)
    acc_s = stage_pv(n_tiles - 1, acc)

    wk = WINDOW + qb
    rowi = lax.broadcasted_iota(jnp.int32, (LANES, nq), 0)
    q_win = jnp.where(rowi < DK, qt, jnp.zeros_like(qt))
    acc_w = []
    for b in range(ATT_NQ):
        cs = slice(b * hq, (b + 1) * hq)
        w0 = pl.multiple_of(jnp.maximum((i + b) * qb - WINDOW, 0), qb)
        sw = _dot(kw_ref[pl.ds(w0, wk), :], q_win[:, cs]) + bw_ref[0, 0, :, cs]
        pw = jnp.exp(sw - jnp.max(sw, axis=0, keepdims=True)).astype(BF16)
        acc_w.append(_dot(vw_ref[0, :V_ROWS, pl.ds(w0, wk)], pw))
    acc_w = jnp.concatenate(acc_w, axis=1)

    pad_rows = jnp.zeros((LANES - DK, qb), F32)
    for b in range(ATT_NQ):
        rows = slice(b * qb, (b + 1) * qb)
        gt_t = _sigmoid(gl_ref[rows, :] + gb_ref[...]).T
        outs = []
        for g in range(GQA):
            cs = slice(b * hq + g * qb, b * hq + (g + 1) * qb)
            o_s = acc_s[:DK, cs] * (gt_t[GQA + g:GQA + g + 1, :] / acc_s[DK:DK + 1, cs])
            o_w = acc_w[:DK, cs] * (gt_t[2 * GQA + g:2 * GQA + g + 1, :] / acc_w[DK:DK + 1, cs])
            outs.append(jnp.concatenate([o_s + o_w, pad_rows], axis=0).T[:, :DK])
        o = ocg_ref[rows, :] + jnp.concatenate(outs, axis=1)
        o_ref[rows, :] = (o * _silu(gc_ref[rows, :])).astype(o_ref.dtype)


def _rel_bucket(dist):
    n = jnp.maximum(dist, 0)
    exact = REL_BUCKETS // 2
    nf = jnp.maximum(n, 1).astype(jnp.float32)
    large = exact + (jnp.log(nf / exact) / math.log(REL_MAX_DIST / exact)
                     * (REL_BUCKETS - exact)).astype(jnp.int32)
    return jnp.where(n < exact, n, jnp.minimum(large, REL_BUCKETS - 1))


def nsa_bias_tables(rel_bias, seq):
    qb = Q_BLOCK
    n_pad = seq // CMP_STRIDE
    c0 = n_pad - NA
    wc = 2 * n_pad
    ws = SEL_FAR * qb + SEL_TK
    nmax = max(seq, SEL_FAR * qb + qb)
    padl = CMP_STRIDE * (wc + 2)
    bvec = rel_bias[_rel_bucket(jnp.arange(nmax))].T.astype(F32)
    vext = jnp.concatenate([jnp.full((NSA_HEADS, padl), NEG, F32), bvec], axis=1)
    pos = jnp.arange(padl + nmax) - padl
    vwin = jnp.where(pos < WINDOW, vext, NEG)

    mlo, mhi = c0 - wc + 1, NA - 1 + c0
    start = padl + CMP_STRIDE * mlo - (CMP_BLOCK - 1)
    assert start >= 0
    u = vext[:, start:start + CMP_STRIDE * (mhi - mlo + 1)].reshape(NSA_HEADS, mhi - mlo + 1, CMP_STRIDE)
    urev = u[:, ::-1, :]
    gcb = jnp.stack([urev[:, NA - 1 - a:NA - 1 - a + wc, :] for a in range(NA)], axis=1)
    gcb = gcb.transpose(0, 1, 3, 2).reshape(HKV, GQA * qb, wc)
    nqb = seq // qb
    bc = jnp.stack([gcb[:, :, c0 - NA * i:c0 - NA * i + n_pad] for i in range(nqb)], axis=1)

    def toeplitz(vec, d0, width):
        vrev = vec[:, ::-1]
        s0 = nmax - d0 - qb
        period = width + qb - 1
        assert s0 >= 0 and s0 + period <= padl + nmax
        sl = vrev[:, s0:s0 + period]
        y = jnp.concatenate([sl[:, qb - 1:], sl[:, :qb - 1]], axis=1)
        z = jnp.tile(y, (1, qb))[:, :qb * (period - 1)].reshape(NSA_HEADS, qb, period - 1)
        return z[:, :, :width].reshape(HKV, GQA * qb, width)

    gs = toeplitz(vext, SEL_FAR * qb, ws)
    bw = tuple(tuple(toeplitz(vwin, d0, WINDOW + qb) for d0 in starts)
               for starts in ([min(b * qb, WINDOW) for b in range(ATT_NQ)], [WINDOW] * ATT_NQ))
    return bc, gs, bw


def _overlap_matrix_t(n_pad):
    c_start = np.arange(n_pad)[None, :] * CMP_STRIDE
    s_start = np.arange(MASK_ROWS)[:, None] * SLC_BLOCK
    ov = np.clip(np.minimum(c_start + CMP_BLOCK, s_start + SLC_BLOCK) - np.maximum(c_start, s_start), 0, None)
    return ov.astype(np.float32) / CMP_BLOCK


def nsa_branch(h1, h2, kv, hgl, gate_b4, kct, vc, tables, bsz, seq):
    qb = Q_BLOCK
    nqb = seq // qb
    n_pad = seq // CMP_STRIDE
    n_slc = seq // SLC_BLOCK
    assert n_slc <= MASK_ROWS and seq % SEL_TK == 0 and nqb % SEL_SUB == 0
    assert (SEL_TK // qb) % ATT_NQ == 0 and nqb % ATT_NQ == 0
    bc, gs, bw = tables
    m = bsz * seq
    gw = GQA * DK

    sq = SEL_SUB * qb
    nsq = seq // sq
    ovt = jnp.asarray(_overlap_matrix_t(n_pad), BF16)
    qsel, ocg = pl.pallas_call(
        _nsa_select_kernel,
        grid=(HKV, bsz, nsq),
        in_specs=[
            pl.BlockSpec((sq, gw), lambda hh, b, i: (b * nsq + i, C1_QC // gw + hh)),
            pl.BlockSpec((sq, LANES), lambda hh, b, i: (b * nsq + i, hh)),
            pl.BlockSpec((1, LANES), lambda hh, b, i: (0, hh)),
            pl.BlockSpec((1, DK, n_pad), lambda hh, b, i: (b * HKV + hh, 0, 0)),
            pl.BlockSpec((1, n_pad, DK), lambda hh, b, i: (b * HKV + hh, 0, 0)),
            pl.BlockSpec((1, SEL_SUB) + bc.shape[2:], lambda hh, b, i: (hh, i, 0, 0)),
            pl.BlockSpec(ovt.shape, lambda hh, b, i: (0, 0)),
        ],
        out_specs=[pl.BlockSpec((1, SEL_SUB, LANES, GQA * qb), lambda hh, b, i: (hh, b * nsq + i, 0, 0)),
                   pl.BlockSpec((sq, gw), lambda hh, b, i: (b * nsq + i, hh))],
        out_shape=[jax.ShapeDtypeStruct((HKV, bsz * nqb, LANES, GQA * qb), BF16),
                   jax.ShapeDtypeStruct((m, D_BR), F32)],
        scratch_shapes=[pltpu.VMEM((SEL_SUB, MASK_ROWS, qb), F32)],
        compiler_params=_params(),
        name="nsa_select",
    )(h1, hgl, gate_b4, kct, vc, bc, ovt)

    gs_t = gs.transpose(0, 2, 1)
    bw_t = jnp.stack([jnp.concatenate([w.transpose(0, 2, 1) for w in variant], axis=2) for variant in bw],
                     axis=1)
    ks_ext, kw_ext, vs_ext, vw_ext = kv

    aq = ATT_NQ * qb
    npair = nqb // ATT_NQ
    rowblk = lambda width, c0: pl.BlockSpec((aq, width), lambda hh, b, i: (b * npair + i, c0 // width + hh))
    keys = pl.BlockSpec((seq, LANES), lambda hh, b, i: (b, hh))
    vals = pl.BlockSpec((1, LANES, seq), lambda hh, b, i: (hh, 0, b))
    return pl.pallas_call(
        _nsa_attn_kernel,
        grid=(HKV, bsz, npair),
        in_specs=[
            pl.BlockSpec((1, ATT_NQ, LANES, GQA * qb), lambda hh, b, i: (hh, b * npair + i, 0, 0)),
            rowblk(gw, 0), rowblk(gw, C2_GC),
            pl.BlockSpec((aq, LANES), lambda hh, b, i: (b * npair + i, hh)),
            pl.BlockSpec((1, LANES), lambda hh, b, i: (0, hh)),
            keys, vals, pl.BlockSpec((1,) + gs_t.shape[1:], lambda hh, b, i: (hh, 0, 0)),
            keys, vals, pl.BlockSpec((1, 1) + bw_t.shape[2:], lambda hh, b, i: (hh, jnp.minimum(i, 1), 0, 0)),
        ],
        out_specs=pl.BlockSpec((aq, gw), lambda hh, b, i: (b * npair + i, hh)),
        out_shape=jax.ShapeDtypeStruct((m, D_BR), BF16),
        scratch_shapes=[pltpu.VMEM((SEL_TK, ATT_NQ * GQA * qb), F32), pltpu.VMEM((SEL_TK, ATT_NQ * GQA * qb), BF16)],
        compiler_params=_params(),
        name="nsa_attn",
    )(qsel, ocg, h2, hgl, gate_b4, ks_ext, vs_ext, gs_t, kw_ext, vw_ext, bw_t)


def _final_kernel(oa_ref, ob_ref, oc_ref, om_ref, *rest):
    gate_refs = rest[:2 * N_BRANCHES]
    x_ref, wb_ref, wo_ref, lng_ref, lnb_ref, y_ref, yb_ref = rest[2 * N_BRANCHES:]
    o_refs = (oa_ref, ob_ref, oc_ref, om_ref)
    halves = []
    for c in range(2):
        cs = slice(c * D_BR, (c + 1) * D_BR)
        acc = None
        for k in range(N_BRANCHES):
            term = _sigmoid(gate_refs[2 * k + c][...]) * _dot(o_refs[k][...], wb_ref[k, :, cs])
            acc = term if acc is None else acc + term
        halves.append(acc.astype(BF16))
    merged = jnp.concatenate(halves, axis=1)
    z = DN_ALPHA * x_ref[...] + _dot(merged, wo_ref[...])
    mu = jnp.mean(z, axis=-1, keepdims=True)
    zc = z - mu
    var = jnp.mean(zc * zc, axis=-1, keepdims=True)
    y = zc * lax.rsqrt(var + LN_EPS) * lng_ref[...] + lnb_ref[...]
    y_ref[...] = y
    yb_ref[...] = y.astype(BF16)


def final_merge(o_a, o_b, o_c, o_m, h2, x, w_branch, w_out, ln_g, ln_b, tm=128):
    m = x.shape[0]
    tm = min(tm, m)
    br = pl.BlockSpec((tm, D_BR), lambda i: (i, 0))
    gate = lambda k: pl.BlockSpec((tm, D_BR), lambda i, k=k: (i, C2_MERGE // D_BR + k))
    xs = pl.BlockSpec((tm, D_MODEL), lambda i: (i, 0))
    resident = lambda a: pl.BlockSpec(a.shape, lambda i: (0,) * a.ndim, pipeline_mode=pl.Buffered(1))
    wb = w_branch.astype(BF16)
    wo = w_out.astype(BF16)
    lg, lb = ln_g.reshape(1, D_MODEL), ln_b.reshape(1, D_MODEL)
    ngate = 2 * N_BRANCHES
    return pl.pallas_call(
        _final_kernel,
        grid=(m // tm,),
        in_specs=[br, br, br, br] + [gate(k) for k in range(ngate)] + [xs]
                 + [resident(wb), resident(wo), resident(lg), resident(lb)],
        out_specs=[xs, xs],
        out_shape=[jax.ShapeDtypeStruct((m, D_MODEL), F32), jax.ShapeDtypeStruct((m, D_MODEL), BF16)],
        compiler_params=_params(),
        name="merge_out_ln",
    )(o_a, o_b, o_c, o_m, *([h2] * ngate), x, wb, wo, lg, lb)


def _gate_spread_matrix():
    p = np.zeros((3 * NSA_HEADS, HKV * LANES), np.float32)
    for hh in range(HKV):
        for brn in range(3):
            for g in range(GQA):
                p[brn * NSA_HEADS + hh * GQA + g, hh * LANES + brn * GQA + g] = 1.0
    return p


def _spread_gate_cols(a):
    return jnp.dot(a, jnp.asarray(_gate_spread_matrix()), precision=lax.Precision.HIGHEST)


def layer(l, x, xb, mem_b, tables, bsz, seq, w_in, sgu_ln_g, sgu_ln_b, sgu_w, sgu_b, conv_w, conv_b,
          lru_wa, lru_ba, lru_wx, lru_bx, lru_lambda, cmp_pe_k, cmp_pe_v, cmp_w1_k, cmp_w1_v,
          cmp_w2_k, cmp_w2_v, nsa_gate_b, w_mem_kv, w_branch, w_out, ln_g, ln_b):
    m = bsz * seq
    tm = min(1024, m)
    tm_big = min(2048, m)
    h1 = matmul_f32w(xb, w_in, l, 0, N1, F32, tm_big, 512, w_is_nk=True)
    h2 = matmul_f32w(xb, w_in, l, GL_OFF + 3 * NSA_HEADS, N2, F32, tm, 1024, w_is_nk=True,
                     w_single_buffer=True)
    w_gl = _spread_gate_cols(w_in[l, GL_OFF:GL_OFF + 3 * NSA_HEADS, :].T).astype(BF16)
    *kv, hgl = kv_projection(xb, w_in, l, N1, w_gl, seq, tm)
    gate_b4 = _spread_gate_cols(nsa_gate_b[l].reshape(1, 3 * NSA_HEADS))

    o_a = gmlp_branch(h1, sgu_ln_g[l], sgu_ln_b[l], sgu_w[l], sgu_b[l])
    o_b = lru_branch(h1, bsz, seq, conv_w[l], conv_b[l], lru_wa[l], lru_ba[l], lru_wx[l], lru_bx[l],
                     lru_lambda[l])

    kct, vc = nsa_compress(h1, bsz, seq, cmp_pe_k[l], cmp_pe_v[l],
                           cmp_w1_k[l], cmp_w1_v[l], cmp_w2_k[l], cmp_w2_v[l])
    o_c = nsa_branch(h1, h2, kv, hgl, gate_b4, kct, vc, tables, bsz, seq)

    mrows = mem_b.shape[0]
    kv = matmul_f32w(mem_b, w_mem_kv, l, 0, 2 * D_BR, BF16, min(512, mrows), 512)
    o_m = mem_branch(h2, kv, bsz, seq)
    return final_merge(o_a, o_b, o_c, o_m, h2, x, w_branch[l], w_out[l], ln_g[l], ln_b[l])


def kernel(x, mem, rel_bias, w_in, sgu_ln_g, sgu_ln_b, sgu_w, sgu_b, conv_w, conv_b, lru_wa, lru_ba, lru_wx,
           lru_bx, lru_lambda, cmp_pe_k, cmp_pe_v, cmp_w1_k, cmp_w1_v, cmp_w2_k, cmp_w2_v, nsa_gate_b,
           w_mem_kv, w_branch, w_out, ln_g, ln_b):
    bsz, seq, _ = x.shape
    tables = nsa_bias_tables(rel_bias, seq)
    xf = x.reshape(bsz * seq, D_MODEL)
    xb = xf.astype(BF16)
    mem_b = mem.reshape(-1, D_MODEL).astype(BF16)
    w_in_t = jnp.swapaxes(w_in, 1, 2)
    params = (w_in_t, sgu_ln_g, sgu_ln_b, sgu_w, sgu_b, conv_w, conv_b, lru_wa, lru_ba, lru_wx, lru_bx,
              lru_lambda, cmp_pe_k, cmp_pe_v, cmp_w1_k, cmp_w1_v, cmp_w2_k, cmp_w2_v, nsa_gate_b,
              w_mem_kv, w_branch, w_out, ln_g, ln_b)
    for l in range(w_in.shape[0]):
        xf, xb = layer(l, xf, xb, mem_b, tables, bsz, seq, *params)
    return xf.reshape(bsz, seq, D_MODEL)
```
